```python
import math
import jax, jax.numpy as jnp
from jax import lax
import numpy as np

D_MODEL = 2048
BATCH = 1
SEQ = 8192
DEPTH = 2

SSM_WIDTH = D_MODEL // 2
SSM_GROUP = 16
SSM_GROUPS = SSM_WIDTH // SSM_GROUP
SSM_STATE = 64
DT_MIN = 0.001
DT_MAX = 0.1
ATTN_HEADS = 8
QK_DIM = 64
V_DIM = 2 * QK_DIM
ATTN_WIDTH = ATTN_HEADS * V_DIM
QK_COLS = ATTN_HEADS * 2 * QK_DIM
Q_BLOCK = 128
REL_BUCKETS = 32
REL_MAX_DIST = 128
N_EXPERTS = 16
CAPACITY_FACTOR = 2
EXPERT_FF = D_MODEL
NORM_EPS = 1e-6
IN_COLS = SSM_WIDTH + 2 * QK_COLS + ATTN_WIDTH + 2 * D_MODEL
SPLITS = [SSM_WIDTH,
          SSM_WIDTH + QK_COLS,
          SSM_WIDTH + 2 * QK_COLS,
          SSM_WIDTH + 2 * QK_COLS + ATTN_WIDTH,
          SSM_WIDTH + 2 * QK_COLS + ATTN_WIDTH + D_MODEL]

kernel_name = "hybrid_s5_diffattn_ec_moe_encoder"

F32 = jnp.float32


def rms_norm(x, gain):
    xf = x.astype(F32)
    y = xf * lax.rsqrt(jnp.mean(xf * xf, axis=-1, keepdims=True) + NORM_EPS)
    return (y * gain.astype(F32)).astype(x.dtype)


def t5_bucket(rel):
    half = REL_BUCKETS // 2
    exact = half // 2
    side = jnp.where(rel > 0, half, 0).astype(jnp.int32)
    n = jnp.abs(rel)
    nf = jnp.maximum(n, 1).astype(F32)
    large = exact + (jnp.log(nf / exact) / math.log(REL_MAX_DIST / exact)
                     * (half - exact)).astype(jnp.int32)
    large = jnp.minimum(large, half - 1)
    return side + jnp.where(n < exact, n, large).astype(jnp.int32)


def ssm_combine(c1, c2):
    a1r, a1i, b1r, b1i = c1
    a2r, a2i, b2r, b2i = c2
    return (a2r * a1r - a2i * a1i,
            a2r * a1i + a2i * a1r,
            a2r * b1r - a2i * b1i + b2r,
            a2r * b1i + a2i * b1r + b2i)


def s5_direction(u, a_re, a_im, log_dt, b_re, b_im, c_re, c_im, reverse):
    a_re = a_re.astype(F32)
    a_im = a_im.astype(F32)
    dt = jnp.exp(log_dt.astype(F32))[:, None]
    mag = jnp.exp(a_re * dt)
    ang = a_im * dt
    abar_re = mag * jnp.cos(ang)
    abar_im = mag * jnp.sin(ang)
    den = a_re * a_re + a_im * a_im
    nr = abar_re - 1.0
    ni = abar_im
    coef_re = ((nr * a_re + ni * a_im) / den)[..., None]
    coef_im = ((ni * a_re - nr * a_im) / den)[..., None]
    b_re = b_re.astype(F32)
    b_im = b_im.astype(F32)
    bb_re = coef_re * b_re - coef_im * b_im
    bb_im = coef_re * b_im + coef_im * b_re
    bu_re = jnp.einsum('bsgp,gnp->bsgn', u, bb_re)
    bu_im = jnp.einsum('bsgp,gnp->bsgn', u, bb_im)
    shape = bu_re.shape
    elems = (jnp.broadcast_to(abar_re, shape), jnp.broadcast_to(abar_im, shape), bu_re, bu_im)
    _, _, s_re, s_im = lax.associative_scan(ssm_combine, elems, reverse=reverse, axis=1)
    return (jnp.einsum('bsgn,gpn->bsgp', s_re, c_re.astype(F32))
            - jnp.einsum('bsgn,gpn->bsgp', s_im, c_im.astype(F32)))


def diff_attention(q, k, v, rel_bias, lam):
    b, s, h = q.shape[0], q.shape[1], q.shape[2]
    nb = s // Q_BLOCK
    q = q * (QK_DIM ** -0.5)
    k1 = jnp.transpose(k[..., 0, :], (0, 2, 1, 3))
    k2 = jnp.transpose(k[..., 1, :], (0, 2, 1, 3))

    def to_blocks(t):
        t = jnp.transpose(t, (0, 2, 1, 3)).reshape(b, h, nb, Q_BLOCK, QK_DIM)
        return jnp.transpose(t, (2, 0, 1, 3, 4))

    q1b = to_blocks(q[..., 0, :])
    q2b = to_blocks(q[..., 1, :])
    k_pos = jnp.arange(s, dtype=jnp.int32)

    def block(args):
        q1_blk, q2_blk, blk = args
        q_pos = blk * Q_BLOCK + jnp.arange(Q_BLOCK, dtype=jnp.int32)
        bias = rel_bias[t5_bucket(k_pos[None, :] - q_pos[:, None])]
        bias = jnp.transpose(bias, (2, 0, 1)).astype(F32)[None]
        s1 = jnp.einsum('bhqd,bhkd->bhqk', q1_blk, k1).astype(F32) + bias
        s2 = jnp.einsum('bhqd,bhkd->bhqk', q2_blk, k2).astype(F32) + bias
        p = jax.nn.softmax(s1, axis=-1) - lam * jax.nn.softmax(s2, axis=-1)
        return jnp.einsum('bhqk,bhkd->bhqd', p.astype(v.dtype), v)

    o = lax.map(block, (q1b, q2b, jnp.arange(nb, dtype=jnp.int32)))
    return jnp.transpose(o, (1, 0, 3, 2, 4)).reshape(b, s, h, V_DIM)


def expert_choice_ffn(h, w_router, w_gate, w_up, w_down):
    b, s, d = h.shape
    cap = CAPACITY_FACTOR * s // N_EXPERTS
    aff = jax.nn.softmax(h.astype(F32) @ w_router.astype(F32), axis=-1)
    gate, idx = lax.top_k(jnp.swapaxes(aff, 1, 2), cap)
    xg = jax.vmap(lambda hb, ib: hb[ib])(h, idx)
    hid = (jax.nn.silu(jnp.einsum('becd,edf->becf', xg, w_gate))
           * jnp.einsum('becd,edf->becf', xg, w_up))
    y = jnp.einsum('becf,efd->becd', hid, w_down) * gate[..., None].astype(h.dtype)
    return jax.vmap(lambda yb, ib: jnp.zeros((s, d), yb.dtype)
                    .at[ib.reshape(-1)].add(yb.reshape(-1, d)))(y, idx)


def setup_inputs(seed: int = 0) -> dict:
    key = jax.random.key(seed)
    ks = jax.random.split(key, 32)
    L, G, N, P, E, F, D = DEPTH, SSM_GROUPS, SSM_STATE, SSM_GROUP, N_EXPERTS, EXPERT_FF, D_MODEL
    nrm = lambda k, shape, scale: jax.random.normal(k, shape, F32) * scale
    x = jax.random.normal(ks[0], (BATCH, SEQ, D), F32)
    w_in = nrm(ks[1], (L, D, IN_COLS), D ** -0.5)
    ssm_a_re = -0.5 * jnp.exp(nrm(ks[2], (L, 2, G, N), 0.05))
    ssm_a_im = math.pi * jnp.arange(N, dtype=F32) + nrm(ks[3], (L, 2, G, N), 0.01)
    ssm_log_dt = jax.random.uniform(ks[4], (L, 2, G), F32, math.log(DT_MIN), math.log(DT_MAX))
    ssm_b_re = nrm(ks[5], (L, 2, G, N, P), (2.0 * P) ** -0.5)
    ssm_b_im = nrm(ks[6], (L, 2, G, N, P), (2.0 * P) ** -0.5)
    ssm_c_re = nrm(ks[7], (L, 2, G, P, N), N ** -0.5)
    ssm_c_im = nrm(ks[8], (L, 2, G, P, N), N ** -0.5)
    ssm_d = nrm(ks[9], (L, SSM_WIDTH), 1.0)
    w_glu = nrm(ks[10], (L, SSM_WIDTH, SSM_WIDTH), SSM_WIDTH ** -0.5)
    w_ssm_branch = nrm(ks[11], (L, SSM_WIDTH, D), SSM_WIDTH ** -0.5)
    q_gain = 1.0 + nrm(ks[12], (L, QK_DIM), 0.02)
    k_gain = 1.0 + nrm(ks[13], (L, QK_DIM), 0.02)
    lambda_q1 = nrm(ks[14], (L, QK_DIM), 0.1)
    lambda_k1 = nrm(ks[15], (L, QK_DIM), 0.1)
    lambda_q2 = nrm(ks[16], (L, QK_DIM), 0.1)
    lambda_k2 = nrm(ks[17], (L, QK_DIM), 0.1)
    subln_gain = 1.0 + nrm(ks[18], (L, V_DIM), 0.02)
    w_attn_branch = nrm(ks[19], (L, ATTN_WIDTH, D), ATTN_WIDTH ** -0.5)
    rel_bias = nrm(ks[20], (REL_BUCKETS, ATTN_HEADS), 0.5)
    w_out = nrm(ks[21], (L, D, D), D ** -0.5)
    norm_mix = 1.0 + nrm(ks[22], (L, D), 0.02)
    norm_ffn = 1.0 + nrm(ks[23], (L, D), 0.02)
    w_router = nrm(ks[24], (L, D, E), D ** -0.5)
    w_expert_gate = nrm(ks[25], (L, E, D, F), D ** -0.5)
    w_expert_up = nrm(ks[26], (L, E, D, F), D ** -0.5)
    w_expert_down = nrm(ks[27], (L, E, F, D), F ** -0.5)
    return {"x": x, "w_in": w_in, "ssm_a_re": ssm_a_re, "ssm_a_im": ssm_a_im,
            "ssm_log_dt": ssm_log_dt, "ssm_b_re": ssm_b_re, "ssm_b_im": ssm_b_im,
            "ssm_c_re": ssm_c_re, "ssm_c_im": ssm_c_im, "ssm_d": ssm_d, "w_glu": w_glu,
            "w_ssm_branch": w_ssm_branch, "q_gain": q_gain, "k_gain": k_gain,
            "lambda_q1": lambda_q1, "lambda_k1": lambda_k1, "lambda_q2": lambda_q2,
            "lambda_k2": lambda_k2, "subln_gain": subln_gain, "w_attn_branch": w_attn_branch,
            "rel_bias": rel_bias, "w_out": w_out, "norm_mix": norm_mix, "norm_ffn": norm_ffn,
            "w_router": w_router, "w_expert_gate": w_expert_gate, "w_expert_up": w_expert_up,
            "w_expert_down": w_expert_down}


def reference(x, w_in, ssm_a_re, ssm_a_im, ssm_log_dt, ssm_b_re, ssm_b_im, ssm_c_re, ssm_c_im,
              ssm_d, w_glu, w_ssm_branch, q_gain, k_gain, lambda_q1, lambda_k1, lambda_q2,
              lambda_k2, subln_gain, w_attn_branch, rel_bias, w_out, norm_mix, norm_ffn,
              w_router, w_expert_gate, w_expert_up, w_expert_down):
    b, s, _ = x.shape
    for l in range(DEPTH):
        lam_init = 0.8 - 0.6 * math.exp(-0.3 * l)
        h = rms_norm(x, norm_mix[l])
        proj = h @ w_in[l]
        u, q, k, v, g_ssm, g_attn = jnp.split(proj, SPLITS, axis=-1)

        uf = u.astype(F32)
        ug = uf.reshape(b, s, SSM_GROUPS, SSM_GROUP)
        y = (s5_direction(ug, ssm_a_re[l, 0], ssm_a_im[l, 0], ssm_log_dt[l, 0],
                          ssm_b_re[l, 0], ssm_b_im[l, 0], ssm_c_re[l, 0], ssm_c_im[l, 0], False)
             + s5_direction(ug, ssm_a_re[l, 1], ssm_a_im[l, 1], ssm_log_dt[l, 1],
                            ssm_b_re[l, 1], ssm_b_im[l, 1], ssm_c_re[l, 1], ssm_c_im[l, 1], True))
        y = y.reshape(b, s, SSM_WIDTH) + ssm_d[l].astype(F32) * uf
        y = jax.nn.gelu(y).astype(x.dtype)
        y = y * jax.nn.sigmoid(y @ w_glu[l])
        ssm_out = y @ w_ssm_branch[l]

        qn = rms_norm(q.reshape(b, s, ATTN_HEADS, 2, QK_DIM), q_gain[l])
        kn = rms_norm(k.reshape(b, s, ATTN_HEADS, 2, QK_DIM), k_gain[l])
        vh = jnp.transpose(v.reshape(b, s, ATTN_HEADS, V_DIM), (0, 2, 1, 3))
        lam = (jnp.exp(jnp.sum(lambda_q1[l].astype(F32) * lambda_k1[l].astype(F32)))
               - jnp.exp(jnp.sum(lambda_q2[l].astype(F32) * lambda_k2[l].astype(F32)))
               + lam_init)
        o = diff_attention(qn, kn, vh, rel_bias, lam)
        o = rms_norm(o, subln_gain[l]) * (1.0 - lam_init)
        attn_out = o.reshape(b, s, ATTN_WIDTH) @ w_attn_branch[l]

        merged = jax.nn.sigmoid(g_ssm) * ssm_out + jax.nn.sigmoid(g_attn) * attn_out
        x = x + merged @ w_out[l]

        h2 = rms_norm(x, norm_ffn[l])
        x = x + expert_choice_ffn(h2, w_router[l], w_expert_gate[l], w_expert_up[l],
                                  w_expert_down[l])
    return x
```

```python
import functools
import math

import jax
import jax.numpy as jnp
import numpy as np
from jax import lax
from jax.experimental import pallas as pl
from jax.experimental.pallas import tpu as pltpu

F32 = jnp.float32
BF16 = jnp.bfloat16
I32 = jnp.int32

D_MODEL = 2048
DEPTH = 2
SSM_WIDTH = D_MODEL // 2
SSM_GROUP = 16
SSM_GROUPS = SSM_WIDTH // SSM_GROUP
SSM_STATE = 64
ATTN_HEADS = 8
QK_DIM = 64
V_DIM = 2 * QK_DIM
ATTN_WIDTH = ATTN_HEADS * V_DIM
QK_COLS = ATTN_HEADS * 2 * QK_DIM
REL_BUCKETS = 32
REL_MAX_DIST = 128
N_EXPERTS = 16
CAPACITY_FACTOR = 2
EXPERT_FF = D_MODEL
NORM_EPS = 1e-6
IN_COLS = SSM_WIDTH + 2 * QK_COLS + ATTN_WIDTH + 2 * D_MODEL
COL_Q = SSM_WIDTH
COL_K = COL_Q + QK_COLS
COL_V = COL_K + QK_COLS
COL_GS = COL_V + ATTN_WIDTH
COL_GA = COL_GS + D_MODEL

V7X_LANES = 128
F32_SUBLANES = 8
V7X_VMEM_BYTES = 64 * 1024 * 1024
VMEM_LIMIT = 56 * 1024 * 1024
LOG2E = 1.4426950408889634

S5_CHUNK = 16
S5_GB = 8
NEG_BIG = -1e30


def _cp(*sem):
    return pltpu.CompilerParams(dimension_semantics=sem, vmem_limit_bytes=VMEM_LIMIT)


def _norm_matmul_kernel(x_ref, g_ref, w_ref, o_ref, h_ref):
    @pl.when(pl.program_id(1) == 0)
    def _():
        x = x_ref[...]
        r = lax.rsqrt(jnp.mean(x * x, axis=-1, keepdims=True) + NORM_EPS)
        h_ref[...] = (x * r * g_ref[...]).astype(BF16)

    o_ref[...] = jnp.dot(h_ref[...], w_ref[...], preferred_element_type=F32).astype(o_ref.dtype)


def norm_matmul(x, gain, w, *, tm, tn):
    s, d = x.shape
    n = w.shape[1]
    return pl.pallas_call(
        _norm_matmul_kernel,
        out_shape=jax.ShapeDtypeStruct((s, n), BF16),
        grid=(s // tm, n // tn),
        in_specs=[pl.BlockSpec((tm, d), lambda i, j: (i, 0)),
                  pl.BlockSpec((1, d), lambda i, j: (0, 0)),
                  pl.BlockSpec((d, tn), lambda i, j: (0, j))],
        out_specs=pl.BlockSpec((tm, tn), lambda i, j: (i, j)),
        scratch_shapes=[pltpu.VMEM((tm, d), BF16)],
        compiler_params=_cp("parallel", "arbitrary"),
        name="norm_in_proj",
    )(x, gain.reshape(1, d).astype(F32), w)


def _qk_norm_kernel(q_ref, k_ref, gq_ref, gk_ref, qo_ref, ko_ref):
    lane = lax.broadcasted_iota(I32, (1, V7X_LANES), 1)
    lo_mask = lane < QK_DIM

    def norm(src_ref, g_ref, dst_ref):
        g = g_ref[...]
        for a in range(src_ref.shape[1] // V7X_LANES):
            x = src_ref[:, a * V7X_LANES:(a + 1) * V7X_LANES].astype(F32)
            ss = x * x
            lo = jnp.sum(jnp.where(lo_mask, ss, 0.0), axis=-1, keepdims=True)
            hi = jnp.sum(jnp.where(lo_mask, 0.0, ss), axis=-1, keepdims=True)
            ms = jnp.where(lo_mask, lo, hi) * (1.0 / QK_DIM)
            y = x * lax.rsqrt(ms + NORM_EPS) * g
            dst_ref[:, a * V7X_LANES:(a + 1) * V7X_LANES] = y.astype(dst_ref.dtype)

    norm(q_ref, gq_ref, qo_ref)
    norm(k_ref, gk_ref, ko_ref)


def qk_norm(proj, q_gain, k_gain, *, tm):
    s = proj.shape[0]
    gq = (jnp.tile(q_gain.astype(F32), 2) * (QK_DIM ** -0.5 * LOG2E)).reshape(1, V7X_LANES)
    gk = jnp.tile(k_gain.astype(F32), 2).reshape(1, V7X_LANES)
    cq, ck = COL_Q // QK_COLS, COL_K // QK_COLS
    return pl.pallas_call(
        _qk_norm_kernel,
        out_shape=(jax.ShapeDtypeStruct((s, QK_COLS), BF16), jax.ShapeDtypeStruct((s, QK_COLS), BF16)),
        grid=(s // tm,),
        in_specs=[pl.BlockSpec((tm, QK_COLS), lambda i: (i, cq)),
                  pl.BlockSpec((tm, QK_COLS), lambda i: (i, ck)),
                  pl.BlockSpec((1, V7X_LANES), lambda i: (0, 0)),
                  pl.BlockSpec((1, V7X_LANES), lambda i: (0, 0))],
        out_specs=(pl.BlockSpec((tm, QK_COLS), lambda i: (i, 0)),
                   pl.BlockSpec((tm, QK_COLS), lambda i: (i, 0))),
        compiler_params=_cp("parallel"),
        name="qk_norm",
    )(proj, proj, gq, gk)


def _bias_tile_kernel(rb_ref, o_ref, *, tk, tq):
    h = pl.program_id(0)
    d = pl.program_id(1)
    kl = lax.broadcasted_iota(I32, (tk, tq), 0)
    ql = lax.broadcasted_iota(I32, (tk, tq), 1)
    rel = (d - 1) * tk + kl - ql
    half = REL_BUCKETS // 2
    exact = half // 2
    side = jnp.where(rel > 0, half, 0).astype(I32)
    n = jnp.abs(rel)
    nf = jnp.maximum(n, 1).astype(F32)
    large = exact + (jnp.log(nf / exact) / math.log(REL_MAX_DIST / exact) * (half - exact)).astype(I32)
    large = jnp.minimum(large, half - 1)
    bucket = side + jnp.where(n < exact, n, large).astype(I32)
    val = jnp.zeros((tk, tq), F32)
    for b in range(REL_BUCKETS):
        val = jnp.where(bucket == b, rb_ref[b, h], val)
    o_ref[0, 0] = val * LOG2E


def bias_tiles(rel_bias, *, tk, tq):
    return pl.pallas_call(
        functools.partial(_bias_tile_kernel, tk=tk, tq=tq),
        out_shape=jax.ShapeDtypeStruct((ATTN_HEADS, 3, tk, tq), F32),
        grid=(ATTN_HEADS, 3),
        in_specs=[pl.BlockSpec(memory_space=pltpu.SMEM)],
        out_specs=pl.BlockSpec((1, 1, tk, tq), lambda h, d: (h, d, 0, 0)),
        compiler_params=_cp("parallel", "parallel"),
        name="t5_bias_tiles",
    )(rel_bias.astype(F32))


def _attn_kernel(lam_ref, far_ref, q_ref, k_ref, vt_ref, bias_ref, g_ref, o_ref,
                 m_ref, l_ref, acc_ref, *, nk, tk, tq, out_scale):
    h = pl.program_id(0)
    i = pl.program_id(1)
    lane = lax.broadcasted_iota(I32, (1, V7X_LANES), 1)
    q = q_ref[...]
    zero = jnp.zeros_like(q)
    qmaps = (jnp.where(lane < QK_DIM, q, zero), jnp.where(lane < QK_DIM, zero, q))
    m_ref[...] = jnp.full(m_ref.shape, NEG_BIG, F32)
    l_ref[...] = jnp.zeros(l_ref.shape, F32)
    acc_ref[...] = jnp.zeros(acc_ref.shape, F32)

    def update(j, bias_tile, bias_const):
        kt = k_ref[0, j]
        vt = vt_ref[0, j]
        for c in range(2):
            s = lax.dot_general(kt, qmaps[c], (((1,), (1,)), ((), ())),
                                preferred_element_type=F32)
            if bias_tile is not None:
                s = s + bias_tile
            m_old = m_ref[c:c + 1, :]
            m_new = jnp.maximum(m_old, jnp.max(s, axis=0, keepdims=True) + bias_const)
            alpha = jnp.exp2(m_old - m_new)
            p = jnp.exp2(s - (m_new - bias_const))
            l_ref[c:c + 1, :] = alpha * l_ref[c:c + 1, :] + jnp.sum(p, axis=0, keepdims=True)
            acc_ref[c] = alpha * acc_ref[c] + jnp.dot(vt, p.astype(BF16), preferred_element_type=F32)
            m_ref[c:c + 1, :] = m_new

    def left_body(j, carry):
        update(j, None, far_ref[h, 0])
        return carry

    def right_body(j, carry):
        update(j, None, far_ref[h, 1])
        return carry

    lax.fori_loop(0, jnp.maximum(i - 1, 0), left_body, 0)
    for dd in range(3):
        j = i + dd - 1

        @pl.when(jnp.logical_and(j >= 0, j < nk))
        def _():
            update(j, bias_ref[0, dd], 0.0)

    lax.fori_loop(jnp.minimum(i + 2, nk), nk, right_body, 0)

    o1 = acc_ref[0] / l_ref[0:1, :]
    o2 = acc_ref[1] / l_ref[1:2, :]
    o = o1 - lam_ref[0] * o2
    r = lax.rsqrt(jnp.mean(o * o, axis=0, keepdims=True) + NORM_EPS)
    o = o * r * g_ref[...] * out_scale
    o_ref[...] = o.T.astype(o_ref.dtype)


def diff_attention(qn, kn, proj, bias, far_bias, lam, subln_gain, *, lam_init, tq, tk):
    s = qn.shape[0]
    nk = s // tk
    assert tq == tk
    k4 = kn.reshape(nk, tk, ATTN_HEADS, V_DIM).transpose(2, 0, 1, 3)
    v = lax.slice(proj, (0, COL_V), (s, COL_V + ATTN_WIDTH))
    vt4 = v.reshape(nk, tk, ATTN_HEADS, V_DIM).transpose(2, 0, 3, 1)
    kern = functools.partial(_attn_kernel, nk=nk, tk=tk, tq=tq, out_scale=1.0 - lam_init)
    return pl.pallas_call(
        kern,
        out_shape=jax.ShapeDtypeStruct((s, ATTN_WIDTH), BF16),
        grid=(ATTN_HEADS, s // tq),
        in_specs=[pl.BlockSpec(memory_space=pltpu.SMEM),
                  pl.BlockSpec(memory_space=pltpu.SMEM),
                  pl.BlockSpec((tq, V_DIM), lambda h, i: (i, h)),
                  pl.BlockSpec((1, nk, tk, V_DIM), lambda h, i: (h, 0, 0, 0)),
                  pl.BlockSpec((1, nk, V_DIM, tk), lambda h, i: (h, 0, 0, 0)),
                  pl.BlockSpec((1, 3, tk, tq), lambda h, i: (h, 0, 0, 0)),
                  pl.BlockSpec((V_DIM, 1), lambda h, i: (0, 0))],
        out_specs=pl.BlockSpec((tq, V_DIM), lambda h, i: (i, h)),
        scratch_shapes=[pltpu.VMEM((8, tq), F32), pltpu.VMEM((8, tq), F32),
                        pltpu.VMEM((2, V_DIM, tq), F32)],
        compiler_params=_cp("parallel", "arbitrary"),
        name="diff_attention",
    )(lam.reshape(1).astype(F32), far_bias, qn, k4, vt4, bias, subln_gain.reshape(V_DIM, 1).astype(F32))


def s5_operators(a_re, a_im, log_dt, b_re, b_im, c_re, c_im):
    t_len, hp = S5_CHUNK, lax.Precision.HIGHEST
    a_re, a_im = a_re.astype(F32), a_im.astype(F32)
    dt = jnp.exp(log_dt.astype(F32))[..., None]
    steps = jnp.arange(t_len + 1, dtype=F32)[:, None, None, None]
    mag = jnp.exp(a_re * dt * steps)
    ang = a_im * dt * steps
    pw_re, pw_im = mag * jnp.cos(ang), mag * jnp.sin(ang)
    den = a_re * a_re + a_im * a_im
    nr, ni = pw_re[1] - 1.0, pw_im[1]
    coef_re = ((nr * a_re + ni * a_im) / den)[..., None]
    coef_im = ((ni * a_re - nr * a_im) / den)[..., None]
    b_re, b_im = b_re.astype(F32), b_im.astype(F32)
    bb_re = coef_re * b_re - coef_im * b_im
    bb_im = coef_re * b_im + coef_im * b_re
    c_re, c_im = c_re.astype(F32), c_im.astype(F32)

    cp_re = c_re[None] * pw_re[:, :, :, None, :] - c_im[None] * pw_im[:, :, :, None, :]
    cp_im = c_re[None] * pw_im[:, :, :, None, :] + c_im[None] * pw_re[:, :, :, None, :]
    kern = (jnp.einsum('ldgqn,dgnp->ldgqp', cp_re[:t_len], bb_re, precision=hp)
            - jnp.einsum('ldgqn,dgnp->ldgqp', cp_im[:t_len], bb_im, precision=hp))
    jj = jnp.arange(t_len)[:, None]
    tt = jnp.arange(t_len)[None, :]
    lag_f = jnp.clip(tt - jj, 0, t_len - 1)
    lag_b = jnp.clip(jj - tt, 0, t_len - 1)
    toep = (jnp.where((tt >= jj)[:, :, None, None, None], kern[lag_f, 0], 0.0)
            + jnp.where((jj >= tt)[:, :, None, None, None], kern[lag_b, 1], 0.0))
    g = toep.shape[2]
    tp = t_len * SSM_GROUP
    toep = jnp.transpose(toep, (2, 0, 4, 1, 3)).reshape(g, tp, tp)

    def state_map(d, exps):
        pr, pi = pw_re[exps, d], pw_im[exps, d]
        m_re = pr[..., None] * bb_re[d][None] - pi[..., None] * bb_im[d][None]
        m_im = pr[..., None] * bb_im[d][None] + pi[..., None] * bb_re[d][None]
        m_re = jnp.transpose(m_re, (1, 0, 3, 2)).reshape(g, tp, SSM_STATE)
        m_im = jnp.transpose(m_im, (1, 0, 3, 2)).reshape(g, tp, SSM_STATE)
        return [m_re, m_im, m_im, m_re]

    w = jnp.concatenate([toep] + state_map(0, jnp.arange(t_len - 1, -1, -1))
                        + state_map(1, jnp.arange(t_len)), axis=-1).astype(BF16)

    def out_map(d, exps):
        wr, wi = cp_re[exps, d], cp_im[exps, d]
        wr = jnp.transpose(wr, (1, 3, 0, 2)).reshape(g, SSM_STATE, tp)
        wi = jnp.transpose(wi, (1, 3, 0, 2)).reshape(g, SSM_STATE, tp)
        return [wr, -wi]

    mc = jnp.concatenate(out_map(0, jnp.arange(1, t_len + 1))
                         + out_map(1, jnp.arange(t_len, 0, -1)), axis=1).astype(BF16)

    def carry(d):
        ar, ai = pw_re[t_len, d], pw_im[t_len, d]
        return [jnp.concatenate([ar, ar], -1), jnp.concatenate([-ai, ai], -1),
                jnp.concatenate([ai, -ai], -1)]

    coef = jnp.stack(carry(0) + carry(1), axis=0)
    return w, mc, coef


def _gelu_tanh(x):
    return 0.5 * x * (1.0 + jnp.tanh(math.sqrt(2.0 / math.pi) * (x + 0.044715 * (x * x * x))))


def _s5_kernel(u_ref, w_ref, mc_ref, coef_ref, d_ref, y_ref, ef_ref, efs_ref, eb_ref, ebs_ref, *, cn):
    gb = u_ref.shape[0]
    tp = u_ref.shape[2]
    n2 = 2 * SSM_STATE
    for g in range(gb):
        e = jnp.dot(u_ref[g], w_ref[g, :, tp:], preferred_element_type=F32)
        for r, ref in enumerate((ef_ref, efs_ref, eb_ref, ebs_ref)):
            ref[pl.ds(g, cn, stride=gb), :] = e[:, r * n2:(r + 1) * n2]

    cf, cfs, cfw = coef_ref[0], coef_ref[1], coef_ref[2]
    cb, cbs, cbw = coef_ref[3], coef_ref[4], coef_ref[5]

    def step(c, carry):
        s, sw, r, rw = carry
        fo = pl.multiple_of(c * gb, gb)
        bo = pl.multiple_of((cn - 1 - c) * gb, gb)
        e, es = ef_ref[pl.ds(fo, gb), :], efs_ref[pl.ds(fo, gb), :]
        ef_ref[pl.ds(fo, gb), :] = s
        s, sw = cf * s + cfs * sw + e, cf * sw + cfw * s + es
        e, es = eb_ref[pl.ds(bo, gb), :], ebs_ref[pl.ds(bo, gb), :]
        eb_ref[pl.ds(bo, gb), :] = r
        r, rw = cb * r + cbs * rw + e, cb * rw + cbw * r + es
        return s, sw, r, rw

    z = jnp.zeros((gb, n2), F32)
    lax.fori_loop(0, cn, step, (z, z, z, z))

    for g in range(gb):
        u = u_ref[g]
        st = jnp.concatenate([ef_ref[pl.ds(g, cn, stride=gb), :], eb_ref[pl.ds(g, cn, stride=gb), :]],
                             axis=1).astype(BF16)
        y = (jnp.dot(u, w_ref[g, :, :tp], preferred_element_type=F32)
             + jnp.dot(st, mc_ref[g], preferred_element_type=F32)
             + u.astype(F32) * d_ref[g])
        y_ref[g] = _gelu_tanh(y).astype(y_ref.dtype)


def s5_mixer(proj, ops, ssm_d):
    w, mc, coef = ops
    s = proj.shape[0]
    g, p, t_len = SSM_GROUPS, SSM_GROUP, S5_CHUNK
    cn, tp = s // t_len, t_len * p
    u = lax.slice(proj, (0, 0), (s, SSM_WIDTH))
    ug = u.reshape(cn, t_len, g, p).transpose(2, 0, 1, 3).reshape(g, cn, tp)
    dsk = jnp.tile(ssm_d.astype(F32).reshape(g, 1, p), (1, 1, t_len))
    gb = S5_GB
    yg = pl.pallas_call(
        functools.partial(_s5_kernel, cn=cn),
        out_shape=jax.ShapeDtypeStruct((g, cn, tp), BF16),
        grid=(g // gb,),
        in_specs=[pl.BlockSpec((gb, cn, tp), lambda i: (i, 0, 0)),
                  pl.BlockSpec((gb, tp, w.shape[2]), lambda i: (i, 0, 0)),
                  pl.BlockSpec((gb, mc.shape[1], tp), lambda i: (i, 0, 0)),
                  pl.BlockSpec((6, gb, 2 * SSM_STATE), lambda i: (0, i, 0)),
                  pl.BlockSpec((gb, 1, tp), lambda i: (i, 0, 0))],
        out_specs=pl.BlockSpec((gb, cn, tp), lambda i: (i, 0, 0)),
        scratch_shapes=[pltpu.VMEM((cn * gb, 2 * SSM_STATE), F32) for _ in range(4)],
        compiler_params=_cp("parallel"),
        name="s5_chunked_scan",
    )(ug, w, mc, coef, dsk)
    return yg.reshape(g, cn, t_len, p).transpose(1, 2, 0, 3).reshape(s, SSM_WIDTH)


def _sigmoid(x):
    return 1.0 / (1.0 + jnp.exp(-x))


def _glu_kernel(y_ref, w_ref, o_ref):
    y = y_ref[...]
    z = jnp.dot(y, w_ref[...], preferred_element_type=F32)
    o_ref[...] = (y.astype(F32) * _sigmoid(z)).astype(o_ref.dtype)


def glu(y, w, *, tm):
    s, d = y.shape
    return pl.pallas_call(
        _glu_kernel,
        out_shape=jax.ShapeDtypeStruct((s, d), BF16),
        grid=(s // tm,),
        in_specs=[pl.BlockSpec((tm, d), lambda i: (i, 0)),
                  pl.BlockSpec((d, d), lambda i: (0, 0))],
        out_specs=pl.BlockSpec((tm, d), lambda i: (i, 0)),
        compiler_params=_cp("parallel"),
        name="half_glu",
    )(y, w)


def _merge_kernel(yg_ref, ao_ref, gs_ref, ga_ref, ws_ref, wa_ref, o_ref):
    a = jnp.dot(yg_ref[...], ws_ref[...], preferred_element_type=F32)
    b = jnp.dot(ao_ref[...], wa_ref[...], preferred_element_type=F32)
    o = _sigmoid(gs_ref[...].astype(F32)) * a + _sigmoid(ga_ref[...].astype(F32)) * b
    o_ref[...] = o.astype(o_ref.dtype)


def gated_merge(yg, ao, proj, ws, wa, *, tm, tn):
    s, k = yg.shape
    n = ws.shape[1]
    cs, ca = COL_GS // tn, COL_GA // tn
    return pl.pallas_call(
        _merge_kernel,
        out_shape=jax.ShapeDtypeStruct((s, n), BF16),
        grid=(s // tm, n // tn),
        in_specs=[pl.BlockSpec((tm, k), lambda i, j: (i, 0)),
                  pl.BlockSpec((tm, k), lambda i, j: (i, 0)),
                  pl.BlockSpec((tm, tn), lambda i, j: (i, cs + j)),
                  pl.BlockSpec((tm, tn), lambda i, j: (i, ca + j)),
                  pl.BlockSpec((k, tn), lambda i, j: (0, j)),
                  pl.BlockSpec((k, tn), lambda i, j: (0, j))],
        out_specs=pl.BlockSpec((tm, tn), lambda i, j: (i, j)),
        compiler_params=_cp("parallel", "arbitrary"),
        name="gated_merge",
    )(yg, ao, proj, proj, ws, wa)


def _out_router_kernel(m_ref, x_ref, w_ref, g_ref, wr_ref, wrt_ref, xo_ref, h_ref, aff_ref, afft_ref):
    x1 = x_ref[...] + jnp.dot(m_ref[...], w_ref[...], preferred_element_type=F32)
    xo_ref[...] = x1
    r = lax.rsqrt(jnp.mean(x1 * x1, axis=-1, keepdims=True) + NORM_EPS)
    h = x1 * r * g_ref[...]
    h_ref[...] = h
    hp = lax.Precision.HIGHEST
    lg = jnp.dot(h, wr_ref[...], preferred_element_type=F32, precision=hp)
    e = jnp.exp(lg - jnp.max(lg, axis=-1, keepdims=True))
    aff_ref[...] = e / jnp.sum(e, axis=-1, keepdims=True)
    lgt = lax.dot_general(wrt_ref[...], h, (((1,), (1,)), ((), ())),
                          preferred_element_type=F32, precision=hp)
    et = jnp.exp(lgt - jnp.max(lgt, axis=0, keepdims=True))
    afft_ref[...] = et / jnp.sum(et, axis=0, keepdims=True)


def out_proj_router(merged, x, w_out, gain, w_router, *, tm):
    s, d = x.shape
    e = w_router.shape[1]
    wr = w_router.astype(F32)
    return pl.pallas_call(
        _out_router_kernel,
        out_shape=(jax.ShapeDtypeStruct((s, d), F32), jax.ShapeDtypeStruct((s, d), F32),
                   jax.ShapeDtypeStruct((s, e), F32), jax.ShapeDtypeStruct((e, s), F32)),
        grid=(s // tm,),
        in_specs=[pl.BlockSpec((tm, d), lambda i: (i, 0)),
                  pl.BlockSpec((tm, d), lambda i: (i, 0)),
                  pl.BlockSpec((d, d), lambda i: (0, 0)),
                  pl.BlockSpec((1, d), lambda i: (0, 0)),
                  pl.BlockSpec((d, e), lambda i: (0, 0)),
                  pl.BlockSpec((e, d), lambda i: (0, 0))],
        out_specs=(pl.BlockSpec((tm, d), lambda i: (i, 0)),
                   pl.BlockSpec((tm, d), lambda i: (i, 0)),
                   pl.BlockSpec((tm, e), lambda i: (i, 0)),
                   pl.BlockSpec((e, tm), lambda i: (0, i))),
        compiler_params=_cp("parallel"),
        name="out_proj_router",
    )(merged, x, w_out, gain.reshape(1, d).astype(F32), wr, wr.T)


def _select_kernel(afft_ref, slot_ref, cum_ref, idx_ref, acc_ref, *, s, cap, blk):
    ne = afft_ref.shape[0]

    def bit_body(b, thr):
        cand = thr | jnp.left_shift(jnp.ones((ne, 1), I32), 30 - b)
        keys = pltpu.bitcast(afft_ref[...], I32)
        cnt = jnp.sum((keys >= cand).astype(I32), axis=1, keepdims=True)
        return jnp.where(cnt >= cap, cand, thr)

    thr = lax.fori_loop(0, 31, bit_body, jnp.zeros((ne, 1), I32))
    keys = pltpu.bitcast(afft_ref[...], I32)
    need = cap - jnp.sum((keys > thr).astype(I32), axis=1, keepdims=True)

    ri = lax.broadcasted_iota(I32, (blk, blk), 0)
    ci = lax.broadcasted_iota(I32, (blk, blk), 1)
    upper = jnp.where(ri < ci, 1.0, 0.0).astype(BF16)
    jcol = lax.broadcasted_iota(I32, (cap, 1), 0)
    trow = lax.broadcasted_iota(I32, (blk, V7X_LANES), 0)
    tlane = lax.broadcasted_iota(I32, (blk, V7X_LANES), 1)
    acc_ref[...] = jnp.zeros(acc_ref.shape, F32)

    def blk_body(b, carry):
        ceq, csel = carry
        off = pl.multiple_of(b * blk, blk)
        kb = pltpu.bitcast(afft_ref[:, pl.ds(off, blk)], I32)
        gt = kb > thr
        eq = kb == thr
        eqf = jnp.where(eq, 1.0, 0.0)
        rank_eq = jnp.dot(eqf.astype(BF16), upper, preferred_element_type=F32) + ceq
        sel = jnp.logical_or(gt, jnp.logical_and(eq, rank_eq < need.astype(F32)))
        self_ = jnp.where(sel, 1.0, 0.0)
        cum = jnp.dot(self_.astype(BF16), upper, preferred_element_type=F32) + csel
        cum_i = cum.astype(I32)
        cum_ref[:, pl.ds(off, blk)] = cum_i
        slot = jnp.where(sel, cum_i, -1)
        slot_ref[:, pl.ds(off, blk)] = slot
        tok = off + trow
        digits = jnp.where(tlane == 0, tok // V7X_LANES, jnp.where(tlane == 1, tok % V7X_LANES, 0))
        digits = digits.astype(F32).astype(BF16)
        for e in range(ne):
            oh = jnp.where(slot[e:e + 1, :] == jcol, 1.0, 0.0).astype(BF16)
            acc_ref[e] += jnp.dot(oh, digits, preferred_element_type=F32)
        return (ceq + jnp.sum(eqf, axis=1, keepdims=True), csel + jnp.sum(self_, axis=1, keepdims=True))

    z = jnp.zeros((ne, 1), F32)
    lax.fori_loop(0, s // blk, blk_body, (z, z))
    a = acc_ref[...]
    idx_ref[...] = (a[:, :, 0:1] * float(V7X_LANES) + a[:, :, 1:2]).astype(I32)


def expert_select(afft, *, cap, blk):
    ne, s = afft.shape
    return pl.pallas_call(
        functools.partial(_select_kernel, s=s, cap=cap, blk=blk),
        out_shape=(jax.ShapeDtypeStruct((ne, s), I32), jax.ShapeDtypeStruct((ne, s), I32),
                   jax.ShapeDtypeStruct((ne, cap, 1), I32)),
        scratch_shapes=[pltpu.VMEM((ne, cap, V7X_LANES), F32)],
        compiler_params=pltpu.CompilerParams(vmem_limit_bytes=VMEM_LIMIT),
        name="expert_select",
    )(afft)


def _ffn_kernel(idx_ref, h_hbm, wg_ref, wu_ref, wd_ref, y_ref, xg32_ref, xg_ref, sem, *, cap):
    e = pl.program_id(0)
    f = pl.program_id(1)

    def row_copy(r):
        tok = idx_ref[e * cap + r]
        return pltpu.make_async_copy(h_hbm.at[pl.ds(tok, 1)], xg32_ref.at[pl.ds(r, 1)], sem)

    @pl.when(f == 0)
    def _():
        def start(r, c):
            row_copy(r).start()
            return c

        def wait(r, c):
            row_copy(r).wait()
            return c

        lax.fori_loop(0, cap, start, 0)
        y_ref[...] = jnp.zeros(y_ref.shape, F32)
        lax.fori_loop(0, cap, wait, 0)
        xg_ref[...] = xg32_ref[...].astype(BF16)

    xg = xg_ref[...]
    a = jnp.dot(xg, wg_ref[0].astype(BF16), preferred_element_type=F32)
    b = jnp.dot(xg, wu_ref[0].astype(BF16), preferred_element_type=F32)
    hid = (a * _sigmoid(a) * b).astype(BF16)
    y_ref[0, 0:cap, :] += jnp.dot(hid, wd_ref[0].astype(BF16), preferred_element_type=F32)


def expert_ffn(idx, h2, wg, wu, wd, *, cap, pad, fc):
    ne, d, ff = wg.shape
    grid_spec = pltpu.PrefetchScalarGridSpec(
        num_scalar_prefetch=1,
        grid=(ne, ff // fc),
        in_specs=[pl.BlockSpec(memory_space=pl.ANY),
                  pl.BlockSpec((1, d, fc), lambda e, f, idx: (e, 0, f)),
                  pl.BlockSpec((1, d, fc), lambda e, f, idx: (e, 0, f)),
                  pl.BlockSpec((1, fc, d), lambda e, f, idx: (e, f, 0))],
        out_specs=pl.BlockSpec((1, cap + pad, d), lambda e, f, idx: (e, 0, 0)),
        scratch_shapes=[pltpu.VMEM((cap, d), F32), pltpu.VMEM((cap, d), BF16),
                        pltpu.SemaphoreType.DMA(())],
    )
    return pl.pallas_call(
        functools.partial(_ffn_kernel, cap=cap),
        out_shape=jax.ShapeDtypeStruct((ne, cap + pad, d), F32),
        grid_spec=grid_spec,
        compiler_params=_cp("arbitrary", "arbitrary"),
        name="expert_ffn",
    )(idx.reshape(-1), h2, wg, wu, wd)


def _combine_kernel(st_ref, x_ref, aff_ref, slot_ref, y_hbm, o_ref, buf_ref, sem, *, ne, rows):
    t = pl.program_id(0)
    tm = x_ref.shape[0]
    lane_r = lax.broadcasted_iota(I32, (1, rows), 1)

    def chunk_start(e, c):
        st8 = (st_ref[t * ne + e] // F32_SUBLANES) * F32_SUBLANES
        return pl.multiple_of(st8 + c * rows, F32_SUBLANES)

    def chunk_copy(e, c):
        return pltpu.make_async_copy(y_hbm.at[e, pl.ds(chunk_start(e, c), rows)], buf_ref.at[e], sem.at[e])

    for e in range(ne):
        chunk_copy(e, 0).start()
    o_ref[...] = x_ref[...]

    def add_chunk(e, c):
        rel = slot_ref[:, e:e + 1] - chunk_start(e, c)
        oh = jnp.where(rel == lane_r, 1.0, 0.0).astype(BF16)
        contrib = jnp.dot(oh, buf_ref[e].astype(BF16), preferred_element_type=F32)
        o_ref[...] += aff_ref[:, e:e + 1] * contrib

    for e in range(ne):
        chunk_copy(e, 0).wait()
        add_chunk(e, 0)
        end = st_ref[(t + 1) * ne + e]
        for c in range(1, tm // rows + 1):

            @pl.when(end > chunk_start(e, c))
            def _():
                cp = chunk_copy(e, c)
                cp.start()
                cp.wait()
                add_chunk(e, c)


def moe_combine(starts, x1, aff, slot, ye, *, tm, rows):
    s, d = x1.shape
    ne = aff.shape[1]
    grid_spec = pltpu.PrefetchScalarGridSpec(
        num_scalar_prefetch=1,
        grid=(s // tm,),
        in_specs=[pl.BlockSpec((tm, d), lambda t, st: (t, 0)),
                  pl.BlockSpec((tm, ne), lambda t, st: (t, 0)),
                  pl.BlockSpec((tm, ne), lambda t, st: (t, 0)),
                  pl.BlockSpec(memory_space=pl.ANY)],
        out_specs=pl.BlockSpec((tm, d), lambda t, st: (t, 0)),
        scratch_shapes=[pltpu.VMEM((ne, rows, d), F32), pltpu.SemaphoreType.DMA((ne,))],
    )
    return pl.pallas_call(
        functools.partial(_combine_kernel, ne=ne, rows=rows),
        out_shape=jax.ShapeDtypeStruct((s, d), F32),
        grid_spec=grid_spec,
        compiler_params=_cp("arbitrary"),
        name="moe_combine",
    )(starts.reshape(-1), x1, aff, slot, ye)


TM_PROJ, TN_PROJ = 1024, 1024
TM_ROW = 256
ATTN_TILE = 512
SEL_BLK = 256
FFN_CHUNK = 256
COMBINE_ROWS = 64


def _layer(x, l, p, bias, far_bias):
    s = x.shape[0]
    lam_init = 0.8 - 0.6 * math.exp(-0.3 * l)
    proj = norm_matmul(x, p["norm_mix"], p["w_in"].astype(BF16), tm=min(TM_PROJ, s), tn=TN_PROJ)

    ops = s5_operators(p["ssm_a_re"], p["ssm_a_im"], p["ssm_log_dt"], p["ssm_b_re"], p["ssm_b_im"],
                       p["ssm_c_re"], p["ssm_c_im"])
    y = s5_mixer(proj, ops, p["ssm_d"])
    yg = glu(y, p["w_glu"].astype(BF16), tm=min(TM_PROJ, s))

    qn, kn = qk_norm(proj, p["q_gain"], p["k_gain"], tm=TM_ROW)
    lam = (jnp.exp(jnp.sum(p["lambda_q1"].astype(F32) * p["lambda_k1"].astype(F32)))
           - jnp.exp(jnp.sum(p["lambda_q2"].astype(F32) * p["lambda_k2"].astype(F32))) + lam_init)
    ao = diff_attention(qn, kn, proj, bias, far_bias, lam, p["subln_gain"], lam_init=lam_init,
                        tq=ATTN_TILE, tk=ATTN_TILE)

    merged = gated_merge(yg, ao, proj, p["w_ssm_branch"].astype(BF16), p["w_attn_branch"].astype(BF16),
                         tm=min(TM_PROJ, s), tn=TN_PROJ)
    x1, h2, aff, afft = out_proj_router(merged, x, p["w_out"].astype(BF16), p["norm_ffn"], p["w_router"],
                                        tm=TM_ROW)

    cap = CAPACITY_FACTOR * s // N_EXPERTS
    slot_t, cum_t, idx = expert_select(afft, cap=cap, blk=SEL_BLK)
    ye = expert_ffn(idx, h2, p["w_expert_gate"], p["w_expert_up"], p["w_expert_down"],
                    cap=cap, pad=COMBINE_ROWS, fc=FFN_CHUNK)
    starts = jnp.concatenate([cum_t[:, ::TM_ROW].T, jnp.full((1, N_EXPERTS), cap, I32)], axis=0)
    return moe_combine(starts, x1, aff, slot_t.T, ye, tm=TM_ROW, rows=COMBINE_ROWS)


_LAYER_PARAMS = ("w_in", "ssm_a_re", "ssm_a_im", "ssm_log_dt", "ssm_b_re", "ssm_b_im", "ssm_c_re",
                 "ssm_c_im", "ssm_d", "w_glu", "w_ssm_branch", "q_gain", "k_gain", "lambda_q1",
                 "lambda_k1", "lambda_q2", "lambda_k2", "subln_gain", "w_attn_branch", "w_out",
                 "norm_mix", "norm_ffn", "w_router", "w_expert_gate", "w_expert_up", "w_expert_down")


def kernel(x, w_in, ssm_a_re, ssm_a_im, ssm_log_dt, ssm_b_re, ssm_b_im, ssm_c_re, ssm_c_im, ssm_d, w_glu, w_ssm_branch, q_gain, k_gain, lambda_q1, lambda_k1, lambda_q2, lambda_k2, subln_gain, w_attn_branch, rel_bias, w_out, norm_mix, norm_ffn, w_router, w_expert_gate, w_expert_up, w_expert_down):
    args = dict(locals())
    b = x.shape[0]
    bias = bias_tiles(rel_bias, tk=ATTN_TILE, tq=ATTN_TILE)
    half = REL_BUCKETS // 2
    far_bias = jnp.stack([rel_bias[half - 1], rel_bias[REL_BUCKETS - 1]], axis=1).astype(F32) * LOG2E
    outs = []
    for bi in range(b):
        xb = x[bi].astype(F32)
        for l in range(DEPTH):
            xb = _layer(xb, l, {k: args[k][l] for k in _LAYER_PARAMS}, bias, far_bias)
        outs.append(xb)
    return jnp.stack(outs, axis=0).astype(x.dtype)
```

```python
import functools
import math

import jax
import jax.numpy as jnp
import numpy as np
from jax import lax
from jax.experimental import pallas as pl
from jax.experimental.pallas import tpu as pltpu

F32 = jnp.float32
BF16 = jnp.bfloat16
I32 = jnp.int32

D_MODEL = 2048
DEPTH = 2
SSM_WIDTH = D_MODEL // 2
SSM_GROUP = 16
SSM_GROUPS = SSM_WIDTH // SSM_GROUP
SSM_STATE = 64
ATTN_HEADS = 8
QK_DIM = 64
V_DIM = 2 * QK_DIM
ATTN_WIDTH = ATTN_HEADS * V_DIM
QK_COLS = ATTN_HEADS * 2 * QK_DIM
REL_BUCKETS = 32
REL_MAX_DIST = 128
N_EXPERTS = 16
CAPACITY_FACTOR = 2
EXPERT_FF = D_MODEL
NORM_EPS = 1e-6
IN_COLS = SSM_WIDTH + 2 * QK_COLS + ATTN_WIDTH + 2 * D_MODEL
COL_Q = SSM_WIDTH
COL_K = COL_Q + QK_COLS
COL_V = COL_K + QK_COLS
COL_GS = COL_V + ATTN_WIDTH
COL_GA = COL_GS + D_MODEL

V7X_LANES = 128
F32_SUBLANES = 8
V7X_VMEM_BYTES = 64 * 1024 * 1024
VMEM_LIMIT = 56 * 1024 * 1024
LOG2E = 1.4426950408889634

S5_CHUNK = 16
S5_GB = 8
NEG_BIG = -1e30


def _cp(*sem):
    return pltpu.CompilerParams(dimension_semantics=sem, vmem_limit_bytes=VMEM_LIMIT)


def _norm_matmul_kernel(x_ref, g_ref, w_ref, o_ref, h_ref):
    @pl.when(pl.program_id(1) == 0)
    def _():
        x = x_ref[...]
        r = lax.rsqrt(jnp.mean(x * x, axis=-1, keepdims=True) + NORM_EPS)
        h_ref[...] = (x * r * g_ref[...]).astype(BF16)

    o_ref[...] = jnp.dot(h_ref[...], w_ref[...], preferred_element_type=F32).astype(o_ref.dtype)


def norm_matmul(x, gain, w, *, tm, tn):
    s, d = x.shape
    n = w.shape[1]
    return pl.pallas_call(
        _norm_matmul_kernel,
        out_shape=jax.ShapeDtypeStruct((s, n), BF16),
        grid=(s // tm, n // tn),
        in_specs=[pl.BlockSpec((tm, d), lambda i, j: (i, 0)),
                  pl.BlockSpec((1, d), lambda i, j: (0, 0)),
                  pl.BlockSpec((d, tn), lambda i, j: (0, j))],
        out_specs=pl.BlockSpec((tm, tn), lambda i, j: (i, j)),
        scratch_shapes=[pltpu.VMEM((tm, d), BF16)],
        compiler_params=_cp("parallel", "arbitrary"),
        name="norm_in_proj",
    )(x, gain.reshape(1, d).astype(F32), w)


def _qk_norm_kernel(q_ref, k_ref, gq_ref, gk_ref, qo_ref, ko_ref):
    lane = lax.broadcasted_iota(I32, (1, V7X_LANES), 1)
    lo_mask = lane < QK_DIM

    def norm(src_ref, g_ref, dst_ref):
        g = g_ref[...]
        for a in range(src_ref.shape[1] // V7X_LANES):
            x = src_ref[:, a * V7X_LANES:(a + 1) * V7X_LANES].astype(F32)
            ss = x * x
            lo = jnp.sum(jnp.where(lo_mask, ss, 0.0), axis=-1, keepdims=True)
            hi = jnp.sum(jnp.where(lo_mask, 0.0, ss), axis=-1, keepdims=True)
            ms = jnp.where(lo_mask, lo, hi) * (1.0 / QK_DIM)
            y = x * lax.rsqrt(ms + NORM_EPS) * g
            dst_ref[:, a * V7X_LANES:(a + 1) * V7X_LANES] = y.astype(dst_ref.dtype)

    norm(q_ref, gq_ref, qo_ref)
    norm(k_ref, gk_ref, ko_ref)


def qk_norm(proj, q_gain, k_gain, *, tm):
    s = proj.shape[0]
    gq = (jnp.tile(q_gain.astype(F32), 2) * (QK_DIM ** -0.5 * LOG2E)).reshape(1, V7X_LANES)
    gk = jnp.tile(k_gain.astype(F32), 2).reshape(1, V7X_LANES)
    cq, ck = COL_Q // QK_COLS, COL_K // QK_COLS
    return pl.pallas_call(
        _qk_norm_kernel,
        out_shape=(jax.ShapeDtypeStruct((s, QK_COLS), BF16), jax.ShapeDtypeStruct((s, QK_COLS), BF16)),
        grid=(s // tm,),
        in_specs=[pl.BlockSpec((tm, QK_COLS), lambda i: (i, cq)),
                  pl.BlockSpec((tm, QK_COLS), lambda i: (i, ck)),
                  pl.BlockSpec((1, V7X_LANES), lambda i: (0, 0)),
                  pl.BlockSpec((1, V7X_LANES), lambda i: (0, 0))],
        out_specs=(pl.BlockSpec((tm, QK_COLS), lambda i: (i, 0)),
                   pl.BlockSpec((tm, QK_COLS), lambda i: (i, 0))),
        compiler_params=_cp("parallel"),
        name="qk_norm",
    )(proj, proj, gq, gk)


ATTN_NEAR = 3
ATTN_BIAS_TILES = 5
ATTN_UNROLL = 4


def _bias_tile_kernel(rb_ref, o_ref, *, tk, tq):
    h = pl.program_id(0)
    d = pl.program_id(1)
    kl = lax.broadcasted_iota(I32, (tk, tq), 0)
    ql = lax.broadcasted_iota(I32, (tk, tq), 1)
    rel = (d - ATTN_BIAS_TILES // 2) * tk + kl - ql
    half = REL_BUCKETS // 2
    exact = half // 2
    side = jnp.where(rel > 0, half, 0).astype(I32)
    n = jnp.abs(rel)
    nf = jnp.maximum(n, 1).astype(F32)
    large = exact + (jnp.log(nf / exact) / math.log(REL_MAX_DIST / exact) * (half - exact)).astype(I32)
    large = jnp.minimum(large, half - 1)
    bucket = side + jnp.where(n < exact, n, large).astype(I32)
    val = jnp.zeros((tk, tq), F32)
    for b in range(REL_BUCKETS):
        val = jnp.where(bucket == b, rb_ref[b, h], val)
    o_ref[0, 0] = val * LOG2E


def bias_tiles(rel_bias, *, tk, tq):
    return pl.pallas_call(
        functools.partial(_bias_tile_kernel, tk=tk, tq=tq),
        out_shape=jax.ShapeDtypeStruct((ATTN_HEADS, ATTN_BIAS_TILES, tk, tq), F32),
        grid=(ATTN_HEADS, ATTN_BIAS_TILES),
        in_specs=[pl.BlockSpec(memory_space=pltpu.SMEM)],
        out_specs=pl.BlockSpec((1, 1, tk, tq), lambda h, d: (h, d, 0, 0)),
        compiler_params=_cp("parallel", "parallel"),
        name="t5_bias_tiles",
    )(rel_bias.astype(F32))


def _attn_kernel(lam_ref, far_ref, q_ref, k_ref, vt_ref, bias_ref, g_ref, o_ref,
                 s_ref, cm_ref, m_ref, l_ref, acc_ref, *, nk, tk, tq, out_scale):
    h = pl.program_id(0)
    i = pl.program_id(1)
    lane = lax.broadcasted_iota(I32, (1, V7X_LANES), 1)
    q = q_ref[...]
    zero = jnp.zeros_like(q)
    qmaps = (jnp.where(lane < QK_DIM, q, zero), jnp.where(lane < QK_DIM, zero, q))
    m_ref[...] = jnp.full(m_ref.shape, NEG_BIG, F32)
    l_ref[...] = jnp.zeros(l_ref.shape, F32)
    acc_ref[...] = jnp.zeros(acc_ref.shape, F32)

    def scores(j, slot, bias_tile, bias_const):
        kt = k_ref[0, j]
        for c in range(2):
            s = lax.dot_general(kt, qmaps[c], (((1,), (1,)), ((), ())),
                                preferred_element_type=F32)
            if bias_tile is not None:
                s = s + bias_tile
            s_ref[slot, c] = s
            cm_ref[slot, c:c + 1, :] = jnp.max(s, axis=0, keepdims=True) + bias_const
        cm_ref[slot, 2:3, :] = jnp.zeros((1, tq), F32) + bias_const

    def absorb(j, slot):
        vt = vt_ref[0, j]
        cb = cm_ref[slot, 2:3, :]
        for c in range(2):
            m_old = m_ref[c:c + 1, :]
            m_new = jnp.maximum(m_old, cm_ref[slot, c:c + 1, :])
            alpha = jnp.exp2(m_old - m_new)
            p = jnp.exp2(s_ref[slot, c] - (m_new - cb))
            l_ref[c:c + 1, :] = alpha * l_ref[c:c + 1, :] + jnp.sum(p, axis=0, keepdims=True)
            acc_ref[c] = alpha * acc_ref[c] + jnp.dot(vt, p.astype(BF16), preferred_element_type=F32)
            m_ref[c:c + 1, :] = m_new

    n0 = jnp.clip(i - 1, 0, nk - ATTN_NEAR)
    nfar = nk - ATTN_NEAR

    def far_tile(t):
        return jnp.where(t < n0, t, t + ATTN_NEAR)

    def far_const(j):
        return jnp.where(j < i, far_ref[h, 0], far_ref[h, 1])

    def near_bias(j):
        return bias_ref[0, j - i + ATTN_BIAS_TILES // 2]

    scores(n0, 0, near_bias(n0), 0.0)
    scores(n0 + 1, 1, near_bias(n0 + 1), 0.0)
    absorb(n0, 0)
    scores(n0 + 2, 0, near_bias(n0 + 2), 0.0)
    absorb(n0 + 1, 1)
    j0 = far_tile(0)
    scores(j0, 1, None, far_const(j0))
    absorb(n0 + 2, 0)

    def group(u, jprev):
        for w in range(ATTN_UNROLL):
            jn = far_tile(ATTN_UNROLL * u + w + 1)
            scores(jn, w % 2, None, far_const(jn))
            absorb(jprev, (w + 1) % 2)
            jprev = jn
        return jprev

    jlast = lax.fori_loop(0, (nfar - 1) // ATTN_UNROLL, group, j0)
    absorb(jlast, 1)

    o1 = acc_ref[0] / l_ref[0:1, :]
    o2 = acc_ref[1] / l_ref[1:2, :]
    o = o1 - lam_ref[0] * o2
    r = lax.rsqrt(jnp.mean(o * o, axis=0, keepdims=True) + NORM_EPS)
    o = o * r * g_ref[...] * out_scale
    o_ref[...] = o.T.astype(o_ref.dtype)


def diff_attention(qn, kn, proj, bias, far_bias, lam, subln_gain, *, lam_init, tq, tk):
    s = qn.shape[0]
    nk = s // tk
    assert tq == tk and nk > ATTN_NEAR and (nk - ATTN_NEAR - 1) % ATTN_UNROLL == 0
    k4 = kn.reshape(nk, tk, ATTN_HEADS, V_DIM).transpose(2, 0, 1, 3)
    v = lax.slice(proj, (0, COL_V), (s, COL_V + ATTN_WIDTH))
    vt4 = v.reshape(nk, tk, ATTN_HEADS, V_DIM).transpose(2, 0, 3, 1)
    kern = functools.partial(_attn_kernel, nk=nk, tk=tk, tq=tq, out_scale=1.0 - lam_init)
    return pl.pallas_call(
        kern,
        out_shape=jax.ShapeDtypeStruct((s, ATTN_WIDTH), BF16),
        grid=(ATTN_HEADS, s // tq),
        in_specs=[pl.BlockSpec(memory_space=pltpu.SMEM),
                  pl.BlockSpec(memory_space=pltpu.SMEM),
                  pl.BlockSpec((tq, V_DIM), lambda h, i: (i, h)),
                  pl.BlockSpec((1, nk, tk, V_DIM), lambda h, i: (h, 0, 0, 0)),
                  pl.BlockSpec((1, nk, V_DIM, tk), lambda h, i: (h, 0, 0, 0)),
                  pl.BlockSpec((1, ATTN_BIAS_TILES, tk, tq), lambda h, i: (h, 0, 0, 0)),
                  pl.BlockSpec((V_DIM, 1), lambda h, i: (0, 0))],
        out_specs=pl.BlockSpec((tq, V_DIM), lambda h, i: (i, h)),
        scratch_shapes=[pltpu.VMEM((2, 2, tk, tq), F32),
                        pltpu.VMEM((2, F32_SUBLANES, tq), F32),
                        pltpu.VMEM((F32_SUBLANES, tq), F32),
                        pltpu.VMEM((F32_SUBLANES, tq), F32),
                        pltpu.VMEM((2, V_DIM, tq), F32)],
        compiler_params=_cp("parallel", "arbitrary"),
        name="diff_attention",
    )(lam.reshape(1).astype(F32), far_bias, qn, k4, vt4, bias, subln_gain.reshape(V_DIM, 1).astype(F32))


def s5_operators(a_re, a_im, log_dt, b_re, b_im, c_re, c_im):
    t_len, hp = S5_CHUNK, lax.Precision.HIGHEST
    a_re, a_im = a_re.astype(F32), a_im.astype(F32)
    dt = jnp.exp(log_dt.astype(F32))[..., None]
    steps = jnp.arange(t_len + 1, dtype=F32)[:, None, None, None]
    mag = jnp.exp(a_re * dt * steps)
    ang = a_im * dt * steps
    pw_re, pw_im = mag * jnp.cos(ang), mag * jnp.sin(ang)
    den = a_re * a_re + a_im * a_im
    nr, ni = pw_re[1] - 1.0, pw_im[1]
    coef_re = ((nr * a_re + ni * a_im) / den)[..., None]
    coef_im = ((ni * a_re - nr * a_im) / den)[..., None]
    b_re, b_im = b_re.astype(F32), b_im.astype(F32)
    bb_re = coef_re * b_re - coef_im * b_im
    bb_im = coef_re * b_im + coef_im * b_re
    c_re, c_im = c_re.astype(F32), c_im.astype(F32)

    cp_re = c_re[None] * pw_re[:, :, :, None, :] - c_im[None] * pw_im[:, :, :, None, :]
    cp_im = c_re[None] * pw_im[:, :, :, None, :] + c_im[None] * pw_re[:, :, :, None, :]
    kern = (jnp.einsum('ldgqn,dgnp->ldgqp', cp_re[:t_len], bb_re, precision=hp)
            - jnp.einsum('ldgqn,dgnp->ldgqp', cp_im[:t_len], bb_im, precision=hp))
    jj = jnp.arange(t_len)[:, None]
    tt = jnp.arange(t_len)[None, :]
    lag_f = jnp.clip(tt - jj, 0, t_len - 1)
    lag_b = jnp.clip(jj - tt, 0, t_len - 1)
    toep = (jnp.where((tt >= jj)[:, :, None, None, None], kern[lag_f, 0], 0.0)
            + jnp.where((jj >= tt)[:, :, None, None, None], kern[lag_b, 1], 0.0))
    g = toep.shape[2]
    tp = t_len * SSM_GROUP
    toep = jnp.transpose(toep, (2, 0, 4, 1, 3)).reshape(g, tp, tp)

    def state_map(d, exps):
        pr, pi = pw_re[exps, d], pw_im[exps, d]
        m_re = pr[..., None] * bb_re[d][None] - pi[..., None] * bb_im[d][None]
        m_im = pr[..., None] * bb_im[d][None] + pi[..., None] * bb_re[d][None]
        m_re = jnp.transpose(m_re, (1, 0, 3, 2)).reshape(g, tp, SSM_STATE)
        m_im = jnp.transpose(m_im, (1, 0, 3, 2)).reshape(g, tp, SSM_STATE)
        return [m_re, m_im, m_im, m_re]

    w = jnp.concatenate([toep] + state_map(0, jnp.arange(t_len - 1, -1, -1))
                        + state_map(1, jnp.arange(t_len)), axis=-1).astype(BF16)

    def out_map(d, exps):
        wr, wi = cp_re[exps, d], cp_im[exps, d]
        wr = jnp.transpose(wr, (1, 3, 0, 2)).reshape(g, SSM_STATE, tp)
        wi = jnp.transpose(wi, (1, 3, 0, 2)).reshape(g, SSM_STATE, tp)
        return [wr, -wi]

    mc = jnp.concatenate(out_map(0, jnp.arange(1, t_len + 1))
                         + out_map(1, jnp.arange(t_len, 0, -1)), axis=1).astype(BF16)

    def carry(d):
        ar, ai = pw_re[t_len, d], pw_im[t_len, d]
        return [jnp.concatenate([ar, ar], -1), jnp.concatenate([-ai, ai], -1),
                jnp.concatenate([ai, -ai], -1)]

    coef = jnp.stack(carry(0) + carry(1), axis=0)
    return w, mc, coef


def _gelu_tanh(x):
    return 0.5 * x * (1.0 + jnp.tanh(math.sqrt(2.0 / math.pi) * (x + 0.044715 * (x * x * x))))


def _s5_kernel(u_ref, w_ref, mc_ref, coef_ref, d_ref, y_ref, ef_ref, efs_ref, eb_ref, ebs_ref, *, cn):
    gb = u_ref.shape[0]
    tp = u_ref.shape[2]
    n2 = 2 * SSM_STATE
    for g in range(gb):
        e = jnp.dot(u_ref[g], w_ref[g, :, tp:], preferred_element_type=F32)
        for r, ref in enumerate((ef_ref, efs_ref, eb_ref, ebs_ref)):
            ref[pl.ds(g, cn, stride=gb), :] = e[:, r * n2:(r + 1) * n2]

    cf, cfs, cfw = coef_ref[0], coef_ref[1], coef_ref[2]
    cb, cbs, cbw = coef_ref[3], coef_ref[4], coef_ref[5]

    def step(c, carry):
        s, sw, r, rw = carry
        fo = pl.multiple_of(c * gb, gb)
        bo = pl.multiple_of((cn - 1 - c) * gb, gb)
        e, es = ef_ref[pl.ds(fo, gb), :], efs_ref[pl.ds(fo, gb), :]
        ef_ref[pl.ds(fo, gb), :] = s
        s, sw = cf * s + cfs * sw + e, cf * sw + cfw * s + es
        e, es = eb_ref[pl.ds(bo, gb), :], ebs_ref[pl.ds(bo, gb), :]
        eb_ref[pl.ds(bo, gb), :] = r
        r, rw = cb * r + cbs * rw + e, cb * rw + cbw * r + es
        return s, sw, r, rw

    z = jnp.zeros((gb, n2), F32)
    lax.fori_loop(0, cn, step, (z, z, z, z))

    for g in range(gb):
        u = u_ref[g]
        st = jnp.concatenate([ef_ref[pl.ds(g, cn, stride=gb), :], eb_ref[pl.ds(g, cn, stride=gb), :]],
                             axis=1).astype(BF16)
        y = (jnp.dot(u, w_ref[g, :, :tp], preferred_element_type=F32)
             + jnp.dot(st, mc_ref[g], preferred_element_type=F32)
             + u.astype(F32) * d_ref[g])
        y_ref[g] = _gelu_tanh(y).astype(y_ref.dtype)


def s5_mixer(proj, ops, ssm_d):
    w, mc, coef = ops
    s = proj.shape[0]
    g, p, t_len = SSM_GROUPS, SSM_GROUP, S5_CHUNK
    cn, tp = s // t_len, t_len * p
    u = lax.slice(proj, (0, 0), (s, SSM_WIDTH))
    ug = u.reshape(cn, t_len, g, p).transpose(2, 0, 1, 3).reshape(g, cn, tp)
    dsk = jnp.tile(ssm_d.astype(F32).reshape(g, 1, p), (1, 1, t_len))
    gb = S5_GB
    yg = pl.pallas_call(
        functools.partial(_s5_kernel, cn=cn),
        out_shape=jax.ShapeDtypeStruct((g, cn, tp), BF16),
        grid=(g // gb,),
        in_specs=[pl.BlockSpec((gb, cn, tp), lambda i: (i, 0, 0)),
                  pl.BlockSpec((gb, tp, w.shape[2]), lambda i: (i, 0, 0)),
                  pl.BlockSpec((gb, mc.shape[1], tp), lambda i: (i, 0, 0)),
                  pl.BlockSpec((6, gb, 2 * SSM_STATE), lambda i: (0, i, 0)),
                  pl.BlockSpec((gb, 1, tp), lambda i: (i, 0, 0))],
        out_specs=pl.BlockSpec((gb, cn, tp), lambda i: (i, 0, 0)),
        scratch_shapes=[pltpu.VMEM((cn * gb, 2 * SSM_STATE), F32) for _ in range(4)],
        compiler_params=_cp("parallel"),
        name="s5_chunked_scan",
    )(ug, w, mc, coef, dsk)
    return yg.reshape(g, cn, t_len, p).transpose(1, 2, 0, 3).reshape(s, SSM_WIDTH)


def _sigmoid(x):
    return 1.0 / (1.0 + jnp.exp(-x))


def _glu_kernel(y_ref, w_ref, o_ref):
    y = y_ref[...]
    z = jnp.dot(y, w_ref[...], preferred_element_type=F32)
    o_ref[...] = (y.astype(F32) * _sigmoid(z)).astype(o_ref.dtype)


def glu(y, w, *, tm):
    s, d = y.shape
    return pl.pallas_call(
        _glu_kernel,
        out_shape=jax.ShapeDtypeStruct((s, d), BF16),
        grid=(s // tm,),
        in_specs=[pl.BlockSpec((tm, d), lambda i: (i, 0)),
                  pl.BlockSpec((d, d), lambda i: (0, 0))],
        out_specs=pl.BlockSpec((tm, d), lambda i: (i, 0)),
        compiler_params=_cp("parallel"),
        name="half_glu",
    )(y, w)


def _merge_kernel(yg_ref, ao_ref, gs_ref, ga_ref, ws_ref, wa_ref, o_ref):
    a = jnp.dot(yg_ref[...], ws_ref[...], preferred_element_type=F32)
    b = jnp.dot(ao_ref[...], wa_ref[...], preferred_element_type=F32)
    o = _sigmoid(gs_ref[...].astype(F32)) * a + _sigmoid(ga_ref[...].astype(F32)) * b
    o_ref[...] = o.astype(o_ref.dtype)


def gated_merge(yg, ao, proj, ws, wa, *, tm, tn):
    s, k = yg.shape
    n = ws.shape[1]
    cs, ca = COL_GS // tn, COL_GA // tn
    return pl.pallas_call(
        _merge_kernel,
        out_shape=jax.ShapeDtypeStruct((s, n), BF16),
        grid=(s // tm, n // tn),
        in_specs=[pl.BlockSpec((tm, k), lambda i, j: (i, 0)),
                  pl.BlockSpec((tm, k), lambda i, j: (i, 0)),
                  pl.BlockSpec((tm, tn), lambda i, j: (i, cs + j)),
                  pl.BlockSpec((tm, tn), lambda i, j: (i, ca + j)),
                  pl.BlockSpec((k, tn), lambda i, j: (0, j)),
                  pl.BlockSpec((k, tn), lambda i, j: (0, j))],
        out_specs=pl.BlockSpec((tm, tn), lambda i, j: (i, j)),
        compiler_params=_cp("parallel", "arbitrary"),
        name="gated_merge",
    )(yg, ao, proj, proj, ws, wa)


def _out_router_kernel(m_ref, x_ref, w_ref, g_ref, wr_ref, wrt_ref, xo_ref, h_ref, aff_ref, afft_ref):
    x1 = x_ref[...] + jnp.dot(m_ref[...], w_ref[...], preferred_element_type=F32)
    xo_ref[...] = x1
    r = lax.rsqrt(jnp.mean(x1 * x1, axis=-1, keepdims=True) + NORM_EPS)
    h = x1 * r * g_ref[...]
    h_ref[...] = h
    hp = lax.Precision.HIGHEST
    lg = jnp.dot(h, wr_ref[...], preferred_element_type=F32, precision=hp)
    e = jnp.exp(lg - jnp.max(lg, axis=-1, keepdims=True))
    aff_ref[...] = e / jnp.sum(e, axis=-1, keepdims=True)
    lgt = lax.dot_general(wrt_ref[...], h, (((1,), (1,)), ((), ())),
                          preferred_element_type=F32, precision=hp)
    et = jnp.exp(lgt - jnp.max(lgt, axis=0, keepdims=True))
    afft_ref[...] = et / jnp.sum(et, axis=0, keepdims=True)


def out_proj_router(merged, x, w_out, gain, w_router, *, tm):
    s, d = x.shape
    e = w_router.shape[1]
    wr = w_router.astype(F32)
    return pl.pallas_call(
        _out_router_kernel,
        out_shape=(jax.ShapeDtypeStruct((s, d), F32), jax.ShapeDtypeStruct((s, d), F32),
                   jax.ShapeDtypeStruct((s, e), F32), jax.ShapeDtypeStruct((e, s), F32)),
        grid=(s // tm,),
        in_specs=[pl.BlockSpec((tm, d), lambda i: (i, 0)),
                  pl.BlockSpec((tm, d), lambda i: (i, 0)),
                  pl.BlockSpec((d, d), lambda i: (0, 0)),
                  pl.BlockSpec((1, d), lambda i: (0, 0)),
                  pl.BlockSpec((d, e), lambda i: (0, 0)),
                  pl.BlockSpec((e, d), lambda i: (0, 0))],
        out_specs=(pl.BlockSpec((tm, d), lambda i: (i, 0)),
                   pl.BlockSpec((tm, d), lambda i: (i, 0)),
                   pl.BlockSpec((tm, e), lambda i: (i, 0)),
                   pl.BlockSpec((e, tm), lambda i: (0, i))),
        compiler_params=_cp("parallel"),
        name="out_proj_router",
    )(merged, x, w_out, gain.reshape(1, d).astype(F32), wr, wr.T)


def _select_kernel(afft_ref, slot_ref, cum_ref, idx_ref, acc_ref, *, s, cap, blk):
    ne = afft_ref.shape[0]

    def bit_body(b, thr):
        cand = thr | jnp.left_shift(jnp.ones((ne, 1), I32), 30 - b)
        keys = pltpu.bitcast(afft_ref[...], I32)
        cnt = jnp.sum((keys >= cand).astype(I32), axis=1, keepdims=True)
        return jnp.where(cnt >= cap, cand, thr)

    thr = lax.fori_loop(0, 31, bit_body, jnp.zeros((ne, 1), I32))
    keys = pltpu.bitcast(afft_ref[...], I32)
    need = cap - jnp.sum((keys > thr).astype(I32), axis=1, keepdims=True)

    ri = lax.broadcasted_iota(I32, (blk, blk), 0)
    ci = lax.broadcasted_iota(I32, (blk, blk), 1)
    upper = jnp.where(ri < ci, 1.0, 0.0).astype(BF16)
    jcol = lax.broadcasted_iota(I32, (cap, 1), 0)
    trow = lax.broadcasted_iota(I32, (blk, V7X_LANES), 0)
    tlane = lax.broadcasted_iota(I32, (blk, V7X_LANES), 1)
    acc_ref[...] = jnp.zeros(acc_ref.shape, F32)

    def blk_body(b, carry):
        ceq, csel = carry
        off = pl.multiple_of(b * blk, blk)
        kb = pltpu.bitcast(afft_ref[:, pl.ds(off, blk)], I32)
        gt = kb > thr
        eq = kb == thr
        eqf = jnp.where(eq, 1.0, 0.0)
        rank_eq = jnp.dot(eqf.astype(BF16), upper, preferred_element_type=F32) + ceq
        sel = jnp.logical_or(gt, jnp.logical_and(eq, rank_eq < need.astype(F32)))
        self_ = jnp.where(sel, 1.0, 0.0)
        cum = jnp.dot(self_.astype(BF16), upper, preferred_element_type=F32) + csel
        cum_i = cum.astype(I32)
        cum_ref[:, pl.ds(off, blk)] = cum_i
        slot = jnp.where(sel, cum_i, -1)
        slot_ref[:, pl.ds(off, blk)] = slot
        tok = off + trow
        digits = jnp.where(tlane == 0, tok // V7X_LANES, jnp.where(tlane == 1, tok % V7X_LANES, 0))
        digits = digits.astype(F32).astype(BF16)
        for e in range(ne):
            oh = jnp.where(slot[e:e + 1, :] == jcol, 1.0, 0.0).astype(BF16)
            acc_ref[e] += jnp.dot(oh, digits, preferred_element_type=F32)
        return (ceq + jnp.sum(eqf, axis=1, keepdims=True), csel + jnp.sum(self_, axis=1, keepdims=True))

    z = jnp.zeros((ne, 1), F32)
    lax.fori_loop(0, s // blk, blk_body, (z, z))
    a = acc_ref[...]
    idx_ref[...] = (a[:, :, 0:1] * float(V7X_LANES) + a[:, :, 1:2]).astype(I32)


def expert_select(afft, *, cap, blk):
    ne, s = afft.shape
    return pl.pallas_call(
        functools.partial(_select_kernel, s=s, cap=cap, blk=blk),
        out_shape=(jax.ShapeDtypeStruct((ne, s), I32), jax.ShapeDtypeStruct((ne, s), I32),
                   jax.ShapeDtypeStruct((ne, cap, 1), I32)),
        scratch_shapes=[pltpu.VMEM((ne, cap, V7X_LANES), F32)],
        compiler_params=pltpu.CompilerParams(vmem_limit_bytes=VMEM_LIMIT),
        name="expert_select",
    )(afft)


GATHER_UNROLL = 8


def _ffn_kernel(idx_ref, h_hbm, wg_ref, wu_ref, wd_ref, y_ref, xg32_ref, xg_ref, sem, *, cap):
    e = pl.program_id(0)
    f = pl.program_id(1)

    def row_copy(r):
        tok = idx_ref[e * cap + r]
        return pltpu.make_async_copy(h_hbm.at[pl.ds(tok, 1)], xg32_ref.at[pl.ds(r, 1)], sem)

    @pl.when(f == 0)
    def _():
        def start(rb, c):
            for w in range(GATHER_UNROLL):
                row_copy(rb * GATHER_UNROLL + w).start()
            return c

        lax.fori_loop(0, cap // GATHER_UNROLL, start, 0)
        y_ref[...] = jnp.zeros(y_ref.shape, F32)
        pltpu.make_async_copy(h_hbm.at[pl.ds(0, cap)], xg32_ref, sem).wait()
        xg_ref[...] = xg32_ref[...].astype(BF16)

    xg = xg_ref[...]
    a = jnp.dot(xg, wg_ref[0, 0].astype(BF16), preferred_element_type=F32)
    b = jnp.dot(xg, wu_ref[0, 0].astype(BF16), preferred_element_type=F32)
    hid = (a * _sigmoid(a) * b).astype(BF16)
    y_ref[0, 0:cap, :] += jnp.dot(hid, wd_ref[0, 0].astype(BF16), preferred_element_type=F32)


def expert_ffn(idx, h2, wg, wu, wd, layer, *, cap, pad, fc):
    _, ne, d, ff = wg.shape
    grid_spec = pltpu.PrefetchScalarGridSpec(
        num_scalar_prefetch=1,
        grid=(ne, ff // fc),
        in_specs=[pl.BlockSpec(memory_space=pl.ANY),
                  pl.BlockSpec((1, 1, d, fc), lambda e, f, idx: (layer, e, 0, f)),
                  pl.BlockSpec((1, 1, d, fc), lambda e, f, idx: (layer, e, 0, f)),
                  pl.BlockSpec((1, 1, fc, d), lambda e, f, idx: (layer, e, f, 0))],
        out_specs=pl.BlockSpec((1, cap + pad, d), lambda e, f, idx: (e, 0, 0)),
        scratch_shapes=[pltpu.VMEM((cap, d), F32), pltpu.VMEM((cap, d), BF16),
                        pltpu.SemaphoreType.DMA(())],
    )
    return pl.pallas_call(
        functools.partial(_ffn_kernel, cap=cap),
        out_shape=jax.ShapeDtypeStruct((ne, cap + pad, d), F32),
        grid_spec=grid_spec,
        compiler_params=_cp("arbitrary", "arbitrary"),
        name="expert_ffn",
    )(idx.reshape(-1), h2, wg, wu, wd)


def _combine_kernel(st_ref, x_ref, aff_ref, slot_ref, y_hbm, o_ref, buf_ref, xbuf_ref, sem, xsem,
                    *, ne, rows, nt):
    t = pl.program_id(0)
    tm = x_ref.shape[0]

    def chunk_start(tt, e, c):
        st8 = (st_ref[tt * ne + e] // F32_SUBLANES) * F32_SUBLANES
        return pl.multiple_of(st8 + c * rows, F32_SUBLANES)

    def first_copy(tt, e, slot):
        return pltpu.make_async_copy(y_hbm.at[e, pl.ds(chunk_start(tt, e, 0), rows)],
                                     buf_ref.at[slot, e], sem.at[slot, e])

    @pl.when(t == 0)
    def _():
        for e in range(ne):
            first_copy(0, e, 0).start()

    @pl.when(t + 1 < nt)
    def _():
        for e in range(ne):
            first_copy(t + 1, e, (t + 1) % 2).start()

    slot = t % 2
    for e in range(ne):
        first_copy(t, e, slot).wait()

    pair = V7X_LANES // rows
    lane = lax.broadcasted_iota(I32, (1, V7X_LANES), 1)
    hi_parts, lo_parts = [], []
    for a in range(ne // pair):
        rel = jnp.zeros((tm, V7X_LANES), I32)
        gate = jnp.zeros((tm, V7X_LANES), F32)
        for b in range(pair):
            e = a * pair + b
            in_e = jnp.logical_and(lane >= b * rows, lane < (b + 1) * rows)
            rel = jnp.where(in_e, slot_ref[:, e:e + 1] - chunk_start(t, e, 0) + b * rows, rel)
            gate = jnp.where(in_e, aff_ref[:, e:e + 1], gate)
        gate = jnp.where(rel == lane, gate, 0.0)
        g_hi = gate.astype(BF16)
        hi_parts.append(g_hi)
        lo_parts.append((gate - g_hi.astype(F32)).astype(BF16))
    rhs = buf_ref[slot].reshape(ne * rows, x_ref.shape[1]).astype(BF16)
    o_ref[...] = (x_ref[...]
                  + jnp.dot(jnp.concatenate(hi_parts, axis=1), rhs, preferred_element_type=F32)
                  + jnp.dot(jnp.concatenate(lo_parts, axis=1), rhs, preferred_element_type=F32))

    lane_r = lax.broadcasted_iota(I32, (1, rows), 1)
    for e in range(ne):
        end = st_ref[(t + 1) * ne + e]
        for c in range(1, tm // rows + 1):

            @pl.when(end > chunk_start(t, e, c))
            def _():
                cp = pltpu.make_async_copy(y_hbm.at[e, pl.ds(chunk_start(t, e, c), rows)], xbuf_ref, xsem)
                cp.start()
                cp.wait()
                rel = slot_ref[:, e:e + 1] - chunk_start(t, e, c)
                oh = jnp.where(rel == lane_r, 1.0, 0.0).astype(BF16)
                contrib = jnp.dot(oh, xbuf_ref[...].astype(BF16), preferred_element_type=F32)
                o_ref[...] += aff_ref[:, e:e + 1] * contrib


def moe_combine(starts, x1, aff, slot, ye, *, tm, rows):
    s, d = x1.shape
    ne = aff.shape[1]
    nt = s // tm
    assert V7X_LANES % rows == 0 and ne % (V7X_LANES // rows) == 0
    grid_spec = pltpu.PrefetchScalarGridSpec(
        num_scalar_prefetch=1,
        grid=(nt,),
        in_specs=[pl.BlockSpec((tm, d), lambda t, st: (t, 0)),
                  pl.BlockSpec((tm, ne), lambda t, st: (t, 0)),
                  pl.BlockSpec((tm, ne), lambda t, st: (t, 0)),
                  pl.BlockSpec(memory_space=pl.ANY)],
        out_specs=pl.BlockSpec((tm, d), lambda t, st: (t, 0)),
        scratch_shapes=[pltpu.VMEM((2, ne, rows, d), F32), pltpu.VMEM((rows, d), F32),
                        pltpu.SemaphoreType.DMA((2, ne)), pltpu.SemaphoreType.DMA(())],
    )
    return pl.pallas_call(
        functools.partial(_combine_kernel, ne=ne, rows=rows, nt=nt),
        out_shape=jax.ShapeDtypeStruct((s, d), F32),
        grid_spec=grid_spec,
        compiler_params=_cp("arbitrary"),
        name="moe_combine",
    )(starts.reshape(-1), x1, aff, slot, ye)


TM_PROJ, TN_PROJ = 1024, 1024
TM_ROW = 256
ATTN_TILE = 512
SEL_BLK = 256
FFN_CHUNK = 256
COMBINE_ROWS = 64


def _layer(x, l, p, bias, far_bias):
    s = x.shape[0]
    lam_init = 0.8 - 0.6 * math.exp(-0.3 * l)
    proj = norm_matmul(x, p["norm_mix"], p["w_in"].astype(BF16), tm=min(TM_PROJ, s), tn=TN_PROJ)

    ops = s5_operators(p["ssm_a_re"], p["ssm_a_im"], p["ssm_log_dt"], p["ssm_b_re"], p["ssm_b_im"],
                       p["ssm_c_re"], p["ssm_c_im"])
    y = s5_mixer(proj, ops, p["ssm_d"])
    yg = glu(y, p["w_glu"].astype(BF16), tm=min(TM_PROJ, s))

    qn, kn = qk_norm(proj, p["q_gain"], p["k_gain"], tm=TM_ROW)
    lam = (jnp.exp(jnp.sum(p["lambda_q1"].astype(F32) * p["lambda_k1"].astype(F32)))
           - jnp.exp(jnp.sum(p["lambda_q2"].astype(F32) * p["lambda_k2"].astype(F32))) + lam_init)
    ao = diff_attention(qn, kn, proj, bias, far_bias, lam, p["subln_gain"], lam_init=lam_init,
                        tq=ATTN_TILE, tk=ATTN_TILE)

    merged = gated_merge(yg, ao, proj, p["w_ssm_branch"].astype(BF16), p["w_attn_branch"].astype(BF16),
                         tm=min(TM_PROJ, s), tn=TN_PROJ)
    x1, h2, aff, afft = out_proj_router(merged, x, p["w_out"].astype(BF16), p["norm_ffn"], p["w_router"],
                                        tm=TM_ROW)

    cap = CAPACITY_FACTOR * s // N_EXPERTS
    slot_t, cum_t, idx = expert_select(afft, cap=cap, blk=SEL_BLK)
    ye = expert_ffn(idx, h2, p["w_expert_gate"], p["w_expert_up"], p["w_expert_down"], l,
                    cap=cap, pad=COMBINE_ROWS, fc=FFN_CHUNK)
    starts = jnp.concatenate([cum_t[:, ::TM_ROW].T, jnp.full((1, N_EXPERTS), cap, I32)], axis=0)
    return moe_combine(starts, x1, aff, slot_t.T, ye, tm=TM_ROW, rows=COMBINE_ROWS)


_LAYER_PARAMS = ("w_in", "ssm_a_re", "ssm_a_im", "ssm_log_dt", "ssm_b_re", "ssm_b_im", "ssm_c_re",
                 "ssm_c_im", "ssm_d", "w_glu", "w_ssm_branch", "q_gain", "k_gain", "lambda_q1",
                 "lambda_k1", "lambda_q2", "lambda_k2", "subln_gain", "w_attn_branch", "w_out",
                 "norm_mix", "norm_ffn", "w_router")
_STACKED_PARAMS = ("w_expert_gate", "w_expert_up", "w_expert_down")


def kernel(x, w_in, ssm_a_re, ssm_a_im, ssm_log_dt, ssm_b_re, ssm_b_im, ssm_c_re, ssm_c_im, ssm_d, w_glu, w_ssm_branch, q_gain, k_gain, lambda_q1, lambda_k1, lambda_q2, lambda_k2, subln_gain, w_attn_branch, rel_bias, w_out, norm_mix, norm_ffn, w_router, w_expert_gate, w_expert_up, w_expert_down):
    args = dict(locals())
    b = x.shape[0]
    bias = bias_tiles(rel_bias, tk=ATTN_TILE, tq=ATTN_TILE)
    half = REL_BUCKETS // 2
    far_bias = jnp.stack([rel_bias[half - 1], rel_bias[REL_BUCKETS - 1]], axis=1).astype(F32) * LOG2E
    outs = []
    for bi in range(b):
        xb = x[bi].astype(F32)
        for l in range(DEPTH):
            p = {k: args[k][l] for k in _LAYER_PARAMS}
            p.update({k: args[k] for k in _STACKED_PARAMS})
            xb = _layer(xb, l, p, bias, far_bias)
        outs.append(xb)
    return jnp.stack(outs, axis=0).astype(x.dtype)
```

```python
import functools
import math

import jax
import jax.numpy as jnp
import numpy as np
from jax import lax
from jax.experimental import pallas as pl
from jax.experimental.pallas import tpu as pltpu

F32 = jnp.float32
BF16 = jnp.bfloat16
I32 = jnp.int32

D_MODEL = 2048
DEPTH = 2
SSM_WIDTH = D_MODEL // 2
SSM_GROUP = 16
SSM_GROUPS = SSM_WIDTH // SSM_GROUP
SSM_STATE = 64
ATTN_HEADS = 8
QK_DIM = 64
V_DIM = 2 * QK_DIM
ATTN_WIDTH = ATTN_HEADS * V_DIM
QK_COLS = ATTN_HEADS * 2 * QK_DIM
REL_BUCKETS = 32
REL_MAX_DIST = 128
N_EXPERTS = 16
CAPACITY_FACTOR = 2
EXPERT_FF = D_MODEL
NORM_EPS = 1e-6
IN_COLS = SSM_WIDTH + 2 * QK_COLS + ATTN_WIDTH + 2 * D_MODEL
COL_Q = SSM_WIDTH
COL_K = COL_Q + QK_COLS
COL_V = COL_K + QK_COLS
COL_GS = COL_V + ATTN_WIDTH
COL_GA = COL_GS + D_MODEL

V7X_LANES = 128
F32_SUBLANES = 8
V7X_VMEM_BYTES = 64 * 1024 * 1024
VMEM_LIMIT = 56 * 1024 * 1024
LOG2E = 1.4426950408889634

S5_CHUNK = 16
S5_GB = 8
NEG_BIG = -1e30


def _cp(*sem):
    return pltpu.CompilerParams(dimension_semantics=sem, vmem_limit_bytes=VMEM_LIMIT)


def _norm_matmul_kernel(x_ref, g_ref, w_ref, o_ref, h_ref):
    @pl.when(pl.program_id(1) == 0)
    def _():
        x = x_ref[...]
        r = lax.rsqrt(jnp.mean(x * x, axis=-1, keepdims=True) + NORM_EPS)
        h_ref[...] = (x * r * g_ref[...]).astype(BF16)

    o_ref[...] = jnp.dot(h_ref[...], w_ref[...], preferred_element_type=F32).astype(o_ref.dtype)


def norm_matmul(x, gain, w, *, tm, tn):
    s, d = x.shape
    n = w.shape[1]
    return pl.pallas_call(
        _norm_matmul_kernel,
        out_shape=jax.ShapeDtypeStruct((s, n), BF16),
        grid=(s // tm, n // tn),
        in_specs=[pl.BlockSpec((tm, d), lambda i, j: (i, 0)),
                  pl.BlockSpec((1, d), lambda i, j: (0, 0)),
                  pl.BlockSpec((d, tn), lambda i, j: (0, j))],
        out_specs=pl.BlockSpec((tm, tn), lambda i, j: (i, j)),
        scratch_shapes=[pltpu.VMEM((tm, d), BF16)],
        compiler_params=_cp("parallel", "arbitrary"),
        name="norm_in_proj",
    )(x, gain.reshape(1, d).astype(F32), w)


def _qk_norm_kernel(q_ref, k_ref, gq_ref, gk_ref, qo_ref, ko_ref):
    lane = lax.broadcasted_iota(I32, (1, V7X_LANES), 1)
    lo_mask = lane < QK_DIM

    def norm(src_ref, g_ref, dst_ref):
        g = g_ref[...]
        for a in range(src_ref.shape[1] // V7X_LANES):
            x = src_ref[:, a * V7X_LANES:(a + 1) * V7X_LANES].astype(F32)
            ss = x * x
            lo = jnp.sum(jnp.where(lo_mask, ss, 0.0), axis=-1, keepdims=True)
            hi = jnp.sum(jnp.where(lo_mask, 0.0, ss), axis=-1, keepdims=True)
            ms = jnp.where(lo_mask, lo, hi) * (1.0 / QK_DIM)
            y = x * lax.rsqrt(ms + NORM_EPS) * g
            dst_ref[:, a * V7X_LANES:(a + 1) * V7X_LANES] = y.astype(dst_ref.dtype)

    norm(q_ref, gq_ref, qo_ref)
    norm(k_ref, gk_ref, ko_ref)


def qk_norm(proj, q_gain, k_gain, *, tm):
    s = proj.shape[0]
    gq = (jnp.tile(q_gain.astype(F32), 2) * (QK_DIM ** -0.5 * LOG2E)).reshape(1, V7X_LANES)
    gk = jnp.tile(k_gain.astype(F32), 2).reshape(1, V7X_LANES)
    cq, ck = COL_Q // QK_COLS, COL_K // QK_COLS
    return pl.pallas_call(
        _qk_norm_kernel,
        out_shape=(jax.ShapeDtypeStruct((s, QK_COLS), BF16), jax.ShapeDtypeStruct((s, QK_COLS), BF16)),
        grid=(s // tm,),
        in_specs=[pl.BlockSpec((tm, QK_COLS), lambda i: (i, cq)),
                  pl.BlockSpec((tm, QK_COLS), lambda i: (i, ck)),
                  pl.BlockSpec((1, V7X_LANES), lambda i: (0, 0)),
                  pl.BlockSpec((1, V7X_LANES), lambda i: (0, 0))],
        out_specs=(pl.BlockSpec((tm, QK_COLS), lambda i: (i, 0)),
                   pl.BlockSpec((tm, QK_COLS), lambda i: (i, 0))),
        compiler_params=_cp("parallel"),
        name="qk_norm",
    )(proj, proj, gq, gk)


ATTN_NEAR = 3
ATTN_BIAS_TILES = 5
ATTN_UNROLL = 6


def _bias_tile_kernel(rb_ref, o_ref, *, tk, tq):
    h = pl.program_id(0)
    d = pl.program_id(1)
    kl = lax.broadcasted_iota(I32, (tk, tq), 0)
    ql = lax.broadcasted_iota(I32, (tk, tq), 1)
    rel = (d - ATTN_BIAS_TILES // 2) * tk + kl - ql
    half = REL_BUCKETS // 2
    exact = half // 2
    side = jnp.where(rel > 0, half, 0).astype(I32)
    n = jnp.abs(rel)
    nf = jnp.maximum(n, 1).astype(F32)
    large = exact + (jnp.log(nf / exact) / math.log(REL_MAX_DIST / exact) * (half - exact)).astype(I32)
    large = jnp.minimum(large, half - 1)
    bucket = side + jnp.where(n < exact, n, large).astype(I32)
    val = jnp.zeros((tk, tq), F32)
    for b in range(REL_BUCKETS):
        val = jnp.where(bucket == b, rb_ref[b, h], val)
    o_ref[0, 0] = val * LOG2E


def bias_tiles(rel_bias, *, tk, tq):
    return pl.pallas_call(
        functools.partial(_bias_tile_kernel, tk=tk, tq=tq),
        out_shape=jax.ShapeDtypeStruct((ATTN_HEADS, ATTN_BIAS_TILES, tk, tq), F32),
        grid=(ATTN_HEADS, ATTN_BIAS_TILES),
        in_specs=[pl.BlockSpec(memory_space=pltpu.SMEM)],
        out_specs=pl.BlockSpec((1, 1, tk, tq), lambda h, d: (h, d, 0, 0)),
        compiler_params=_cp("parallel", "parallel"),
        name="t5_bias_tiles",
    )(rel_bias.astype(F32))


def _attn_kernel(lam_ref, far_ref, q_ref, k_ref, vt_ref, bias_ref, g_ref, o_ref,
                 s_ref, cm_ref, m_ref, l_ref, acc_ref, *, nk, tk, tq, out_scale):
    h = pl.program_id(0)
    i = pl.program_id(1)
    lane = lax.broadcasted_iota(I32, (1, V7X_LANES), 1)
    q = q_ref[...]
    zero = jnp.zeros_like(q)
    qmaps = (jnp.where(lane < QK_DIM, q, zero), jnp.where(lane < QK_DIM, zero, q))
    m_ref[...] = jnp.full(m_ref.shape, NEG_BIG, F32)
    l_ref[...] = jnp.zeros(l_ref.shape, F32)
    acc_ref[...] = jnp.zeros(acc_ref.shape, F32)

    def scores(j, slot, bias_tile, bias_const):
        kt = k_ref[0, j]
        for c in range(2):
            s = lax.dot_general(kt, qmaps[c], (((1,), (1,)), ((), ())),
                                preferred_element_type=F32)
            if bias_tile is not None:
                s = s + bias_tile
            s_ref[slot, c] = s
            cm_ref[slot, c:c + 1, :] = jnp.max(s, axis=0, keepdims=True) + bias_const
        cm_ref[slot, 2:3, :] = jnp.zeros((1, tq), F32) + bias_const

    def absorb(j, slot):
        vt = vt_ref[0, j]
        cb = cm_ref[slot, 2:3, :]
        for c in range(2):
            m_old = m_ref[c:c + 1, :]
            m_new = jnp.maximum(m_old, cm_ref[slot, c:c + 1, :])
            alpha = jnp.exp2(m_old - m_new)
            p = jnp.exp2(s_ref[slot, c] - (m_new - cb))
            l_ref[c:c + 1, :] = alpha * l_ref[c:c + 1, :] + jnp.sum(p, axis=0, keepdims=True)
            acc_ref[c] = alpha * acc_ref[c] + jnp.dot(vt, p.astype(BF16), preferred_element_type=F32)
            m_ref[c:c + 1, :] = m_new

    n0 = jnp.clip(i - 1, 0, nk - ATTN_NEAR)
    nfar = nk - ATTN_NEAR

    def far_tile(t):
        return jnp.where(t < n0, t, t + ATTN_NEAR)

    def far_const(j):
        return jnp.where(j < i, far_ref[h, 0], far_ref[h, 1])

    def near_bias(j):
        return bias_ref[0, j - i + ATTN_BIAS_TILES // 2]

    scores(n0, 0, near_bias(n0), 0.0)
    scores(n0 + 1, 1, near_bias(n0 + 1), 0.0)
    absorb(n0, 0)
    scores(n0 + 2, 0, near_bias(n0 + 2), 0.0)
    absorb(n0 + 1, 1)
    j0 = far_tile(0)
    scores(j0, 1, None, far_const(j0))
    absorb(n0 + 2, 0)

    def group(u, jprev):
        for w in range(ATTN_UNROLL):
            jn = far_tile(ATTN_UNROLL * u + w + 1)
            scores(jn, w % 2, None, far_const(jn))
            absorb(jprev, (w + 1) % 2)
            jprev = jn
        return jprev

    jlast = lax.fori_loop(0, (nfar - 1) // ATTN_UNROLL, group, j0)
    absorb(jlast, 1)

    o1 = acc_ref[0] / l_ref[0:1, :]
    o2 = acc_ref[1] / l_ref[1:2, :]
    o = o1 - lam_ref[0] * o2
    r = lax.rsqrt(jnp.mean(o * o, axis=0, keepdims=True) + NORM_EPS)
    o = o * r * g_ref[...] * out_scale
    o_ref[...] = o.T.astype(o_ref.dtype)


def diff_attention(qn, kn, proj, bias, far_bias, lam, subln_gain, *, lam_init, tq, tk):
    s = qn.shape[0]
    nk = s // tk
    assert tq == tk and nk > ATTN_NEAR and (nk - ATTN_NEAR - 1) % ATTN_UNROLL == 0
    k4 = kn.reshape(nk, tk, ATTN_HEADS, V_DIM).transpose(2, 0, 1, 3)
    v = lax.slice(proj, (0, COL_V), (s, COL_V + ATTN_WIDTH))
    vt4 = v.reshape(nk, tk, ATTN_HEADS, V_DIM).transpose(2, 0, 3, 1)
    kern = functools.partial(_attn_kernel, nk=nk, tk=tk, tq=tq, out_scale=1.0 - lam_init)
    return pl.pallas_call(
        kern,
        out_shape=jax.ShapeDtypeStruct((s, ATTN_WIDTH), BF16),
        grid=(ATTN_HEADS, s // tq),
        in_specs=[pl.BlockSpec(memory_space=pltpu.SMEM),
                  pl.BlockSpec(memory_space=pltpu.SMEM),
                  pl.BlockSpec((tq, V_DIM), lambda h, i: (i, h)),
                  pl.BlockSpec((1, nk, tk, V_DIM), lambda h, i: (h, 0, 0, 0)),
                  pl.BlockSpec((1, nk, V_DIM, tk), lambda h, i: (h, 0, 0, 0)),
                  pl.BlockSpec((1, ATTN_BIAS_TILES, tk, tq), lambda h, i: (h, 0, 0, 0)),
                  pl.BlockSpec((V_DIM, 1), lambda h, i: (0, 0))],
        out_specs=pl.BlockSpec((tq, V_DIM), lambda h, i: (i, h)),
        scratch_shapes=[pltpu.VMEM((2, 2, tk, tq), F32),
                        pltpu.VMEM((2, F32_SUBLANES, tq), F32),
                        pltpu.VMEM((F32_SUBLANES, tq), F32),
                        pltpu.VMEM((F32_SUBLANES, tq), F32),
                        pltpu.VMEM((2, V_DIM, tq), F32)],
        compiler_params=_cp("parallel", "arbitrary"),
        name="diff_attention",
    )(lam.reshape(1).astype(F32), far_bias, qn, k4, vt4, bias, subln_gain.reshape(V_DIM, 1).astype(F32))


def s5_operators(a_re, a_im, log_dt, b_re, b_im, c_re, c_im):
    t_len, hp = S5_CHUNK, lax.Precision.HIGHEST
    a_re, a_im = a_re.astype(F32), a_im.astype(F32)
    dt = jnp.exp(log_dt.astype(F32))[..., None]
    steps = jnp.arange(t_len + 1, dtype=F32)[:, None, None, None]
    mag = jnp.exp(a_re * dt * steps)
    ang = a_im * dt * steps
    pw_re, pw_im = mag * jnp.cos(ang), mag * jnp.sin(ang)
    den = a_re * a_re + a_im * a_im
    nr, ni = pw_re[1] - 1.0, pw_im[1]
    coef_re = ((nr * a_re + ni * a_im) / den)[..., None]
    coef_im = ((ni * a_re - nr * a_im) / den)[..., None]
    b_re, b_im = b_re.astype(F32), b_im.astype(F32)
    bb_re = coef_re * b_re - coef_im * b_im
    bb_im = coef_re * b_im + coef_im * b_re
    c_re, c_im = c_re.astype(F32), c_im.astype(F32)

    g, n_st, p_ch = a_re.shape[1], SSM_STATE, SSM_GROUP
    tp = t_len * p_ch
    pwt_re = jnp.transpose(pw_re, (1, 2, 3, 0))
    pwt_im = jnp.transpose(pw_im, (1, 2, 3, 0))
    ct_re = jnp.transpose(c_re, (0, 1, 3, 2))
    ct_im = jnp.transpose(c_im, (0, 1, 3, 2))
    bbt_re = jnp.transpose(bb_re, (0, 1, 3, 2))
    bbt_im = jnp.transpose(bb_im, (0, 1, 3, 2))
    cp_re = ct_re[:, :, :, None, :] * pwt_re[..., None] - ct_im[:, :, :, None, :] * pwt_im[..., None]
    cp_im = ct_re[:, :, :, None, :] * pwt_im[..., None] + ct_im[:, :, :, None, :] * pwt_re[..., None]

    zlag = jnp.zeros((g, n_st, t_len - 1, p_ch), F32)

    def lagged(cp):
        return (jnp.concatenate([zlag, cp[0, :, :, :t_len]], axis=2),
                jnp.concatenate([jnp.flip(cp[1, :, :, :t_len], axis=2), zlag], axis=2))

    rf_re, rb_re = lagged(cp_re)
    rf_im, rb_im = lagged(cp_im)
    r_cat = jnp.concatenate([rf_re, rf_im, rb_re, rb_im], axis=1).reshape(g, 4 * n_st, (2 * t_len - 1) * p_ch)
    a_cat = jnp.concatenate([bbt_re[0], -bbt_im[0], bbt_re[1], -bbt_im[1]], axis=-1)
    kp = jnp.einsum('gpk,gkx->gpx', a_cat, r_cat, precision=hp)
    toep = jnp.stack([kp[:, :, (t_len - 1 - j) * p_ch:(t_len - 1 - j) * p_ch + tp] for j in range(t_len)],
                     axis=1).reshape(g, tp, tp)

    def state_map(d, reverse):
        pr = jnp.transpose(pw_re[:t_len, d], (1, 0, 2))
        pi = jnp.transpose(pw_im[:t_len, d], (1, 0, 2))
        if reverse:
            pr, pi = jnp.flip(pr, axis=1), jnp.flip(pi, axis=1)
        br, bi = bbt_re[d][:, None], bbt_im[d][:, None]
        m_re = (br * pr[:, :, None, :] - bi * pi[:, :, None, :]).reshape(g, tp, n_st)
        m_im = (br * pi[:, :, None, :] + bi * pr[:, :, None, :]).reshape(g, tp, n_st)
        return [m_re, m_im, m_im, m_re]

    w = jnp.concatenate([toep] + state_map(0, True) + state_map(1, False), axis=-1).astype(BF16)

    def out_map(d, reverse):
        wr, wi = cp_re[d][:, :, 1:t_len + 1], cp_im[d][:, :, 1:t_len + 1]
        if reverse:
            wr, wi = jnp.flip(wr, axis=2), jnp.flip(wi, axis=2)
        return [wr.reshape(g, n_st, tp), -wi.reshape(g, n_st, tp)]

    mc = jnp.concatenate(out_map(0, False) + out_map(1, True), axis=1).astype(BF16)

    def carry(d):
        ar, ai = pw_re[t_len, d], pw_im[t_len, d]
        return [jnp.concatenate([ar, ar], -1), jnp.concatenate([-ai, ai], -1),
                jnp.concatenate([ai, -ai], -1)]

    coef = jnp.stack(carry(0) + carry(1), axis=0)
    return w, mc, coef


def _gelu_tanh(x):
    return 0.5 * x * (1.0 + jnp.tanh(math.sqrt(2.0 / math.pi) * (x + 0.044715 * (x * x * x))))


def _s5_kernel(x_ref, sel_ref, w_ref, mc_ref, coef_ref, d_ref, y_ref,
               u_ref, ef_ref, efs_ref, eb_ref, ebs_ref, *, cn):
    gb = w_ref.shape[0]
    tp = w_ref.shape[1]
    n2 = 2 * SSM_STATE
    xcat = x_ref[0]

    def sel(g):
        off = (gb - 1 - g) * SSM_GROUP
        return sel_ref[off:off + xcat.shape[1], :]

    for g in range(gb):
        u_ref[g] = jnp.dot(xcat, sel(g), preferred_element_type=F32).astype(BF16)
        e = jnp.dot(u_ref[g], w_ref[g, :, tp:], preferred_element_type=F32)
        for r, ref in enumerate((ef_ref, efs_ref, eb_ref, ebs_ref)):
            ref[pl.ds(g, cn, stride=gb), :] = e[:, r * n2:(r + 1) * n2]

    cf, cfs, cfw = coef_ref[0], coef_ref[1], coef_ref[2]
    cb, cbs, cbw = coef_ref[3], coef_ref[4], coef_ref[5]

    def step(c, carry):
        s, sw, r, rw = carry
        fo = pl.multiple_of(c * gb, gb)
        bo = pl.multiple_of((cn - 1 - c) * gb, gb)
        e, es = ef_ref[pl.ds(fo, gb), :], efs_ref[pl.ds(fo, gb), :]
        ef_ref[pl.ds(fo, gb), :] = s
        s, sw = cf * s + cfs * sw + e, cf * sw + cfw * s + es
        e, es = eb_ref[pl.ds(bo, gb), :], ebs_ref[pl.ds(bo, gb), :]
        eb_ref[pl.ds(bo, gb), :] = r
        r, rw = cb * r + cbs * rw + e, cb * rw + cbw * r + es
        return s, sw, r, rw

    z = jnp.zeros((gb, n2), F32)
    lax.fori_loop(0, cn, step, (z, z, z, z))

    ycat = None
    for g in range(gb):
        u = u_ref[g]
        st = jnp.concatenate([ef_ref[pl.ds(g, cn, stride=gb), :], eb_ref[pl.ds(g, cn, stride=gb), :]],
                             axis=1).astype(BF16)
        y = (jnp.dot(u, w_ref[g, :, :tp], preferred_element_type=F32)
             + jnp.dot(st, mc_ref[g], preferred_element_type=F32)
             + u.astype(F32) * d_ref[g])
        placed = lax.dot_general(_gelu_tanh(y).astype(BF16), sel(g), (((1,), (1,)), ((), ())),
                                 preferred_element_type=F32)
        ycat = placed if ycat is None else ycat + placed
    y_ref[0] = ycat.astype(y_ref.dtype)


def s5_lane_selector():
    gb, p, t_len = S5_GB, SSM_GROUP, S5_CHUNK
    r = jnp.arange(t_len * V7X_LANES + (gb - 1) * p)[:, None] - (gb - 1) * p
    c = jnp.arange(t_len * p)[None, :]
    hit = (r >= 0) & (r // V7X_LANES == c // p) & ((r % V7X_LANES) // p == 0) & (r % p == c % p)
    return hit.astype(BF16)


def s5_mixer(proj, ops, ssm_d):
    w, mc, coef = ops
    s = proj.shape[0]
    g, p, t_len = SSM_GROUPS, SSM_GROUP, S5_CHUNK
    cn, tp = s // t_len, t_len * p
    gb = S5_GB
    assert gb * p == V7X_LANES
    nslab = g // gb
    u = lax.slice(proj, (0, 0), (s, SSM_WIDTH))
    xcat = u.reshape(cn, t_len, nslab, V7X_LANES).transpose(2, 0, 1, 3).reshape(nslab, cn, t_len * V7X_LANES)
    sel = s5_lane_selector()
    dsk = jnp.tile(ssm_d.astype(F32).reshape(g, 1, p), (1, 1, t_len))
    ycat = pl.pallas_call(
        functools.partial(_s5_kernel, cn=cn),
        out_shape=jax.ShapeDtypeStruct((nslab, cn, t_len * V7X_LANES), BF16),
        grid=(nslab,),
        in_specs=[pl.BlockSpec((1, cn, t_len * V7X_LANES), lambda i: (i, 0, 0)),
                  pl.BlockSpec(sel.shape, lambda i: (0, 0)),
                  pl.BlockSpec((gb, tp, w.shape[2]), lambda i: (i, 0, 0)),
                  pl.BlockSpec((gb, mc.shape[1], tp), lambda i: (i, 0, 0)),
                  pl.BlockSpec((6, gb, 2 * SSM_STATE), lambda i: (0, i, 0)),
                  pl.BlockSpec((gb, 1, tp), lambda i: (i, 0, 0))],
        out_specs=pl.BlockSpec((1, cn, t_len * V7X_LANES), lambda i: (i, 0, 0)),
        scratch_shapes=[pltpu.VMEM((gb, cn, tp), BF16)]
        + [pltpu.VMEM((cn * gb, 2 * SSM_STATE), F32) for _ in range(4)],
        compiler_params=_cp("parallel"),
        name="s5_chunked_scan",
    )(xcat, sel, w, mc, coef, dsk)
    return ycat.reshape(nslab, cn, t_len, V7X_LANES).transpose(1, 2, 0, 3).reshape(s, SSM_WIDTH)


def _sigmoid(x):
    return 1.0 / (1.0 + jnp.exp(-x))


def _glu_kernel(y_ref, w_ref, o_ref):
    y = y_ref[...]
    z = jnp.dot(y, w_ref[...], preferred_element_type=F32)
    o_ref[...] = (y.astype(F32) * _sigmoid(z)).astype(o_ref.dtype)


def glu(y, w, *, tm):
    s, d = y.shape
    return pl.pallas_call(
        _glu_kernel,
        out_shape=jax.ShapeDtypeStruct((s, d), BF16),
        grid=(s // tm,),
        in_specs=[pl.BlockSpec((tm, d), lambda i: (i, 0)),
                  pl.BlockSpec((d, d), lambda i: (0, 0))],
        out_specs=pl.BlockSpec((tm, d), lambda i: (i, 0)),
        compiler_params=_cp("parallel"),
        name="half_glu",
    )(y, w)


def _merge_kernel(yg_ref, ao_ref, gs_ref, ga_ref, ws_ref, wa_ref, o_ref):
    a = jnp.dot(yg_ref[...], ws_ref[...], preferred_element_type=F32)
    b = jnp.dot(ao_ref[...], wa_ref[...], preferred_element_type=F32)
    o = _sigmoid(gs_ref[...].astype(F32)) * a + _sigmoid(ga_ref[...].astype(F32)) * b
    o_ref[...] = o.astype(o_ref.dtype)


def gated_merge(yg, ao, proj, ws, wa, *, tm, tn):
    s, k = yg.shape
    n = ws.shape[1]
    cs, ca = COL_GS // tn, COL_GA // tn
    return pl.pallas_call(
        _merge_kernel,
        out_shape=jax.ShapeDtypeStruct((s, n), BF16),
        grid=(s // tm, n // tn),
        in_specs=[pl.BlockSpec((tm, k), lambda i, j: (i, 0)),
                  pl.BlockSpec((tm, k), lambda i, j: (i, 0)),
                  pl.BlockSpec((tm, tn), lambda i, j: (i, cs + j)),
                  pl.BlockSpec((tm, tn), lambda i, j: (i, ca + j)),
                  pl.BlockSpec((k, tn), lambda i, j: (0, j)),
                  pl.BlockSpec((k, tn), lambda i, j: (0, j))],
        out_specs=pl.BlockSpec((tm, tn), lambda i, j: (i, j)),
        compiler_params=_cp("parallel", "arbitrary"),
        name="gated_merge",
    )(yg, ao, proj, proj, ws, wa)


def _out_router_kernel(m_ref, x_ref, w_ref, g_ref, wrt_ref, xo_ref, h_ref, afft_ref):
    x1 = x_ref[...] + jnp.dot(m_ref[...], w_ref[...], preferred_element_type=F32)
    xo_ref[...] = x1
    r = lax.rsqrt(jnp.mean(x1 * x1, axis=-1, keepdims=True) + NORM_EPS)
    h = x1 * r * g_ref[...]
    h_ref[...] = h
    lgt = lax.dot_general(wrt_ref[...], h, (((1,), (1,)), ((), ())),
                          preferred_element_type=F32, precision=lax.Precision.HIGHEST)
    et = jnp.exp(lgt - jnp.max(lgt, axis=0, keepdims=True))
    afft_ref[...] = et / jnp.sum(et, axis=0, keepdims=True)


def out_proj_router(merged, x, w_out, gain, w_router, *, tm):
    s, d = x.shape
    e = w_router.shape[1]
    return pl.pallas_call(
        _out_router_kernel,
        out_shape=(jax.ShapeDtypeStruct((s, d), F32), jax.ShapeDtypeStruct((s, d), F32),
                   jax.ShapeDtypeStruct((e, s), F32)),
        grid=(s // tm,),
        in_specs=[pl.BlockSpec((tm, d), lambda i: (i, 0)),
                  pl.BlockSpec((tm, d), lambda i: (i, 0)),
                  pl.BlockSpec((d, d), lambda i: (0, 0)),
                  pl.BlockSpec((1, d), lambda i: (0, 0)),
                  pl.BlockSpec((e, d), lambda i: (0, 0))],
        out_specs=(pl.BlockSpec((tm, d), lambda i: (i, 0)),
                   pl.BlockSpec((tm, d), lambda i: (i, 0)),
                   pl.BlockSpec((e, tm), lambda i: (0, i))),
        compiler_params=_cp("parallel"),
        name="out_proj_router",
    )(merged, x, w_out, gain.reshape(1, d).astype(F32), w_router.astype(F32).T)


def _select_kernel(afft_ref, slot_ref, cum_ref, idx_ref, acc_ref, *, s, cap, blk):
    ne = afft_ref.shape[0]

    def bit_body(b, thr):
        cand = thr | jnp.left_shift(jnp.ones((ne, 1), I32), 30 - b)
        keys = pltpu.bitcast(afft_ref[...], I32)
        cnt = jnp.sum((keys >= cand).astype(I32), axis=1, keepdims=True)
        return jnp.where(cnt >= cap, cand, thr)

    thr = lax.fori_loop(0, 31, bit_body, jnp.zeros((ne, 1), I32))
    keys = pltpu.bitcast(afft_ref[...], I32)
    need = cap - jnp.sum((keys > thr).astype(I32), axis=1, keepdims=True)

    ri = lax.broadcasted_iota(I32, (blk, blk), 0)
    ci = lax.broadcasted_iota(I32, (blk, blk), 1)
    upper = jnp.where(ri < ci, 1.0, 0.0).astype(BF16)
    jcol = lax.broadcasted_iota(I32, (cap, 1), 0)
    trow = lax.broadcasted_iota(I32, (blk, V7X_LANES), 0)
    tlane = lax.broadcasted_iota(I32, (blk, V7X_LANES), 1)
    acc_ref[...] = jnp.zeros(acc_ref.shape, F32)

    def blk_body(b, carry):
        ceq, csel = carry
        off = pl.multiple_of(b * blk, blk)
        kb = pltpu.bitcast(afft_ref[:, pl.ds(off, blk)], I32)
        gt = kb > thr
        eq = kb == thr
        eqf = jnp.where(eq, 1.0, 0.0)
        rank_eq = jnp.dot(eqf.astype(BF16), upper, preferred_element_type=F32) + ceq
        sel = jnp.logical_or(gt, jnp.logical_and(eq, rank_eq < need.astype(F32)))
        self_ = jnp.where(sel, 1.0, 0.0)
        cum = jnp.dot(self_.astype(BF16), upper, preferred_element_type=F32) + csel
        cum_i = cum.astype(I32)
        cum_ref[:, pl.ds(off, blk)] = cum_i
        slot = jnp.where(sel, cum_i, -1)
        slot_ref[:, pl.ds(off, blk)] = slot
        tok = off + trow
        digits = jnp.where(tlane == 0, tok // V7X_LANES, jnp.where(tlane == 1, tok % V7X_LANES, 0))
        digits = digits.astype(F32).astype(BF16)
        for e in range(ne):
            oh = jnp.where(slot[e:e + 1, :] == jcol, 1.0, 0.0).astype(BF16)
            acc_ref[e] += jnp.dot(oh, digits, preferred_element_type=F32)
        return (ceq + jnp.sum(eqf, axis=1, keepdims=True), csel + jnp.sum(self_, axis=1, keepdims=True))

    z = jnp.zeros((ne, 1), F32)
    lax.fori_loop(0, s // blk, blk_body, (z, z))
    a = acc_ref[...]
    idx_ref[...] = (a[:, :, 0:1] * float(V7X_LANES) + a[:, :, 1:2]).astype(I32)


def expert_select(afft, *, cap, blk):
    ne, s = afft.shape
    return pl.pallas_call(
        functools.partial(_select_kernel, s=s, cap=cap, blk=blk),
        out_shape=(jax.ShapeDtypeStruct((ne, s), I32), jax.ShapeDtypeStruct((ne, s), I32),
                   jax.ShapeDtypeStruct((ne, cap, 1), I32)),
        scratch_shapes=[pltpu.VMEM((ne, cap, V7X_LANES), F32)],
        compiler_params=pltpu.CompilerParams(vmem_limit_bytes=VMEM_LIMIT),
        name="expert_select",
    )(afft)


GATHER_UNROLL = 8


def _ffn_kernel(idx_ref, h_hbm, wg_ref, wu_ref, wd_ref, y_ref, xg32_ref, xg_ref, sem, *, cap):
    e = pl.program_id(0)
    f = pl.program_id(1)

    def row_copy(r):
        tok = idx_ref[e * cap + r]
        return pltpu.make_async_copy(h_hbm.at[pl.ds(tok, 1)], xg32_ref.at[pl.ds(r, 1)], sem)

    @pl.when(f == 0)
    def _():
        def start(rb, c):
            for w in range(GATHER_UNROLL):
                row_copy(rb * GATHER_UNROLL + w).start()
            return c

        lax.fori_loop(0, cap // GATHER_UNROLL, start, 0)
        y_ref[...] = jnp.zeros(y_ref.shape, F32)
        pltpu.make_async_copy(h_hbm.at[pl.ds(0, cap)], xg32_ref, sem).wait()
        xg_ref[...] = xg32_ref[...].astype(BF16)

    xg = xg_ref[...]
    a = jnp.dot(xg, wg_ref[0, 0].astype(BF16), preferred_element_type=F32)
    b = jnp.dot(xg, wu_ref[0, 0].astype(BF16), preferred_element_type=F32)
    hid = (a * _sigmoid(a) * b).astype(BF16)
    y_ref[0, 0:cap, :] += jnp.dot(hid, wd_ref[0, 0].astype(BF16), preferred_element_type=F32)


def expert_ffn(idx, h2, wg, wu, wd, layer, *, cap, pad, fc):
    _, ne, d, ff = wg.shape
    grid_spec = pltpu.PrefetchScalarGridSpec(
        num_scalar_prefetch=1,
        grid=(ne, ff // fc),
        in_specs=[pl.BlockSpec(memory_space=pl.ANY),
                  pl.BlockSpec((1, 1, d, fc), lambda e, f, idx: (layer, e, 0, f)),
                  pl.BlockSpec((1, 1, d, fc), lambda e, f, idx: (layer, e, 0, f)),
                  pl.BlockSpec((1, 1, fc, d), lambda e, f, idx: (layer, e, f, 0))],
        out_specs=pl.BlockSpec((1, cap + pad, d), lambda e, f, idx: (e, 0, 0)),
        scratch_shapes=[pltpu.VMEM((cap, d), F32), pltpu.VMEM((cap, d), BF16),
                        pltpu.SemaphoreType.DMA(())],
    )
    return pl.pallas_call(
        functools.partial(_ffn_kernel, cap=cap),
        out_shape=jax.ShapeDtypeStruct((ne, cap + pad, d), F32),
        grid_spec=grid_spec,
        compiler_params=_cp("arbitrary", "arbitrary"),
        name="expert_ffn",
    )(idx.reshape(-1), h2, wg, wu, wd)


def _combine_kernel(st_ref, x_ref, aff_ref, slot_ref, y_hbm, o_ref, buf_ref, xbuf_ref, sem, xsem,
                    *, ne, rows, nt):
    t = pl.program_id(0)
    tm = x_ref.shape[0]

    def chunk_start(tt, e, c):
        st8 = (st_ref[tt * ne + e] // F32_SUBLANES) * F32_SUBLANES
        return pl.multiple_of(st8 + c * rows, F32_SUBLANES)

    def first_copy(tt, e, slot):
        return pltpu.make_async_copy(y_hbm.at[e, pl.ds(chunk_start(tt, e, 0), rows)],
                                     buf_ref.at[slot, e], sem.at[slot, e])

    @pl.when(t == 0)
    def _():
        for e in range(ne):
            first_copy(0, e, 0).start()

    @pl.when(t + 1 < nt)
    def _():
        for e in range(ne):
            first_copy(t + 1, e, (t + 1) % 2).start()

    slot = t % 2
    for e in range(ne):
        first_copy(t, e, slot).wait()

    pair = V7X_LANES // rows
    lane = lax.broadcasted_iota(I32, (1, V7X_LANES), 1)
    hi_parts, lo_parts = [], []
    for a in range(ne // pair):
        rel = jnp.zeros((tm, V7X_LANES), I32)
        gate = jnp.zeros((tm, V7X_LANES), F32)
        for b in range(pair):
            e = a * pair + b
            in_e = jnp.logical_and(lane >= b * rows, lane < (b + 1) * rows)
            rel = jnp.where(in_e, slot_ref[:, e:e + 1] - chunk_start(t, e, 0) + b * rows, rel)
            gate = jnp.where(in_e, aff_ref[:, e:e + 1], gate)
        gate = jnp.where(rel == lane, gate, 0.0)
        g_hi = gate.astype(BF16)
        hi_parts.append(g_hi)
        lo_parts.append((gate - g_hi.astype(F32)).astype(BF16))
    rhs = buf_ref[slot].reshape(ne * rows, x_ref.shape[1]).astype(BF16)
    o_ref[...] = (x_ref[...]
                  + jnp.dot(jnp.concatenate(hi_parts, axis=1), rhs, preferred_element_type=F32)
                  + jnp.dot(jnp.concatenate(lo_parts, axis=1), rhs, preferred_element_type=F32))

    lane_r = lax.broadcasted_iota(I32, (1, rows), 1)
    for e in range(ne):
        end = st_ref[(t + 1) * ne + e]
        for c in range(1, tm // rows + 1):

            @pl.when(end > chunk_start(t, e, c))
            def _():
                cp = pltpu.make_async_copy(y_hbm.at[e, pl.ds(chunk_start(t, e, c), rows)], xbuf_ref, xsem)
                cp.start()
                cp.wait()
                rel = slot_ref[:, e:e + 1] - chunk_start(t, e, c)
                oh = jnp.where(rel == lane_r, 1.0, 0.0).astype(BF16)
                contrib = jnp.dot(oh, xbuf_ref[...].astype(BF16), preferred_element_type=F32)
                o_ref[...] += aff_ref[:, e:e + 1] * contrib


def moe_combine(starts, x1, aff, slot, ye, *, tm, rows):
    s, d = x1.shape
    ne = aff.shape[1]
    nt = s // tm
    assert V7X_LANES % rows == 0 and ne % (V7X_LANES // rows) == 0
    grid_spec = pltpu.PrefetchScalarGridSpec(
        num_scalar_prefetch=1,
        grid=(nt,),
        in_specs=[pl.BlockSpec((tm, d), lambda t, st: (t, 0)),
                  pl.BlockSpec((tm, ne), lambda t, st: (t, 0)),
                  pl.BlockSpec((tm, ne), lambda t, st: (t, 0)),
                  pl.BlockSpec(memory_space=pl.ANY)],
        out_specs=pl.BlockSpec((tm, d), lambda t, st: (t, 0)),
        scratch_shapes=[pltpu.VMEM((2, ne, rows, d), F32), pltpu.VMEM((rows, d), F32),
                        pltpu.SemaphoreType.DMA((2, ne)), pltpu.SemaphoreType.DMA(())],
    )
    return pl.pallas_call(
        functools.partial(_combine_kernel, ne=ne, rows=rows, nt=nt),
        out_shape=jax.ShapeDtypeStruct((s, d), F32),
        grid_spec=grid_spec,
        compiler_params=_cp("arbitrary"),
        name="moe_combine",
    )(starts.reshape(-1), x1, aff, slot, ye)


TM_PROJ, TN_PROJ = 1024, 1024
TM_ROW = 256
TM_OUT = 512
ATTN_TILE = 512
SEL_BLK = 256
FFN_CHUNK = 256
COMBINE_ROWS = 64


def _layer(x, l, p, bias, far_bias):
    s = x.shape[0]
    lam_init = 0.8 - 0.6 * math.exp(-0.3 * l)
    proj = norm_matmul(x, p["norm_mix"], p["w_in"].astype(BF16), tm=min(TM_PROJ, s), tn=TN_PROJ)

    ops = s5_operators(p["ssm_a_re"], p["ssm_a_im"], p["ssm_log_dt"], p["ssm_b_re"], p["ssm_b_im"],
                       p["ssm_c_re"], p["ssm_c_im"])
    y = s5_mixer(proj, ops, p["ssm_d"])
    yg = glu(y, p["w_glu"].astype(BF16), tm=min(TM_PROJ, s))

    qn, kn = qk_norm(proj, p["q_gain"], p["k_gain"], tm=TM_ROW)
    lam = (jnp.exp(jnp.sum(p["lambda_q1"].astype(F32) * p["lambda_k1"].astype(F32)))
           - jnp.exp(jnp.sum(p["lambda_q2"].astype(F32) * p["lambda_k2"].astype(F32))) + lam_init)
    ao = diff_attention(qn, kn, proj, bias, far_bias, lam, p["subln_gain"], lam_init=lam_init,
                        tq=ATTN_TILE, tk=ATTN_TILE)

    merged = gated_merge(yg, ao, proj, p["w_ssm_branch"].astype(BF16), p["w_attn_branch"].astype(BF16),
                         tm=min(TM_PROJ, s), tn=TN_PROJ)
    x1, h2, afft = out_proj_router(merged, x, p["w_out"].astype(BF16), p["norm_ffn"], p["w_router"],
                                   tm=min(TM_OUT, s))

    cap = CAPACITY_FACTOR * s // N_EXPERTS
    slot_t, cum_t, idx = expert_select(afft, cap=cap, blk=SEL_BLK)
    ye = expert_ffn(idx, h2, p["w_expert_gate"], p["w_expert_up"], p["w_expert_down"], l,
                    cap=cap, pad=COMBINE_ROWS, fc=FFN_CHUNK)
    starts = jnp.concatenate([cum_t[:, ::TM_ROW].T, jnp.full((1, N_EXPERTS), cap, I32)], axis=0)
    return moe_combine(starts, x1, afft.T, slot_t.T, ye, tm=TM_ROW, rows=COMBINE_ROWS)


_LAYER_PARAMS = ("w_in", "ssm_a_re", "ssm_a_im", "ssm_log_dt", "ssm_b_re", "ssm_b_im", "ssm_c_re",
                 "ssm_c_im", "ssm_d", "w_glu", "w_ssm_branch", "q_gain", "k_gain", "lambda_q1",
                 "lambda_k1", "lambda_q2", "lambda_k2", "subln_gain", "w_attn_branch", "w_out",
                 "norm_mix", "norm_ffn", "w_router")
_STACKED_PARAMS = ("w_expert_gate", "w_expert_up", "w_expert_down")


def kernel(x, w_in, ssm_a_re, ssm_a_im, ssm_log_dt, ssm_b_re, ssm_b_im, ssm_c_re, ssm_c_im, ssm_d, w_glu, w_ssm_branch, q_gain, k_gain, lambda_q1, lambda_k1, lambda_q2, lambda_k2, subln_gain, w_attn_branch, rel_bias, w_out, norm_mix, norm_ffn, w_router, w_expert_gate, w_expert_up, w_expert_down):
    args = dict(locals())
    b = x.shape[0]
    bias = bias_tiles(rel_bias, tk=ATTN_TILE, tq=ATTN_TILE)
    half = REL_BUCKETS // 2
    far_bias = jnp.stack([rel_bias[half - 1], rel_bias[REL_BUCKETS - 1]], axis=1).astype(F32) * LOG2E
    outs = []
    for bi in range(b):
        xb = x[bi].astype(F32)
        for l in range(DEPTH):
            p = {k: args[k][l] for k in _LAYER_PARAMS}
            p.update({k: args[k] for k in _STACKED_PARAMS})
            xb = _layer(xb, l, p, bias, far_bias)
        outs.append(xb)
    return jnp.stack(outs, axis=0).astype(x.dtype)
```

```python
import functools
import math

import jax
import jax.numpy as jnp
import numpy as np
from jax import lax
from jax.experimental import pallas as pl
from jax.experimental.pallas import tpu as pltpu

F32 = jnp.float32
BF16 = jnp.bfloat16
I32 = jnp.int32

D_MODEL = 2048
DEPTH = 2
SSM_WIDTH = D_MODEL // 2
SSM_GROUP = 16
SSM_GROUPS = SSM_WIDTH // SSM_GROUP
SSM_STATE = 64
ATTN_HEADS = 8
QK_DIM = 64
V_DIM = 2 * QK_DIM
ATTN_WIDTH = ATTN_HEADS * V_DIM
QK_COLS = ATTN_HEADS * 2 * QK_DIM
REL_BUCKETS = 32
REL_MAX_DIST = 128
N_EXPERTS = 16
CAPACITY_FACTOR = 2
EXPERT_FF = D_MODEL
NORM_EPS = 1e-6
IN_COLS = SSM_WIDTH + 2 * QK_COLS + ATTN_WIDTH + 2 * D_MODEL
COL_Q = SSM_WIDTH
COL_K = COL_Q + QK_COLS
COL_V = COL_K + QK_COLS
COL_GS = COL_V + ATTN_WIDTH
COL_GA = COL_GS + D_MODEL

V7X_LANES = 128
F32_SUBLANES = 8
V7X_VMEM_BYTES = 64 * 1024 * 1024
VMEM_LIMIT = 56 * 1024 * 1024
LOG2E = 1.4426950408889634

S5_CHUNK = 16
S5_GB = 8
NEG_BIG = -1e30


def _cp(*sem):
    return pltpu.CompilerParams(dimension_semantics=sem, vmem_limit_bytes=VMEM_LIMIT)


def _norm_matmul_kernel(x_ref, g_ref, w_ref, o_ref, h_ref):
    @pl.when(pl.program_id(1) == 0)
    def _():
        x = x_ref[...]
        r = lax.rsqrt(jnp.mean(x * x, axis=-1, keepdims=True) + NORM_EPS)
        h_ref[...] = (x * r * g_ref[...]).astype(BF16)

    o_ref[...] = jnp.dot(h_ref[...], w_ref[...], preferred_element_type=F32).astype(o_ref.dtype)


def norm_matmul(x, gain, w, *, tm, tn):
    s, d = x.shape
    n = w.shape[1]
    return pl.pallas_call(
        _norm_matmul_kernel,
        out_shape=jax.ShapeDtypeStruct((s, n), BF16),
        grid=(s // tm, n // tn),
        in_specs=[pl.BlockSpec((tm, d), lambda i, j: (i, 0)),
                  pl.BlockSpec((1, d), lambda i, j: (0, 0)),
                  pl.BlockSpec((d, tn), lambda i, j: (0, j))],
        out_specs=pl.BlockSpec((tm, tn), lambda i, j: (i, j)),
        scratch_shapes=[pltpu.VMEM((tm, d), BF16)],
        compiler_params=_cp("parallel", "arbitrary"),
        name="norm_in_proj",
    )(x, gain.reshape(1, d).astype(F32), w)


def _attn_prep_kernel(q_ref, k_ref, v_ref, gq_ref, gk_ref, qo_ref, ko_ref, vo_ref):
    lane = lax.broadcasted_iota(I32, (1, V7X_LANES), 1)
    lo_mask = lane < QK_DIM

    def norm(src_ref, g_ref, a):
        x = src_ref[:, a * V7X_LANES:(a + 1) * V7X_LANES].astype(F32)
        ss = x * x
        lo = jnp.sum(jnp.where(lo_mask, ss, 0.0), axis=-1, keepdims=True)
        hi = jnp.sum(jnp.where(lo_mask, 0.0, ss), axis=-1, keepdims=True)
        ms = jnp.where(lo_mask, lo, hi) * (1.0 / QK_DIM)
        return x * lax.rsqrt(ms + NORM_EPS) * g_ref[...]

    for a in range(ATTN_HEADS):
        qo_ref[:, a * V7X_LANES:(a + 1) * V7X_LANES] = norm(q_ref, gq_ref, a).astype(qo_ref.dtype)
        ko_ref[a, 0] = norm(k_ref, gk_ref, a).astype(ko_ref.dtype)
        v = v_ref[:, a * V7X_LANES:(a + 1) * V7X_LANES].astype(F32)
        vo_ref[a, 0] = v.T.astype(vo_ref.dtype)


def attn_prep(proj, q_gain, k_gain, *, tk):
    s = proj.shape[0]
    nk = s // tk
    gq = (jnp.tile(q_gain.astype(F32), 2) * (QK_DIM ** -0.5 * LOG2E)).reshape(1, V7X_LANES)
    gk = jnp.tile(k_gain.astype(F32), 2).reshape(1, V7X_LANES)
    cq, ck, cv = COL_Q // QK_COLS, COL_K // QK_COLS, COL_V // ATTN_WIDTH
    return pl.pallas_call(
        _attn_prep_kernel,
        out_shape=(jax.ShapeDtypeStruct((s, QK_COLS), BF16),
                   jax.ShapeDtypeStruct((ATTN_HEADS, nk, tk, V_DIM), BF16),
                   jax.ShapeDtypeStruct((ATTN_HEADS, nk, V_DIM, tk), BF16)),
        grid=(nk,),
        in_specs=[pl.BlockSpec((tk, QK_COLS), lambda i: (i, cq)),
                  pl.BlockSpec((tk, QK_COLS), lambda i: (i, ck)),
                  pl.BlockSpec((tk, ATTN_WIDTH), lambda i: (i, cv)),
                  pl.BlockSpec((1, V7X_LANES), lambda i: (0, 0)),
                  pl.BlockSpec((1, V7X_LANES), lambda i: (0, 0))],
        out_specs=(pl.BlockSpec((tk, QK_COLS), lambda i: (i, 0)),
                   pl.BlockSpec((ATTN_HEADS, 1, tk, V_DIM), lambda i: (0, i, 0, 0)),
                   pl.BlockSpec((ATTN_HEADS, 1, V_DIM, tk), lambda i: (0, i, 0, 0))),
        compiler_params=_cp("parallel"),
        name="attn_prep",
    )(proj, proj, proj, gq, gk)


ATTN_NEAR = 3
ATTN_BIAS_TILES = 5
ATTN_UNROLL = 6


def _bias_tile_kernel(rb_ref, o_ref, *, tk, tq):
    h = pl.program_id(0)
    d = pl.program_id(1)
    w = tk + tq
    m = lax.broadcasted_iota(I32, (F32_SUBLANES, w), 1)
    rel = (d - ATTN_BIAS_TILES // 2) * tk + (tk - 1) - m
    half = REL_BUCKETS // 2
    exact = half // 2
    side = jnp.where(rel > 0, half, 0).astype(I32)
    n = jnp.abs(rel)
    nf = jnp.maximum(n, 1).astype(F32)
    large = exact + (jnp.log(nf / exact) / math.log(REL_MAX_DIST / exact) * (half - exact)).astype(I32)
    large = jnp.minimum(large, half - 1)
    bucket = side + jnp.where(n < exact, n, large).astype(I32)
    val = jnp.zeros((F32_SUBLANES, w), F32)
    for b in range(REL_BUCKETS):
        val = jnp.where(bucket == b, rb_ref[b, h], val)
    table = jnp.broadcast_to(val[0:1, :] * LOG2E, (tk, w))
    o_ref[0, 0] = pltpu.roll(table, w - tk + 1, 1, stride=1, stride_axis=0)[:, :tq]


def bias_tiles(rel_bias, *, tk, tq):
    assert tk == tq and (tk + tq) & (tk + tq - 1) == 0
    return pl.pallas_call(
        functools.partial(_bias_tile_kernel, tk=tk, tq=tq),
        out_shape=jax.ShapeDtypeStruct((ATTN_HEADS, ATTN_BIAS_TILES, tk, tq), F32),
        grid=(ATTN_HEADS, ATTN_BIAS_TILES),
        in_specs=[pl.BlockSpec(memory_space=pltpu.SMEM)],
        out_specs=pl.BlockSpec((1, 1, tk, tq), lambda h, d: (h, d, 0, 0)),
        compiler_params=_cp("parallel", "parallel"),
        name="t5_bias_tiles",
    )(rel_bias.astype(F32))


def _attn_kernel(lam_ref, far_ref, q_ref, k_ref, vt_ref, bias_ref, g_ref, o_ref,
                 s_ref, cm_ref, m_ref, l_ref, acc_ref, *, nk, tk, tq, out_scale):
    h = pl.program_id(0)
    i = pl.program_id(1)
    lane = lax.broadcasted_iota(I32, (1, V7X_LANES), 1)
    q = q_ref[...]
    zero = jnp.zeros_like(q)
    qmaps = (jnp.where(lane < QK_DIM, q, zero), jnp.where(lane < QK_DIM, zero, q))
    m_ref[...] = jnp.full(m_ref.shape, NEG_BIG, F32)
    l_ref[...] = jnp.zeros(l_ref.shape, F32)
    acc_ref[...] = jnp.zeros(acc_ref.shape, F32)

    def scores(j, slot, bias_tile, bias_const):
        kt = k_ref[0, j]
        for c in range(2):
            s = lax.dot_general(kt, qmaps[c], (((1,), (1,)), ((), ())),
                                preferred_element_type=F32)
            if bias_tile is not None:
                s = s + bias_tile
            s_ref[slot, c] = s
            cm_ref[slot, c:c + 1, :] = jnp.max(s, axis=0, keepdims=True) + bias_const
        cm_ref[slot, 2:3, :] = jnp.zeros((1, tq), F32) + bias_const

    def absorb(j, slot):
        vt = vt_ref[0, j]
        cb = cm_ref[slot, 2:3, :]
        for c in range(2):
            m_old = m_ref[c:c + 1, :]
            m_new = jnp.maximum(m_old, cm_ref[slot, c:c + 1, :])
            alpha = jnp.exp2(m_old - m_new)
            p = jnp.exp2(s_ref[slot, c] - (m_new - cb))
            l_ref[c:c + 1, :] = alpha * l_ref[c:c + 1, :] + jnp.sum(p, axis=0, keepdims=True)
            acc_ref[c] = alpha * acc_ref[c] + jnp.dot(vt, p.astype(BF16), preferred_element_type=F32)
            m_ref[c:c + 1, :] = m_new

    n0 = jnp.clip(i - 1, 0, nk - ATTN_NEAR)
    nfar = nk - ATTN_NEAR

    def far_tile(t):
        return jnp.where(t < n0, t, t + ATTN_NEAR)

    def far_const(j):
        return jnp.where(j < i, far_ref[h, 0], far_ref[h, 1])

    def near_bias(j):
        return bias_ref[0, j - i + ATTN_BIAS_TILES // 2]

    scores(n0, 0, near_bias(n0), 0.0)
    scores(n0 + 1, 1, near_bias(n0 + 1), 0.0)
    absorb(n0, 0)
    scores(n0 + 2, 0, near_bias(n0 + 2), 0.0)
    absorb(n0 + 1, 1)
    j0 = far_tile(0)
    scores(j0, 1, None, far_const(j0))
    absorb(n0 + 2, 0)

    def group(u, jprev):
        for w in range(ATTN_UNROLL):
            jn = far_tile(ATTN_UNROLL * u + w + 1)
            scores(jn, w % 2, None, far_const(jn))
            absorb(jprev, (w + 1) % 2)
            jprev = jn
        return jprev

    jlast = lax.fori_loop(0, (nfar - 1) // ATTN_UNROLL, group, j0)
    absorb(jlast, 1)

    o1 = acc_ref[0] / l_ref[0:1, :]
    o2 = acc_ref[1] / l_ref[1:2, :]
    o = o1 - lam_ref[0] * o2
    r = lax.rsqrt(jnp.mean(o * o, axis=0, keepdims=True) + NORM_EPS)
    o = o * r * g_ref[...] * out_scale
    o_ref[...] = o.T.astype(o_ref.dtype)


def diff_attention(qn, k4, vt4, bias, far_bias, lam, subln_gain, *, lam_init, tq, tk):
    s = qn.shape[0]
    nk = s // tk
    assert tq == tk and nk > ATTN_NEAR and (nk - ATTN_NEAR - 1) % ATTN_UNROLL == 0
    kern = functools.partial(_attn_kernel, nk=nk, tk=tk, tq=tq, out_scale=1.0 - lam_init)
    return pl.pallas_call(
        kern,
        out_shape=jax.ShapeDtypeStruct((s, ATTN_WIDTH), BF16),
        grid=(ATTN_HEADS, s // tq),
        in_specs=[pl.BlockSpec(memory_space=pltpu.SMEM),
                  pl.BlockSpec(memory_space=pltpu.SMEM),
                  pl.BlockSpec((tq, V_DIM), lambda h, i: (i, h)),
                  pl.BlockSpec((1, nk, tk, V_DIM), lambda h, i: (h, 0, 0, 0)),
                  pl.BlockSpec((1, nk, V_DIM, tk), lambda h, i: (h, 0, 0, 0)),
                  pl.BlockSpec((1, ATTN_BIAS_TILES, tk, tq), lambda h, i: (h, 0, 0, 0)),
                  pl.BlockSpec((V_DIM, 1), lambda h, i: (0, 0))],
        out_specs=pl.BlockSpec((tq, V_DIM), lambda h, i: (i, h)),
        scratch_shapes=[pltpu.VMEM((2, 2, tk, tq), F32),
                        pltpu.VMEM((2, F32_SUBLANES, tq), F32),
                        pltpu.VMEM((F32_SUBLANES, tq), F32),
                        pltpu.VMEM((F32_SUBLANES, tq), F32),
                        pltpu.VMEM((2, V_DIM, tq), F32)],
        compiler_params=_cp("parallel", "arbitrary"),
        name="diff_attention",
    )(lam.reshape(1).astype(F32), far_bias, qn, k4, vt4, bias, subln_gain.reshape(V_DIM, 1).astype(F32))


def s5_operators(a_re, a_im, log_dt, b_re, b_im, c_re, c_im):
    t_len, hp = S5_CHUNK, lax.Precision.HIGHEST
    a_re, a_im = a_re.astype(F32), a_im.astype(F32)
    dt = jnp.exp(log_dt.astype(F32))[..., None]
    steps = jnp.arange(t_len + 1, dtype=F32)[:, None, None, None]
    mag = jnp.exp(a_re * dt * steps)
    ang = a_im * dt * steps
    pw_re, pw_im = mag * jnp.cos(ang), mag * jnp.sin(ang)
    den = a_re * a_re + a_im * a_im
    nr, ni = pw_re[1] - 1.0, pw_im[1]
    coef_re = ((nr * a_re + ni * a_im) / den)[..., None]
    coef_im = ((ni * a_re - nr * a_im) / den)[..., None]
    b_re, b_im = b_re.astype(F32), b_im.astype(F32)
    bb_re = coef_re * b_re - coef_im * b_im
    bb_im = coef_re * b_im + coef_im * b_re
    c_re, c_im = c_re.astype(F32), c_im.astype(F32)

    g, n_st, p_ch = a_re.shape[1], SSM_STATE, SSM_GROUP
    tp = t_len * p_ch
    pwt_re = jnp.transpose(pw_re, (1, 2, 3, 0))
    pwt_im = jnp.transpose(pw_im, (1, 2, 3, 0))
    ct_re = jnp.transpose(c_re, (0, 1, 3, 2))
    ct_im = jnp.transpose(c_im, (0, 1, 3, 2))
    bbt_re = jnp.transpose(bb_re, (0, 1, 3, 2))
    bbt_im = jnp.transpose(bb_im, (0, 1, 3, 2))
    cp_re = ct_re[:, :, :, None, :] * pwt_re[..., None] - ct_im[:, :, :, None, :] * pwt_im[..., None]
    cp_im = ct_re[:, :, :, None, :] * pwt_im[..., None] + ct_im[:, :, :, None, :] * pwt_re[..., None]

    zlag = jnp.zeros((g, n_st, t_len - 1, p_ch), F32)

    def lagged(cp):
        return (jnp.concatenate([zlag, cp[0, :, :, :t_len]], axis=2),
                jnp.concatenate([jnp.flip(cp[1, :, :, :t_len], axis=2), zlag], axis=2))

    rf_re, rb_re = lagged(cp_re)
    rf_im, rb_im = lagged(cp_im)
    r_cat = jnp.concatenate([rf_re, rf_im, rb_re, rb_im], axis=1).reshape(g, 4 * n_st, (2 * t_len - 1) * p_ch)
    a_cat = jnp.concatenate([bbt_re[0], -bbt_im[0], bbt_re[1], -bbt_im[1]], axis=-1)
    kp = jnp.einsum('gpk,gkx->gpx', a_cat, r_cat, precision=hp)
    toep = jnp.stack([kp[:, :, (t_len - 1 - j) * p_ch:(t_len - 1 - j) * p_ch + tp] for j in range(t_len)],
                     axis=1).reshape(g, tp, tp).astype(BF16)

    def seg_powers(d, reverse):
        pr = jnp.transpose(pw_re[:t_len, d], (1, 0, 2))
        pi = jnp.transpose(pw_im[:t_len, d], (1, 0, 2))
        if reverse:
            pr, pi = jnp.flip(pr, axis=1), jnp.flip(pi, axis=1)
        return pr, pi

    prf, pif = seg_powers(0, True)
    prb, pib = seg_powers(1, False)
    pa = jnp.concatenate([prf, pif, pif, prf, prb, pib, pib, prb], axis=-1)
    pb = jnp.concatenate([-pif, prf, prf, -pif, -pib, prb, prb, -pib], axis=-1)
    br = jnp.concatenate([bbt_re[0]] * 4 + [bbt_re[1]] * 4, axis=-1)
    bi = jnp.concatenate([bbt_im[0]] * 4 + [bbt_im[1]] * 4, axis=-1)
    smap = (br[:, None] * pa[:, :, None, :] + bi[:, None] * pb[:, :, None, :]).reshape(g, tp, 8 * n_st)
    smap = smap.astype(BF16)

    def out_map(d, reverse):
        wr, wi = cp_re[d][:, :, 1:t_len + 1], cp_im[d][:, :, 1:t_len + 1]
        if reverse:
            wr, wi = jnp.flip(wr, axis=2), jnp.flip(wi, axis=2)
        return [wr.reshape(g, n_st, tp), -wi.reshape(g, n_st, tp)]

    mc = jnp.concatenate(out_map(0, False) + out_map(1, True), axis=1).astype(BF16)

    def carry(d):
        ar, ai = pw_re[t_len, d], pw_im[t_len, d]
        return [jnp.concatenate([ar, ar], -1), jnp.concatenate([-ai, ai], -1),
                jnp.concatenate([ai, -ai], -1)]

    coef = jnp.stack(carry(0) + carry(1), axis=0)
    return toep, smap, mc, coef


def _gelu_tanh(x):
    return 0.5 * x * (1.0 + jnp.tanh(math.sqrt(2.0 / math.pi) * (x + 0.044715 * (x * x * x))))


def _s5_kernel(x_ref, sel_ref, toep_ref, smap_ref, mc_ref, coef_ref, d_ref, y_ref,
               u_ref, ef_ref, efs_ref, eb_ref, ebs_ref, *, cn):
    gb = toep_ref.shape[0]
    tp = toep_ref.shape[1]
    n2 = 2 * SSM_STATE
    xcat = x_ref[0]

    def sel(g):
        off = (gb - 1 - g) * SSM_GROUP
        return sel_ref[off:off + xcat.shape[1], :]

    for g in range(gb):
        u_ref[g] = jnp.dot(xcat, sel(g), preferred_element_type=F32).astype(BF16)
        e = jnp.dot(u_ref[g], smap_ref[g], preferred_element_type=F32)
        for r, ref in enumerate((ef_ref, efs_ref, eb_ref, ebs_ref)):
            ref[pl.ds(g, cn, stride=gb), :] = e[:, r * n2:(r + 1) * n2]

    cf, cfs, cfw = coef_ref[0], coef_ref[1], coef_ref[2]
    cb, cbs, cbw = coef_ref[3], coef_ref[4], coef_ref[5]

    def step(c, carry):
        s, sw, r, rw = carry
        fo = pl.multiple_of(c * gb, gb)
        bo = pl.multiple_of((cn - 1 - c) * gb, gb)
        e, es = ef_ref[pl.ds(fo, gb), :], efs_ref[pl.ds(fo, gb), :]
        ef_ref[pl.ds(fo, gb), :] = s
        s, sw = cf * s + cfs * sw + e, cf * sw + cfw * s + es
        e, es = eb_ref[pl.ds(bo, gb), :], ebs_ref[pl.ds(bo, gb), :]
        eb_ref[pl.ds(bo, gb), :] = r
        r, rw = cb * r + cbs * rw + e, cb * rw + cbw * r + es
        return s, sw, r, rw

    z = jnp.zeros((gb, n2), F32)
    lax.fori_loop(0, cn, step, (z, z, z, z))

    ycat = None
    for g in range(gb):
        u = u_ref[g]
        st = jnp.concatenate([ef_ref[pl.ds(g, cn, stride=gb), :], eb_ref[pl.ds(g, cn, stride=gb), :]],
                             axis=1).astype(BF16)
        y = (jnp.dot(u, toep_ref[g], preferred_element_type=F32)
             + jnp.dot(st, mc_ref[g], preferred_element_type=F32)
             + u.astype(F32) * d_ref[g])
        placed = lax.dot_general(_gelu_tanh(y).astype(BF16), sel(g), (((1,), (1,)), ((), ())),
                                 preferred_element_type=F32)
        ycat = placed if ycat is None else ycat + placed
    y_ref[0] = ycat.astype(y_ref.dtype)


def s5_lane_selector():
    gb, p, t_len = S5_GB, SSM_GROUP, S5_CHUNK
    r = jnp.arange(t_len * V7X_LANES + (gb - 1) * p)[:, None] - (gb - 1) * p
    c = jnp.arange(t_len * p)[None, :]
    hit = (r >= 0) & (r // V7X_LANES == c // p) & ((r % V7X_LANES) // p == 0) & (r % p == c % p)
    return hit.astype(BF16)


def s5_mixer(proj, ops, ssm_d):
    toep, smap, mc, coef = ops
    s = proj.shape[0]
    g, p, t_len = SSM_GROUPS, SSM_GROUP, S5_CHUNK
    cn, tp = s // t_len, t_len * p
    gb = S5_GB
    assert gb * p == V7X_LANES
    nslab = g // gb
    u = lax.slice(proj, (0, 0), (s, SSM_WIDTH))
    xcat = u.reshape(cn, t_len, nslab, V7X_LANES).transpose(2, 0, 1, 3).reshape(nslab, cn, t_len * V7X_LANES)
    sel = s5_lane_selector()
    dsk = jnp.tile(ssm_d.astype(F32).reshape(g, 1, p), (1, 1, t_len))
    ycat = pl.pallas_call(
        functools.partial(_s5_kernel, cn=cn),
        out_shape=jax.ShapeDtypeStruct((nslab, cn, t_len * V7X_LANES), BF16),
        grid=(nslab,),
        in_specs=[pl.BlockSpec((1, cn, t_len * V7X_LANES), lambda i: (i, 0, 0)),
                  pl.BlockSpec(sel.shape, lambda i: (0, 0)),
                  pl.BlockSpec((gb, tp, tp), lambda i: (i, 0, 0)),
                  pl.BlockSpec((gb, tp, smap.shape[2]), lambda i: (i, 0, 0)),
                  pl.BlockSpec((gb, mc.shape[1], tp), lambda i: (i, 0, 0)),
                  pl.BlockSpec((6, gb, 2 * SSM_STATE), lambda i: (0, i, 0)),
                  pl.BlockSpec((gb, 1, tp), lambda i: (i, 0, 0))],
        out_specs=pl.BlockSpec((1, cn, t_len * V7X_LANES), lambda i: (i, 0, 0)),
        scratch_shapes=[pltpu.VMEM((gb, cn, tp), BF16)]
        + [pltpu.VMEM((cn * gb, 2 * SSM_STATE), F32) for _ in range(4)],
        compiler_params=_cp("parallel"),
        name="s5_chunked_scan",
    )(xcat, sel, toep, smap, mc, coef, dsk)
    return ycat.reshape(nslab, cn, t_len, V7X_LANES).transpose(1, 2, 0, 3).reshape(s, SSM_WIDTH)


def _sigmoid(x):
    return 1.0 / (1.0 + jnp.exp(-x))


def _glu_kernel(y_ref, w_ref, o_ref):
    y = y_ref[...]
    z = jnp.dot(y, w_ref[...], preferred_element_type=F32)
    o_ref[...] = (y.astype(F32) * _sigmoid(z)).astype(o_ref.dtype)


def glu(y, w, *, tm):
    s, d = y.shape
    return pl.pallas_call(
        _glu_kernel,
        out_shape=jax.ShapeDtypeStruct((s, d), BF16),
        grid=(s // tm,),
        in_specs=[pl.BlockSpec((tm, d), lambda i: (i, 0)),
                  pl.BlockSpec((d, d), lambda i: (0, 0))],
        out_specs=pl.BlockSpec((tm, d), lambda i: (i, 0)),
        compiler_params=_cp("parallel"),
        name="half_glu",
    )(y, w)


def _merge_kernel(yg_ref, ao_ref, gs_ref, ga_ref, ws_ref, wa_ref, o_ref):
    a = jnp.dot(yg_ref[...], ws_ref[...], preferred_element_type=F32)
    b = jnp.dot(ao_ref[...], wa_ref[...], preferred_element_type=F32)
    o = _sigmoid(gs_ref[...].astype(F32)) * a + _sigmoid(ga_ref[...].astype(F32)) * b
    o_ref[...] = o.astype(o_ref.dtype)


def gated_merge(yg, ao, proj, ws, wa, *, tm, tn):
    s, k = yg.shape
    n = ws.shape[1]
    cs, ca = COL_GS // tn, COL_GA // tn
    return pl.pallas_call(
        _merge_kernel,
        out_shape=jax.ShapeDtypeStruct((s, n), BF16),
        grid=(s // tm, n // tn),
        in_specs=[pl.BlockSpec((tm, k), lambda i, j: (i, 0)),
                  pl.BlockSpec((tm, k), lambda i, j: (i, 0)),
                  pl.BlockSpec((tm, tn), lambda i, j: (i, cs + j)),
                  pl.BlockSpec((tm, tn), lambda i, j: (i, ca + j)),
                  pl.BlockSpec((k, tn), lambda i, j: (0, j)),
                  pl.BlockSpec((k, tn), lambda i, j: (0, j))],
        out_specs=pl.BlockSpec((tm, tn), lambda i, j: (i, j)),
        compiler_params=_cp("parallel", "arbitrary"),
        name="gated_merge",
    )(yg, ao, proj, proj, ws, wa)


def _out_router_kernel(m_ref, x_ref, w_ref, g_ref, wrt_ref, xo_ref, h_ref, afft_ref):
    x1 = x_ref[...] + jnp.dot(m_ref[...], w_ref[...], preferred_element_type=F32)
    xo_ref[...] = x1
    r = lax.rsqrt(jnp.mean(x1 * x1, axis=-1, keepdims=True) + NORM_EPS)
    h = x1 * r * g_ref[...]
    h_ref[...] = h
    lgt = lax.dot_general(wrt_ref[...], h, (((1,), (1,)), ((), ())),
                          preferred_element_type=F32, precision=lax.Precision.HIGHEST)
    et = jnp.exp(lgt - jnp.max(lgt, axis=0, keepdims=True))
    afft_ref[...] = et / jnp.sum(et, axis=0, keepdims=True)


def out_proj_router(merged, x, w_out, gain, w_router, *, tm):
    s, d = x.shape
    e = w_router.shape[1]
    return pl.pallas_call(
        _out_router_kernel,
        out_shape=(jax.ShapeDtypeStruct((s, d), F32), jax.ShapeDtypeStruct((s, d), F32),
                   jax.ShapeDtypeStruct((e, s), F32)),
        grid=(s // tm,),
        in_specs=[pl.BlockSpec((tm, d), lambda i: (i, 0)),
                  pl.BlockSpec((tm, d), lambda i: (i, 0)),
                  pl.BlockSpec((d, d), lambda i: (0, 0)),
                  pl.BlockSpec((1, d), lambda i: (0, 0)),
                  pl.BlockSpec((e, d), lambda i: (0, 0))],
        out_specs=(pl.BlockSpec((tm, d), lambda i: (i, 0)),
                   pl.BlockSpec((tm, d), lambda i: (i, 0)),
                   pl.BlockSpec((e, tm), lambda i: (0, i))),
        compiler_params=_cp("parallel"),
        name="out_proj_router",
    )(merged, x, w_out, gain.reshape(1, d).astype(F32), w_router.astype(F32).T)


SLOT_LO_BITS = 6
SLOT_LO = 1 << SLOT_LO_BITS


def _select_kernel(afft_ref, slot_ref, cum_ref, idx_ref, acc_ref, *, s, cap, blk):
    ne = afft_ref.shape[0]

    def bit_body(b, thr):
        cand = thr | jnp.left_shift(jnp.ones((ne, 1), I32), 30 - b)
        keys = pltpu.bitcast(afft_ref[...], I32)
        cnt = jnp.sum((keys >= cand).astype(I32), axis=1, keepdims=True)
        return jnp.where(cnt >= cap, cand, thr)

    thr = lax.fori_loop(0, 31, bit_body, jnp.zeros((ne, 1), I32))
    keys = pltpu.bitcast(afft_ref[...], I32)
    need = cap - jnp.sum((keys > thr).astype(I32), axis=1, keepdims=True)

    ri = lax.broadcasted_iota(I32, (blk, blk), 0)
    ci = lax.broadcasted_iota(I32, (blk, blk), 1)
    upper = jnp.where(ri < ci, 1.0, 0.0).astype(BF16)
    na = cap // SLOT_LO
    acol = lax.broadcasted_iota(I32, (na, 1), 0)
    bcol = lax.broadcasted_iota(I32, (SLOT_LO, 1), 0)
    tlane = lax.broadcasted_iota(I32, (1, blk), 1)
    acc_ref[...] = jnp.zeros(acc_ref.shape, F32)

    def blk_body(b, carry):
        ceq, csel = carry
        off = pl.multiple_of(b * blk, blk)
        kb = pltpu.bitcast(afft_ref[:, pl.ds(off, blk)], I32)
        gt = kb > thr
        eq = kb == thr
        eqf = jnp.where(eq, 1.0, 0.0)
        rank_eq = jnp.dot(eqf.astype(BF16), upper, preferred_element_type=F32) + ceq
        sel = jnp.logical_or(gt, jnp.logical_and(eq, rank_eq < need.astype(F32)))
        self_ = jnp.where(sel, 1.0, 0.0)
        cum = jnp.dot(self_.astype(BF16), upper, preferred_element_type=F32) + csel
        cum_i = cum.astype(I32)
        cum_ref[:, pl.ds(off, blk)] = cum_i
        slot = jnp.where(sel, cum_i, -1)
        slot_ref[:, pl.ds(off, blk)] = slot
        tok = off + tlane
        hi = (tok // V7X_LANES).astype(F32)
        lo = (tok % V7X_LANES).astype(F32)
        for e in range(ne):
            srow = slot[e:e + 1, :]
            in_a = lax.shift_right_arithmetic(srow, SLOT_LO_BITS) == acol
            lhs = jnp.concatenate([jnp.where(in_a, hi, 0.0), jnp.where(in_a, lo, 0.0)],
                                  axis=0).astype(BF16)
            rhs = jnp.where((srow & (SLOT_LO - 1)) == bcol, 1.0, 0.0).astype(BF16)
            acc_ref[e] += lax.dot_general(lhs, rhs, (((1,), (1,)), ((), ())), preferred_element_type=F32)
        return (ceq + jnp.sum(eqf, axis=1, keepdims=True), csel + jnp.sum(self_, axis=1, keepdims=True))

    z = jnp.zeros((ne, 1), F32)
    lax.fori_loop(0, s // blk, blk_body, (z, z))
    a = acc_ref[...]
    idx_ref[...] = (a[:, :na, :] * float(V7X_LANES) + a[:, na:, :]).astype(I32)


def expert_select(afft, *, cap, blk):
    ne, s = afft.shape
    assert cap % SLOT_LO == 0
    return pl.pallas_call(
        functools.partial(_select_kernel, s=s, cap=cap, blk=blk),
        out_shape=(jax.ShapeDtypeStruct((ne, s), I32), jax.ShapeDtypeStruct((ne, s), I32),
                   jax.ShapeDtypeStruct((ne, cap // SLOT_LO, SLOT_LO), I32)),
        scratch_shapes=[pltpu.VMEM((ne, 2 * (cap // SLOT_LO), SLOT_LO), F32)],
        compiler_params=pltpu.CompilerParams(vmem_limit_bytes=VMEM_LIMIT),
        name="expert_select",
    )(afft)


GATHER_UNROLL = 8


def _ffn_kernel(idx_ref, h_hbm, wg_ref, wu_ref, wd_ref, y_ref, xg32_ref, xg_ref, sem, *, cap):
    e = pl.program_id(0)
    f = pl.program_id(1)

    def row_copy(r):
        tok = idx_ref[e * cap + r]
        return pltpu.make_async_copy(h_hbm.at[pl.ds(tok, 1)], xg32_ref.at[pl.ds(r, 1)], sem)

    @pl.when(f == 0)
    def _():
        def start(rb, c):
            for w in range(GATHER_UNROLL):
                row_copy(rb * GATHER_UNROLL + w).start()
            return c

        lax.fori_loop(0, cap // GATHER_UNROLL, start, 0)
        y_ref[...] = jnp.zeros(y_ref.shape, F32)
        pltpu.make_async_copy(h_hbm.at[pl.ds(0, cap)], xg32_ref, sem).wait()
        xg_ref[...] = xg32_ref[...].astype(BF16)

    xg = xg_ref[...]
    a = jnp.dot(xg, wg_ref[0, 0].astype(BF16), preferred_element_type=F32)
    b = jnp.dot(xg, wu_ref[0, 0].astype(BF16), preferred_element_type=F32)
    hid = (a * _sigmoid(a) * b).astype(BF16)
    y_ref[0, 0:cap, :] += jnp.dot(hid, wd_ref[0, 0].astype(BF16), preferred_element_type=F32)


def expert_ffn(idx, h2, wg, wu, wd, layer, *, cap, pad, fc):
    _, ne, d, ff = wg.shape
    grid_spec = pltpu.PrefetchScalarGridSpec(
        num_scalar_prefetch=1,
        grid=(ne, ff // fc),
        in_specs=[pl.BlockSpec(memory_space=pl.ANY),
                  pl.BlockSpec((1, 1, d, fc), lambda e, f, idx: (layer, e, 0, f)),
                  pl.BlockSpec((1, 1, d, fc), lambda e, f, idx: (layer, e, 0, f)),
                  pl.BlockSpec((1, 1, fc, d), lambda e, f, idx: (layer, e, f, 0))],
        out_specs=pl.BlockSpec((1, cap + pad, d), lambda e, f, idx: (e, 0, 0)),
        scratch_shapes=[pltpu.VMEM((cap, d), F32), pltpu.VMEM((cap, d), BF16),
                        pltpu.SemaphoreType.DMA(())],
    )
    return pl.pallas_call(
        functools.partial(_ffn_kernel, cap=cap),
        out_shape=jax.ShapeDtypeStruct((ne, cap + pad, d), F32),
        grid_spec=grid_spec,
        compiler_params=_cp("arbitrary", "arbitrary"),
        name="expert_ffn",
    )(idx.reshape(-1), h2, wg, wu, wd)


def _combine_kernel(st_ref, x_ref, aff_ref, slot_ref, y_hbm, o_ref, buf_ref, xbuf_ref, sem, xsem,
                    *, ne, rows, nt):
    t = pl.program_id(0)
    tm = x_ref.shape[0]

    def chunk_start(tt, e, c):
        st8 = (st_ref[tt * ne + e] // F32_SUBLANES) * F32_SUBLANES
        return pl.multiple_of(st8 + c * rows, F32_SUBLANES)

    def first_copy(tt, e, slot):
        return pltpu.make_async_copy(y_hbm.at[e, pl.ds(chunk_start(tt, e, 0), rows)],
                                     buf_ref.at[slot, e], sem.at[slot, e])

    @pl.when(t == 0)
    def _():
        for e in range(ne):
            first_copy(0, e, 0).start()

    @pl.when(t + 1 < nt)
    def _():
        for e in range(ne):
            first_copy(t + 1, e, (t + 1) % 2).start()

    slot = t % 2
    for e in range(ne):
        first_copy(t, e, slot).wait()

    pair = V7X_LANES // rows
    lane = lax.broadcasted_iota(I32, (1, V7X_LANES), 1)
    hi_parts, lo_parts = [], []
    for a in range(ne // pair):
        rel = jnp.zeros((tm, V7X_LANES), I32)
        gate = jnp.zeros((tm, V7X_LANES), F32)
        for b in range(pair):
            e = a * pair + b
            in_e = jnp.logical_and(lane >= b * rows, lane < (b + 1) * rows)
            rel = jnp.where(in_e, slot_ref[:, e:e + 1] - chunk_start(t, e, 0) + b * rows, rel)
            gate = jnp.where(in_e, aff_ref[:, e:e + 1], gate)
        gate = jnp.where(rel == lane, gate, 0.0)
        g_hi = gate.astype(BF16)
        hi_parts.append(g_hi)
        lo_parts.append((gate - g_hi.astype(F32)).astype(BF16))
    rhs = buf_ref[slot].reshape(ne * rows, x_ref.shape[1]).astype(BF16)
    o_ref[...] = (x_ref[...]
                  + jnp.dot(jnp.concatenate(hi_parts, axis=1), rhs, preferred_element_type=F32)
                  + jnp.dot(jnp.concatenate(lo_parts, axis=1), rhs, preferred_element_type=F32))

    lane_r = lax.broadcasted_iota(I32, (1, rows), 1)
    for e in range(ne):
        end = st_ref[(t + 1) * ne + e]
        for c in range(1, tm // rows + 1):

            @pl.when(end > chunk_start(t, e, c))
            def _():
                cp = pltpu.make_async_copy(y_hbm.at[e, pl.ds(chunk_start(t, e, c), rows)], xbuf_ref, xsem)
                cp.start()
                cp.wait()
                rel = slot_ref[:, e:e + 1] - chunk_start(t, e, c)
                oh = jnp.where(rel == lane_r, 1.0, 0.0).astype(BF16)
                contrib = jnp.dot(oh, xbuf_ref[...].astype(BF16), preferred_element_type=F32)
                o_ref[...] += aff_ref[:, e:e + 1] * contrib


def moe_combine(starts, x1, aff, slot, ye, *, tm, rows):
    s, d = x1.shape
    ne = aff.shape[1]
    nt = s // tm
    assert V7X_LANES % rows == 0 and ne % (V7X_LANES // rows) == 0
    grid_spec = pltpu.PrefetchScalarGridSpec(
        num_scalar_prefetch=1,
        grid=(nt,),
        in_specs=[pl.BlockSpec((tm, d), lambda t, st: (t, 0)),
                  pl.BlockSpec((tm, ne), lambda t, st: (t, 0)),
                  pl.BlockSpec((tm, ne), lambda t, st: (t, 0)),
                  pl.BlockSpec(memory_space=pl.ANY)],
        out_specs=pl.BlockSpec((tm, d), lambda t, st: (t, 0)),
        scratch_shapes=[pltpu.VMEM((2, ne, rows, d), F32), pltpu.VMEM((rows, d), F32),
                        pltpu.SemaphoreType.DMA((2, ne)), pltpu.SemaphoreType.DMA(())],
    )
    return pl.pallas_call(
        functools.partial(_combine_kernel, ne=ne, rows=rows, nt=nt),
        out_shape=jax.ShapeDtypeStruct((s, d), F32),
        grid_spec=grid_spec,
        compiler_params=_cp("arbitrary"),
        name="moe_combine",
    )(starts.reshape(-1), x1, aff, slot, ye)


TM_PROJ, TN_PROJ = 1024, 1024
TM_ROW = 256
TM_OUT = 512
ATTN_TILE = 512
SEL_BLK = 256
FFN_CHUNK = 256
COMBINE_ROWS = 64


def _layer(x, l, p, bias, far_bias):
    s = x.shape[0]
    lam_init = 0.8 - 0.6 * math.exp(-0.3 * l)
    proj = norm_matmul(x, p["norm_mix"], p["w_in"].astype(BF16), tm=min(TM_PROJ, s), tn=TN_PROJ)

    ops = s5_operators(p["ssm_a_re"], p["ssm_a_im"], p["ssm_log_dt"], p["ssm_b_re"], p["ssm_b_im"],
                       p["ssm_c_re"], p["ssm_c_im"])
    y = s5_mixer(proj, ops, p["ssm_d"])
    yg = glu(y, p["w_glu"].astype(BF16), tm=min(TM_PROJ, s))

    qn, k4, vt4 = attn_prep(proj, p["q_gain"], p["k_gain"], tk=ATTN_TILE)
    lam = (jnp.exp(jnp.sum(p["lambda_q1"].astype(F32) * p["lambda_k1"].astype(F32)))
           - jnp.exp(jnp.sum(p["lambda_q2"].astype(F32) * p["lambda_k2"].astype(F32))) + lam_init)
    ao = diff_attention(qn, k4, vt4, bias, far_bias, lam, p["subln_gain"], lam_init=lam_init,
                        tq=ATTN_TILE, tk=ATTN_TILE)

    merged = gated_merge(yg, ao, proj, p["w_ssm_branch"].astype(BF16), p["w_attn_branch"].astype(BF16),
                         tm=min(TM_PROJ, s), tn=TN_PROJ)
    x1, h2, afft = out_proj_router(merged, x, p["w_out"].astype(BF16), p["norm_ffn"], p["w_router"],
                                   tm=min(TM_OUT, s))

    cap = CAPACITY_FACTOR * s // N_EXPERTS
    slot_t, cum_t, idx = expert_select(afft, cap=cap, blk=SEL_BLK)
    ye = expert_ffn(idx, h2, p["w_expert_gate"], p["w_expert_up"], p["w_expert_down"], l,
                    cap=cap, pad=COMBINE_ROWS, fc=FFN_CHUNK)
    starts = jnp.concatenate([cum_t[:, ::TM_ROW].T, jnp.full((1, N_EXPERTS), cap, I32)], axis=0)
    return moe_combine(starts, x1, afft.T, slot_t.T, ye, tm=TM_ROW, rows=COMBINE_ROWS)


_LAYER_PARAMS = ("w_in", "ssm_a_re", "ssm_a_im", "ssm_log_dt", "ssm_b_re", "ssm_b_im", "ssm_c_re",
                 "ssm_c_im", "ssm_d", "w_glu", "w_ssm_branch", "q_gain", "k_gain", "lambda_q1",
                 "lambda_k1", "lambda_q2", "lambda_k2", "subln_gain", "w_attn_branch", "w_out",
                 "norm_mix", "norm_ffn", "w_router")
_STACKED_PARAMS = ("w_expert_gate", "w_expert_up", "w_expert_down")


def kernel(x, w_in, ssm_a_re, ssm_a_im, ssm_log_dt, ssm_b_re, ssm_b_im, ssm_c_re, ssm_c_im, ssm_d, w_glu, w_ssm_branch, q_gain, k_gain, lambda_q1, lambda_k1, lambda_q2, lambda_k2, subln_gain, w_attn_branch, rel_bias, w_out, norm_mix, norm_ffn, w_router, w_expert_gate, w_expert_up, w_expert_down):
    args = dict(locals())
    b = x.shape[0]
    bias = bias_tiles(rel_bias, tk=ATTN_TILE, tq=ATTN_TILE)
    half = REL_BUCKETS // 2
    far_bias = jnp.stack([rel_bias[half - 1], rel_bias[REL_BUCKETS - 1]], axis=1).astype(F32) * LOG2E
    outs = []
    for bi in range(b):
        xb = x[bi].astype(F32)
        for l in range(DEPTH):
            p = {k: args[k][l] for k in _LAYER_PARAMS}
            p.update({k: args[k] for k in _STACKED_PARAMS})
            xb = _layer(xb, l, p, bias, far_bias)
        outs.append(xb)
    return jnp.stack(outs, axis=0).astype(x.dtype)
```

```python
import functools
import math

import jax
import jax.numpy as jnp
import numpy as np
from jax import lax
from jax.experimental import pallas as pl
from jax.experimental.pallas import tpu as pltpu

F32 = jnp.float32
BF16 = jnp.bfloat16
I32 = jnp.int32

D_MODEL = 2048
DEPTH = 2
SSM_WIDTH = D_MODEL // 2
SSM_GROUP = 16
SSM_GROUPS = SSM_WIDTH // SSM_GROUP
SSM_STATE = 64
ATTN_HEADS = 8
QK_DIM = 64
V_DIM = 2 * QK_DIM
ATTN_WIDTH = ATTN_HEADS * V_DIM
QK_COLS = ATTN_HEADS * 2 * QK_DIM
REL_BUCKETS = 32
REL_MAX_DIST = 128
N_EXPERTS = 16
CAPACITY_FACTOR = 2
EXPERT_FF = D_MODEL
NORM_EPS = 1e-6
IN_COLS = SSM_WIDTH + 2 * QK_COLS + ATTN_WIDTH + 2 * D_MODEL
COL_Q = SSM_WIDTH
COL_K = COL_Q + QK_COLS
COL_V = COL_K + QK_COLS
COL_GS = COL_V + ATTN_WIDTH
COL_GA = COL_GS + D_MODEL

V7X_LANES = 128
F32_SUBLANES = 8
V7X_VMEM_BYTES = 64 * 1024 * 1024
VMEM_LIMIT = 56 * 1024 * 1024
LOG2E = 1.4426950408889634

S5_CHUNK = 16
S5_GB = 8
NEG_BIG = -1e30


def _cp(*sem):
    return pltpu.CompilerParams(dimension_semantics=sem, vmem_limit_bytes=VMEM_LIMIT)


def _norm_matmul_kernel(x_ref, g_ref, w_ref, o_ref, xcat_ref, h_ref, u_ref):
    j = pl.program_id(1)

    @pl.when(j == 0)
    def _():
        x = x_ref[...]
        r = lax.rsqrt(jnp.mean(x * x, axis=-1, keepdims=True) + NORM_EPS)
        h_ref[...] = (x * r * g_ref[...]).astype(BF16)

    res = jnp.dot(h_ref[...], w_ref[...], preferred_element_type=F32)
    o_ref[...] = res.astype(o_ref.dtype)

    @pl.when(j == 0)
    def _():
        cn = u_ref.shape[1] // S5_CHUNK
        for slab in range(u_ref.shape[0]):
            u_ref[slab] = res[:, slab * V7X_LANES:(slab + 1) * V7X_LANES]
            for t in range(S5_CHUNK):
                piece = u_ref[slab, pl.ds(t, cn, stride=S5_CHUNK), :]
                xcat_ref[slab, :, t * V7X_LANES:(t + 1) * V7X_LANES] = piece.astype(xcat_ref.dtype)


def norm_matmul(x, gain, w, *, tm, tn):
    s, d = x.shape
    n = w.shape[1]
    assert tn == SSM_WIDTH and tm % (S5_CHUNK * F32_SUBLANES) == 0
    nslab = SSM_WIDTH // V7X_LANES
    return pl.pallas_call(
        _norm_matmul_kernel,
        out_shape=(jax.ShapeDtypeStruct((s, n), BF16),
                   jax.ShapeDtypeStruct((nslab, s // S5_CHUNK, S5_CHUNK * V7X_LANES), BF16)),
        grid=(s // tm, n // tn),
        in_specs=[pl.BlockSpec((tm, d), lambda i, j: (i, 0)),
                  pl.BlockSpec((1, d), lambda i, j: (0, 0)),
                  pl.BlockSpec((d, tn), lambda i, j: (0, j))],
        out_specs=(pl.BlockSpec((tm, tn), lambda i, j: (i, j)),
                   pl.BlockSpec((nslab, tm // S5_CHUNK, S5_CHUNK * V7X_LANES), lambda i, j: (0, i, 0))),
        scratch_shapes=[pltpu.VMEM((tm, d), BF16), pltpu.VMEM((nslab, tm, V7X_LANES), F32)],
        compiler_params=_cp("parallel", "arbitrary"),
        name="norm_in_proj",
    )(x, gain.reshape(1, d).astype(F32), w)


def _attn_prep_kernel(q_ref, k_ref, v_ref, gq_ref, gk_ref, qo_ref, ko_ref, vo_ref):
    lane = lax.broadcasted_iota(I32, (1, V7X_LANES), 1)
    lo_mask = lane < QK_DIM

    def norm(src_ref, g_ref, a):
        x = src_ref[:, a * V7X_LANES:(a + 1) * V7X_LANES].astype(F32)
        ss = x * x
        lo = jnp.sum(jnp.where(lo_mask, ss, 0.0), axis=-1, keepdims=True)
        hi = jnp.sum(jnp.where(lo_mask, 0.0, ss), axis=-1, keepdims=True)
        ms = jnp.where(lo_mask, lo, hi) * (1.0 / QK_DIM)
        return x * lax.rsqrt(ms + NORM_EPS) * g_ref[...]

    for a in range(ATTN_HEADS):
        qo_ref[:, a * V7X_LANES:(a + 1) * V7X_LANES] = norm(q_ref, gq_ref, a).astype(qo_ref.dtype)
        ko_ref[a, 0] = norm(k_ref, gk_ref, a).astype(ko_ref.dtype)
        v = v_ref[:, a * V7X_LANES:(a + 1) * V7X_LANES].astype(F32)
        vo_ref[a, 0] = v.T.astype(vo_ref.dtype)


def attn_prep(proj, q_gain, k_gain, *, tk):
    s = proj.shape[0]
    nk = s // tk
    gq = (jnp.tile(q_gain.astype(F32), 2) * (QK_DIM ** -0.5 * LOG2E)).reshape(1, V7X_LANES)
    gk = jnp.tile(k_gain.astype(F32), 2).reshape(1, V7X_LANES)
    cq, ck, cv = COL_Q // QK_COLS, COL_K // QK_COLS, COL_V // ATTN_WIDTH
    return pl.pallas_call(
        _attn_prep_kernel,
        out_shape=(jax.ShapeDtypeStruct((s, QK_COLS), BF16),
                   jax.ShapeDtypeStruct((ATTN_HEADS, nk, tk, V_DIM), BF16),
                   jax.ShapeDtypeStruct((ATTN_HEADS, nk, V_DIM, tk), BF16)),
        grid=(nk,),
        in_specs=[pl.BlockSpec((tk, QK_COLS), lambda i: (i, cq)),
                  pl.BlockSpec((tk, QK_COLS), lambda i: (i, ck)),
                  pl.BlockSpec((tk, ATTN_WIDTH), lambda i: (i, cv)),
                  pl.BlockSpec((1, V7X_LANES), lambda i: (0, 0)),
                  pl.BlockSpec((1, V7X_LANES), lambda i: (0, 0))],
        out_specs=(pl.BlockSpec((tk, QK_COLS), lambda i: (i, 0)),
                   pl.BlockSpec((ATTN_HEADS, 1, tk, V_DIM), lambda i: (0, i, 0, 0)),
                   pl.BlockSpec((ATTN_HEADS, 1, V_DIM, tk), lambda i: (0, i, 0, 0))),
        compiler_params=_cp("parallel"),
        name="attn_prep",
    )(proj, proj, proj, gq, gk)


ATTN_NEAR = 3
ATTN_BIAS_TILES = 5
ATTN_UNROLL = 6


def _bias_tile_kernel(rb_ref, o_ref, *, tk, tq):
    h = pl.program_id(0)
    d = pl.program_id(1)
    w = tk + tq
    m = lax.broadcasted_iota(I32, (F32_SUBLANES, w), 1)
    rel = (d - ATTN_BIAS_TILES // 2) * tk + (tk - 1) - m
    half = REL_BUCKETS // 2
    exact = half // 2
    side = jnp.where(rel > 0, half, 0).astype(I32)
    n = jnp.abs(rel)
    nf = jnp.maximum(n, 1).astype(F32)
    large = exact + (jnp.log(nf / exact) / math.log(REL_MAX_DIST / exact) * (half - exact)).astype(I32)
    large = jnp.minimum(large, half - 1)
    bucket = side + jnp.where(n < exact, n, large).astype(I32)
    val = jnp.zeros((F32_SUBLANES, w), F32)
    for b in range(REL_BUCKETS):
        val = jnp.where(bucket == b, rb_ref[b, h], val)
    table = jnp.broadcast_to(val[0:1, :] * LOG2E, (tk, w))
    o_ref[0, 0] = pltpu.roll(table, w - tk + 1, 1, stride=1, stride_axis=0)[:, :tq]


def bias_tiles(rel_bias, *, tk, tq):
    assert tk == tq and (tk + tq) & (tk + tq - 1) == 0
    return pl.pallas_call(
        functools.partial(_bias_tile_kernel, tk=tk, tq=tq),
        out_shape=jax.ShapeDtypeStruct((ATTN_HEADS, ATTN_BIAS_TILES, tk, tq), F32),
        grid=(ATTN_HEADS, ATTN_BIAS_TILES),
        in_specs=[pl.BlockSpec(memory_space=pltpu.SMEM)],
        out_specs=pl.BlockSpec((1, 1, tk, tq), lambda h, d: (h, d, 0, 0)),
        compiler_params=_cp("parallel", "parallel"),
        name="t5_bias_tiles",
    )(rel_bias.astype(F32))


def _attn_kernel(lam_ref, far_ref, q_ref, k_ref, vt_ref, bias_ref, g_ref, o_ref,
                 s_ref, cm_ref, m_ref, l_ref, acc_ref, *, nk, tk, tq, out_scale):
    h = pl.program_id(0)
    i = pl.program_id(1)
    lane = lax.broadcasted_iota(I32, (1, V7X_LANES), 1)
    q = q_ref[...]
    zero = jnp.zeros_like(q)
    qmaps = (jnp.where(lane < QK_DIM, q, zero), jnp.where(lane < QK_DIM, zero, q))
    m_ref[...] = jnp.full(m_ref.shape, NEG_BIG, F32)
    l_ref[...] = jnp.zeros(l_ref.shape, F32)
    acc_ref[...] = jnp.zeros(acc_ref.shape, F32)

    def scores(j, slot, bias_tile, bias_const):
        kt = k_ref[0, j]
        for c in range(2):
            s = lax.dot_general(kt, qmaps[c], (((1,), (1,)), ((), ())),
                                preferred_element_type=F32)
            if bias_tile is not None:
                s = s + bias_tile
            s_ref[slot, c] = s
            cm_ref[slot, c:c + 1, :] = jnp.max(s, axis=0, keepdims=True) + bias_const
        cm_ref[slot, 2:3, :] = jnp.zeros((1, tq), F32) + bias_const

    def absorb(j, slot):
        vt = vt_ref[0, j]
        cb = cm_ref[slot, 2:3, :]
        for c in range(2):
            m_old = m_ref[c:c + 1, :]
            m_new = jnp.maximum(m_old, cm_ref[slot, c:c + 1, :])
            alpha = jnp.exp2(m_old - m_new)
            p = jnp.exp2(s_ref[slot, c] - (m_new - cb))
            l_ref[c:c + 1, :] = alpha * l_ref[c:c + 1, :] + jnp.sum(p, axis=0, keepdims=True)
            acc_ref[c] = alpha * acc_ref[c] + jnp.dot(vt, p.astype(BF16), preferred_element_type=F32)
            m_ref[c:c + 1, :] = m_new

    n0 = jnp.clip(i - 1, 0, nk - ATTN_NEAR)
    nfar = nk - ATTN_NEAR

    def far_tile(t):
        return jnp.where(t < n0, t, t + ATTN_NEAR)

    def far_const(j):
        return jnp.where(j < i, far_ref[h, 0], far_ref[h, 1])

    def near_bias(j):
        return bias_ref[0, j - i + ATTN_BIAS_TILES // 2]

    scores(n0, 0, near_bias(n0), 0.0)
    scores(n0 + 1, 1, near_bias(n0 + 1), 0.0)
    absorb(n0, 0)
    scores(n0 + 2, 0, near_bias(n0 + 2), 0.0)
    absorb(n0 + 1, 1)
    j0 = far_tile(0)
    scores(j0, 1, None, far_const(j0))
    absorb(n0 + 2, 0)

    def group(u, jprev):
        for w in range(ATTN_UNROLL):
            jn = far_tile(ATTN_UNROLL * u + w + 1)
            scores(jn, w % 2, None, far_const(jn))
            absorb(jprev, (w + 1) % 2)
            jprev = jn
        return jprev

    jlast = lax.fori_loop(0, (nfar - 1) // ATTN_UNROLL, group, j0)
    absorb(jlast, 1)

    o1 = acc_ref[0] / l_ref[0:1, :]
    o2 = acc_ref[1] / l_ref[1:2, :]
    o = o1 - lam_ref[0] * o2
    r = lax.rsqrt(jnp.mean(o * o, axis=0, keepdims=True) + NORM_EPS)
    o = o * r * g_ref[...] * out_scale
    o_ref[...] = o.T.astype(o_ref.dtype)


def diff_attention(qn, k4, vt4, bias, far_bias, lam, subln_gain, *, lam_init, tq, tk):
    s = qn.shape[0]
    nk = s // tk
    assert tq == tk and nk > ATTN_NEAR and (nk - ATTN_NEAR - 1) % ATTN_UNROLL == 0
    kern = functools.partial(_attn_kernel, nk=nk, tk=tk, tq=tq, out_scale=1.0 - lam_init)
    return pl.pallas_call(
        kern,
        out_shape=jax.ShapeDtypeStruct((s, ATTN_WIDTH), BF16),
        grid=(ATTN_HEADS, s // tq),
        in_specs=[pl.BlockSpec(memory_space=pltpu.SMEM),
                  pl.BlockSpec(memory_space=pltpu.SMEM),
                  pl.BlockSpec((tq, V_DIM), lambda h, i: (i, h)),
                  pl.BlockSpec((1, nk, tk, V_DIM), lambda h, i: (h, 0, 0, 0)),
                  pl.BlockSpec((1, nk, V_DIM, tk), lambda h, i: (h, 0, 0, 0)),
                  pl.BlockSpec((1, ATTN_BIAS_TILES, tk, tq), lambda h, i: (h, 0, 0, 0)),
                  pl.BlockSpec((V_DIM, 1), lambda h, i: (0, 0))],
        out_specs=pl.BlockSpec((tq, V_DIM), lambda h, i: (i, h)),
        scratch_shapes=[pltpu.VMEM((2, 2, tk, tq), F32),
                        pltpu.VMEM((2, F32_SUBLANES, tq), F32),
                        pltpu.VMEM((F32_SUBLANES, tq), F32),
                        pltpu.VMEM((F32_SUBLANES, tq), F32),
                        pltpu.VMEM((2, V_DIM, tq), F32)],
        compiler_params=_cp("parallel", "arbitrary"),
        name="diff_attention",
    )(lam.reshape(1).astype(F32), far_bias, qn, k4, vt4, bias, subln_gain.reshape(V_DIM, 1).astype(F32))


def s5_operators(a_re, a_im, log_dt, b_re, b_im, c_re, c_im):
    t_len, hp = S5_CHUNK, lax.Precision.HIGHEST
    a_re, a_im = a_re.astype(F32), a_im.astype(F32)
    dt = jnp.exp(log_dt.astype(F32))[..., None]
    steps = jnp.arange(t_len + 1, dtype=F32)[:, None, None, None]
    mag = jnp.exp(a_re * dt * steps)
    ang = a_im * dt * steps
    pw_re, pw_im = mag * jnp.cos(ang), mag * jnp.sin(ang)
    den = a_re * a_re + a_im * a_im
    nr, ni = pw_re[1] - 1.0, pw_im[1]
    coef_re = ((nr * a_re + ni * a_im) / den)[..., None]
    coef_im = ((ni * a_re - nr * a_im) / den)[..., None]
    b_re, b_im = b_re.astype(F32), b_im.astype(F32)
    bb_re = coef_re * b_re - coef_im * b_im
    bb_im = coef_re * b_im + coef_im * b_re
    c_re, c_im = c_re.astype(F32), c_im.astype(F32)

    g, n_st, p_ch = a_re.shape[1], SSM_STATE, SSM_GROUP
    tp = t_len * p_ch
    pwt_re = jnp.transpose(pw_re, (1, 2, 3, 0))
    pwt_im = jnp.transpose(pw_im, (1, 2, 3, 0))
    ct_re = jnp.transpose(c_re, (0, 1, 3, 2))
    ct_im = jnp.transpose(c_im, (0, 1, 3, 2))
    bbt_re = jnp.transpose(bb_re, (0, 1, 3, 2))
    bbt_im = jnp.transpose(bb_im, (0, 1, 3, 2))
    cp_re = ct_re[:, :, :, None, :] * pwt_re[..., None] - ct_im[:, :, :, None, :] * pwt_im[..., None]
    cp_im = ct_re[:, :, :, None, :] * pwt_im[..., None] + ct_im[:, :, :, None, :] * pwt_re[..., None]

    zlag = jnp.zeros((g, n_st, t_len - 1, p_ch), F32)

    def lagged(cp):
        return (jnp.concatenate([zlag, cp[0, :, :, :t_len]], axis=2),
                jnp.concatenate([jnp.flip(cp[1, :, :, :t_len], axis=2), zlag], axis=2))

    rf_re, rb_re = lagged(cp_re)
    rf_im, rb_im = lagged(cp_im)
    r_cat = jnp.concatenate([rf_re, rf_im, rb_re, rb_im], axis=1).reshape(g, 4 * n_st, (2 * t_len - 1) * p_ch)
    a_cat = jnp.concatenate([bbt_re[0], -bbt_im[0], bbt_re[1], -bbt_im[1]], axis=-1)
    kp = jnp.einsum('gpk,gkx->gpx', a_cat, r_cat, precision=hp)
    toep = jnp.stack([kp[:, :, (t_len - 1 - j) * p_ch:(t_len - 1 - j) * p_ch + tp] for j in range(t_len)],
                     axis=1).reshape(g, tp, tp).astype(BF16)

    def seg_powers(d, reverse):
        pr = jnp.transpose(pw_re[:t_len, d], (1, 0, 2))
        pi = jnp.transpose(pw_im[:t_len, d], (1, 0, 2))
        if reverse:
            pr, pi = jnp.flip(pr, axis=1), jnp.flip(pi, axis=1)
        return pr, pi

    prf, pif = seg_powers(0, True)
    prb, pib = seg_powers(1, False)
    pa = jnp.concatenate([prf, pif, pif, prf, prb, pib, pib, prb], axis=-1)
    pb = jnp.concatenate([-pif, prf, prf, -pif, -pib, prb, prb, -pib], axis=-1)
    br = jnp.concatenate([bbt_re[0]] * 4 + [bbt_re[1]] * 4, axis=-1)
    bi = jnp.concatenate([bbt_im[0]] * 4 + [bbt_im[1]] * 4, axis=-1)
    smap = (br[:, None] * pa[:, :, None, :] + bi[:, None] * pb[:, :, None, :]).reshape(g, tp, 8 * n_st)
    smap = smap.astype(BF16)

    def out_map(d, reverse):
        wr, wi = cp_re[d][:, :, 1:t_len + 1], cp_im[d][:, :, 1:t_len + 1]
        if reverse:
            wr, wi = jnp.flip(wr, axis=2), jnp.flip(wi, axis=2)
        return [wr.reshape(g, n_st, tp), -wi.reshape(g, n_st, tp)]

    mc = jnp.concatenate(out_map(0, False) + out_map(1, True), axis=1).astype(BF16)

    def carry(d):
        ar, ai = pw_re[t_len, d], pw_im[t_len, d]
        return [jnp.concatenate([ar, ar], -1), jnp.concatenate([-ai, ai], -1),
                jnp.concatenate([ai, -ai], -1)]

    coef = jnp.stack(carry(0) + carry(1), axis=0)
    return toep, smap, mc, coef


def _gelu_tanh(x):
    return 0.5 * x * (1.0 + jnp.tanh(math.sqrt(2.0 / math.pi) * (x + 0.044715 * (x * x * x))))


def _s5_kernel(x_ref, sel_ref, toep_ref, smap_ref, mc_ref, coef_ref, d_ref, y_ref,
               u_ref, ef_ref, efs_ref, eb_ref, ebs_ref, *, cn):
    gb = toep_ref.shape[0]
    tp = toep_ref.shape[1]
    n2 = 2 * SSM_STATE
    xcat = x_ref[0]

    def sel(g):
        off = (gb - 1 - g) * SSM_GROUP
        return sel_ref[off:off + xcat.shape[1], :]

    for g in range(gb):
        u_ref[g] = jnp.dot(xcat, sel(g), preferred_element_type=F32).astype(BF16)
        e = jnp.dot(u_ref[g], smap_ref[g], preferred_element_type=F32)
        for r, ref in enumerate((ef_ref, efs_ref, eb_ref, ebs_ref)):
            ref[pl.ds(g, cn, stride=gb), :] = e[:, r * n2:(r + 1) * n2]

    cf, cfs, cfw = coef_ref[0], coef_ref[1], coef_ref[2]
    cb, cbs, cbw = coef_ref[3], coef_ref[4], coef_ref[5]

    def step(c, carry):
        s, sw, r, rw = carry
        fo = pl.multiple_of(c * gb, gb)
        bo = pl.multiple_of((cn - 1 - c) * gb, gb)
        e, es = ef_ref[pl.ds(fo, gb), :], efs_ref[pl.ds(fo, gb), :]
        ef_ref[pl.ds(fo, gb), :] = s
        s, sw = cf * s + cfs * sw + e, cf * sw + cfw * s + es
        e, es = eb_ref[pl.ds(bo, gb), :], ebs_ref[pl.ds(bo, gb), :]
        eb_ref[pl.ds(bo, gb), :] = r
        r, rw = cb * r + cbs * rw + e, cb * rw + cbw * r + es
        return s, sw, r, rw

    z = jnp.zeros((gb, n2), F32)
    lax.fori_loop(0, cn, step, (z, z, z, z))

    ycat = None
    for g in range(gb):
        u = u_ref[g]
        st = jnp.concatenate([ef_ref[pl.ds(g, cn, stride=gb), :], eb_ref[pl.ds(g, cn, stride=gb), :]],
                             axis=1).astype(BF16)
        y = (jnp.dot(u, toep_ref[g], preferred_element_type=F32)
             + jnp.dot(st, mc_ref[g], preferred_element_type=F32)
             + u.astype(F32) * d_ref[g])
        placed = lax.dot_general(_gelu_tanh(y).astype(BF16), sel(g), (((1,), (1,)), ((), ())),
                                 preferred_element_type=F32)
        ycat = placed if ycat is None else ycat + placed
    y_ref[0] = ycat.astype(y_ref.dtype)


def s5_lane_selector():
    gb, p, t_len = S5_GB, SSM_GROUP, S5_CHUNK
    r = jnp.arange(t_len * V7X_LANES + (gb - 1) * p)[:, None] - (gb - 1) * p
    c = jnp.arange(t_len * p)[None, :]
    hit = (r >= 0) & (r // V7X_LANES == c // p) & ((r % V7X_LANES) // p == 0) & (r % p == c % p)
    return hit.astype(BF16)


def s5_mixer(xcat, ops, ssm_d):
    toep, smap, mc, coef = ops
    g, p, t_len = SSM_GROUPS, SSM_GROUP, S5_CHUNK
    nslab, cn, _ = xcat.shape
    tp = t_len * p
    gb = S5_GB
    assert gb * p == V7X_LANES and nslab == g // gb
    sel = s5_lane_selector()
    dsk = jnp.tile(ssm_d.astype(F32).reshape(g, 1, p), (1, 1, t_len))
    ycat = pl.pallas_call(
        functools.partial(_s5_kernel, cn=cn),
        out_shape=jax.ShapeDtypeStruct((nslab, cn, t_len * V7X_LANES), BF16),
        grid=(nslab,),
        in_specs=[pl.BlockSpec((1, cn, t_len * V7X_LANES), lambda i: (i, 0, 0)),
                  pl.BlockSpec(sel.shape, lambda i: (0, 0)),
                  pl.BlockSpec((gb, tp, tp), lambda i: (i, 0, 0)),
                  pl.BlockSpec((gb, tp, smap.shape[2]), lambda i: (i, 0, 0)),
                  pl.BlockSpec((gb, mc.shape[1], tp), lambda i: (i, 0, 0)),
                  pl.BlockSpec((6, gb, 2 * SSM_STATE), lambda i: (0, i, 0)),
                  pl.BlockSpec((gb, 1, tp), lambda i: (i, 0, 0))],
        out_specs=pl.BlockSpec((1, cn, t_len * V7X_LANES), lambda i: (i, 0, 0)),
        scratch_shapes=[pltpu.VMEM((gb, cn, tp), BF16)]
        + [pltpu.VMEM((cn * gb, 2 * SSM_STATE), F32) for _ in range(4)],
        compiler_params=_cp("parallel"),
        name="s5_chunked_scan",
    )(xcat, sel, toep, smap, mc, coef, dsk)
    return ycat


def _sigmoid(x):
    return 1.0 / (1.0 + jnp.exp(-x))


def _glu_kernel(ycat_ref, w_ref, o_ref, y_ref):
    cn = ycat_ref.shape[1]
    for slab in range(ycat_ref.shape[0]):
        for t in range(S5_CHUNK):
            piece = ycat_ref[slab, :, t * V7X_LANES:(t + 1) * V7X_LANES].astype(F32)
            y_ref[slab, pl.ds(t, cn, stride=S5_CHUNK), :] = piece
    y = jnp.concatenate([y_ref[slab] for slab in range(ycat_ref.shape[0])], axis=1)
    z = jnp.dot(y.astype(BF16), w_ref[0].astype(BF16), preferred_element_type=F32)
    o_ref[...] = (y * _sigmoid(z)).astype(o_ref.dtype)


def glu(ycat, w, layer, *, tm):
    nslab, cn, _ = ycat.shape
    s, d = cn * S5_CHUNK, nslab * V7X_LANES
    return pl.pallas_call(
        _glu_kernel,
        out_shape=jax.ShapeDtypeStruct((s, d), BF16),
        grid=(s // tm,),
        in_specs=[pl.BlockSpec((nslab, tm // S5_CHUNK, S5_CHUNK * V7X_LANES), lambda i: (0, i, 0)),
                  pl.BlockSpec((1, d, d), lambda i: (layer, 0, 0))],
        out_specs=pl.BlockSpec((tm, d), lambda i: (i, 0)),
        scratch_shapes=[pltpu.VMEM((nslab, tm, V7X_LANES), F32)],
        compiler_params=_cp("parallel"),
        name="half_glu",
    )(ycat, w)


def _merge_kernel(yg_ref, ao_ref, gs_ref, ga_ref, ws_ref, wa_ref, o_ref):
    a = jnp.dot(yg_ref[...], ws_ref[0].astype(BF16), preferred_element_type=F32)
    b = jnp.dot(ao_ref[...], wa_ref[0].astype(BF16), preferred_element_type=F32)
    o = _sigmoid(gs_ref[...].astype(F32)) * a + _sigmoid(ga_ref[...].astype(F32)) * b
    o_ref[...] = o.astype(o_ref.dtype)


def gated_merge(yg, ao, proj, ws, wa, layer, *, tm, tn):
    s, k = yg.shape
    n = ws.shape[2]
    cs, ca = COL_GS // tn, COL_GA // tn
    return pl.pallas_call(
        _merge_kernel,
        out_shape=jax.ShapeDtypeStruct((s, n), BF16),
        grid=(s // tm, n // tn),
        in_specs=[pl.BlockSpec((tm, k), lambda i, j: (i, 0)),
                  pl.BlockSpec((tm, k), lambda i, j: (i, 0)),
                  pl.BlockSpec((tm, tn), lambda i, j: (i, cs + j)),
                  pl.BlockSpec((tm, tn), lambda i, j: (i, ca + j)),
                  pl.BlockSpec((1, k, tn), lambda i, j: (layer, 0, j)),
                  pl.BlockSpec((1, k, tn), lambda i, j: (layer, 0, j))],
        out_specs=pl.BlockSpec((tm, tn), lambda i, j: (i, j)),
        compiler_params=_cp("parallel", "arbitrary"),
        name="gated_merge",
    )(yg, ao, proj, proj, ws, wa)


def _out_router_kernel(m_ref, x_ref, w_ref, g_ref, wrt_ref, xo_ref, h_ref, afft_ref):
    x1 = x_ref[...] + jnp.dot(m_ref[...], w_ref[...], preferred_element_type=F32)
    xo_ref[...] = x1
    r = lax.rsqrt(jnp.mean(x1 * x1, axis=-1, keepdims=True) + NORM_EPS)
    h = x1 * r * g_ref[...]
    h_ref[...] = h
    lgt = lax.dot_general(wrt_ref[...], h, (((1,), (1,)), ((), ())),
                          preferred_element_type=F32, precision=lax.Precision.HIGHEST)
    et = jnp.exp(lgt - jnp.max(lgt, axis=0, keepdims=True))
    afft_ref[...] = et / jnp.sum(et, axis=0, keepdims=True)


def out_proj_router(merged, x, w_out, gain, w_router, *, tm):
    s, d = x.shape
    e = w_router.shape[1]
    return pl.pallas_call(
        _out_router_kernel,
        out_shape=(jax.ShapeDtypeStruct((s, d), F32), jax.ShapeDtypeStruct((s, d), F32),
                   jax.ShapeDtypeStruct((e, s), F32)),
        grid=(s // tm,),
        in_specs=[pl.BlockSpec((tm, d), lambda i: (i, 0)),
                  pl.BlockSpec((tm, d), lambda i: (i, 0)),
                  pl.BlockSpec((d, d), lambda i: (0, 0)),
                  pl.BlockSpec((1, d), lambda i: (0, 0)),
                  pl.BlockSpec((e, d), lambda i: (0, 0))],
        out_specs=(pl.BlockSpec((tm, d), lambda i: (i, 0)),
                   pl.BlockSpec((tm, d), lambda i: (i, 0)),
                   pl.BlockSpec((e, tm), lambda i: (0, i))),
        compiler_params=_cp("parallel"),
        name="out_proj_router",
    )(merged, x, w_out, gain.reshape(1, d).astype(F32), w_router.astype(F32).T)


SLOT_LO_BITS = 6
SLOT_LO = 1 << SLOT_LO_BITS


def _select_kernel(afft_ref, slot_ref, cum_ref, idx_ref, acc_ref, *, s, cap, blk):
    ne = afft_ref.shape[0]

    def bit_body(b, thr):
        cand = thr | jnp.left_shift(jnp.ones((ne, 1), I32), 30 - b)
        keys = pltpu.bitcast(afft_ref[...], I32)
        cnt = jnp.sum((keys >= cand).astype(I32), axis=1, keepdims=True)
        return jnp.where(cnt >= cap, cand, thr)

    thr = lax.fori_loop(0, 31, bit_body, jnp.zeros((ne, 1), I32))
    keys = pltpu.bitcast(afft_ref[...], I32)
    need = cap - jnp.sum((keys > thr).astype(I32), axis=1, keepdims=True)

    ri = lax.broadcasted_iota(I32, (blk, blk), 0)
    ci = lax.broadcasted_iota(I32, (blk, blk), 1)
    upper = jnp.where(ri < ci, 1.0, 0.0).astype(BF16)
    na = cap // SLOT_LO
    acol = lax.broadcasted_iota(I32, (na, 1), 0)
    bcol = lax.broadcasted_iota(I32, (SLOT_LO, 1), 0)
    tlane = lax.broadcasted_iota(I32, (1, blk), 1)
    acc_ref[...] = jnp.zeros(acc_ref.shape, F32)

    def blk_body(b, carry):
        ceq, csel = carry
        off = pl.multiple_of(b * blk, blk)
        kb = pltpu.bitcast(afft_ref[:, pl.ds(off, blk)], I32)
        gt = kb > thr
        eq = kb == thr
        eqf = jnp.where(eq, 1.0, 0.0)
        rank_eq = jnp.dot(eqf.astype(BF16), upper, preferred_element_type=F32) + ceq
        sel = jnp.logical_or(gt, jnp.logical_and(eq, rank_eq < need.astype(F32)))
        self_ = jnp.where(sel, 1.0, 0.0)
        cum = jnp.dot(self_.astype(BF16), upper, preferred_element_type=F32) + csel
        cum_i = cum.astype(I32)
        cum_ref[:, pl.ds(off, blk)] = cum_i
        slot = jnp.where(sel, cum_i, -1)
        slot_ref[:, pl.ds(off, blk)] = slot
        tok = off + tlane
        hi = (tok // V7X_LANES).astype(F32)
        lo = (tok % V7X_LANES).astype(F32)
        for e in range(ne):
            srow = slot[e:e + 1, :]
            in_a = lax.shift_right_arithmetic(srow, SLOT_LO_BITS) == acol
            lhs = jnp.concatenate([jnp.where(in_a, hi, 0.0), jnp.where(in_a, lo, 0.0)],
                                  axis=0).astype(BF16)
            rhs = jnp.where((srow & (SLOT_LO - 1)) == bcol, 1.0, 0.0).astype(BF16)
            acc_ref[e] += lax.dot_general(lhs, rhs, (((1,), (1,)), ((), ())), preferred_element_type=F32)
        return (ceq + jnp.sum(eqf, axis=1, keepdims=True), csel + jnp.sum(self_, axis=1, keepdims=True))

    z = jnp.zeros((ne, 1), F32)
    lax.fori_loop(0, s // blk, blk_body, (z, z))
    a = acc_ref[...]
    idx_ref[...] = (a[:, :na, :] * float(V7X_LANES) + a[:, na:, :]).astype(I32)


def expert_select(afft, *, cap, blk):
    ne, s = afft.shape
    assert cap % SLOT_LO == 0
    return pl.pallas_call(
        functools.partial(_select_kernel, s=s, cap=cap, blk=blk),
        out_shape=(jax.ShapeDtypeStruct((ne, s), I32), jax.ShapeDtypeStruct((ne, s), I32),
                   jax.ShapeDtypeStruct((ne, cap // SLOT_LO, SLOT_LO), I32)),
        scratch_shapes=[pltpu.VMEM((ne, 2 * (cap // SLOT_LO), SLOT_LO), F32)],
        compiler_params=pltpu.CompilerParams(vmem_limit_bytes=VMEM_LIMIT),
        name="expert_select",
    )(afft)


GATHER_UNROLL = 8


def _ffn_kernel(idx_ref, h_hbm, wg_ref, wu_ref, wd_ref, y_ref, xg32_ref, xg_ref, sem, *, cap):
    e = pl.program_id(0)
    f = pl.program_id(1)

    def row_copy(r):
        tok = idx_ref[e * cap + r]
        return pltpu.make_async_copy(h_hbm.at[pl.ds(tok, 1)], xg32_ref.at[pl.ds(r, 1)], sem)

    @pl.when(f == 0)
    def _():
        def start(rb, c):
            for w in range(GATHER_UNROLL):
                row_copy(rb * GATHER_UNROLL + w).start()
            return c

        lax.fori_loop(0, cap // GATHER_UNROLL, start, 0)
        y_ref[...] = jnp.zeros(y_ref.shape, F32)
        pltpu.make_async_copy(h_hbm.at[pl.ds(0, cap)], xg32_ref, sem).wait()
        xg_ref[...] = xg32_ref[...].astype(BF16)

    xg = xg_ref[...]
    a = jnp.dot(xg, wg_ref[0, 0].astype(BF16), preferred_element_type=F32)
    b = jnp.dot(xg, wu_ref[0, 0].astype(BF16), preferred_element_type=F32)
    hid = (a * _sigmoid(a) * b).astype(BF16)
    y_ref[0, 0:cap, :] += jnp.dot(hid, wd_ref[0, 0].astype(BF16), preferred_element_type=F32)


def expert_ffn(idx, h2, wg, wu, wd, layer, *, cap, pad, fc):
    _, ne, d, ff = wg.shape
    grid_spec = pltpu.PrefetchScalarGridSpec(
        num_scalar_prefetch=1,
        grid=(ne, ff // fc),
        in_specs=[pl.BlockSpec(memory_space=pl.ANY),
                  pl.BlockSpec((1, 1, d, fc), lambda e, f, idx: (layer, e, 0, f)),
                  pl.BlockSpec((1, 1, d, fc), lambda e, f, idx: (layer, e, 0, f)),
                  pl.BlockSpec((1, 1, fc, d), lambda e, f, idx: (layer, e, f, 0))],
        out_specs=pl.BlockSpec((1, cap + pad, d), lambda e, f, idx: (e, 0, 0)),
        scratch_shapes=[pltpu.VMEM((cap, d), F32), pltpu.VMEM((cap, d), BF16),
                        pltpu.SemaphoreType.DMA(())],
    )
    return pl.pallas_call(
        functools.partial(_ffn_kernel, cap=cap),
        out_shape=jax.ShapeDtypeStruct((ne, cap + pad, d), F32),
        grid_spec=grid_spec,
        compiler_params=_cp("arbitrary", "arbitrary"),
        name="expert_ffn",
    )(idx.reshape(-1), h2, wg, wu, wd)


def _combine_kernel(st_ref, x_ref, aff_ref, slot_ref, y_hbm, o_ref, buf_ref, xbuf_ref, sem, xsem,
                    *, ne, rows, nt):
    t = pl.program_id(0)
    tm = x_ref.shape[0]

    def chunk_start(tt, e, c):
        st8 = (st_ref[tt * ne + e] // F32_SUBLANES) * F32_SUBLANES
        return pl.multiple_of(st8 + c * rows, F32_SUBLANES)

    def first_copy(tt, e, slot):
        return pltpu.make_async_copy(y_hbm.at[e, pl.ds(chunk_start(tt, e, 0), rows)],
                                     buf_ref.at[slot, e], sem.at[slot, e])

    @pl.when(t == 0)
    def _():
        for e in range(ne):
            first_copy(0, e, 0).start()

    @pl.when(t + 1 < nt)
    def _():
        for e in range(ne):
            first_copy(t + 1, e, (t + 1) % 2).start()

    slot = t % 2
    for e in range(ne):
        first_copy(t, e, slot).wait()

    pair = V7X_LANES // rows
    lane = lax.broadcasted_iota(I32, (1, V7X_LANES), 1)
    hi_parts, lo_parts = [], []
    for a in range(ne // pair):
        rel = jnp.zeros((tm, V7X_LANES), I32)
        gate = jnp.zeros((tm, V7X_LANES), F32)
        for b in range(pair):
            e = a * pair + b
            in_e = jnp.logical_and(lane >= b * rows, lane < (b + 1) * rows)
            rel = jnp.where(in_e, slot_ref[:, e:e + 1] - chunk_start(t, e, 0) + b * rows, rel)
            gate = jnp.where(in_e, aff_ref[:, e:e + 1], gate)
        gate = jnp.where(rel == lane, gate, 0.0)
        g_hi = gate.astype(BF16)
        hi_parts.append(g_hi)
        lo_parts.append((gate - g_hi.astype(F32)).astype(BF16))
    rhs = buf_ref[slot].reshape(ne * rows, x_ref.shape[1]).astype(BF16)
    o_ref[...] = (x_ref[...]
                  + jnp.dot(jnp.concatenate(hi_parts, axis=1), rhs, preferred_element_type=F32)
                  + jnp.dot(jnp.concatenate(lo_parts, axis=1), rhs, preferred_element_type=F32))

    lane_r = lax.broadcasted_iota(I32, (1, rows), 1)
    for e in range(ne):
        end = st_ref[(t + 1) * ne + e]
        for c in range(1, tm // rows + 1):

            @pl.when(end > chunk_start(t, e, c))
            def _():
                cp = pltpu.make_async_copy(y_hbm.at[e, pl.ds(chunk_start(t, e, c), rows)], xbuf_ref, xsem)
                cp.start()
                cp.wait()
                rel = slot_ref[:, e:e + 1] - chunk_start(t, e, c)
                oh = jnp.where(rel == lane_r, 1.0, 0.0).astype(BF16)
                contrib = jnp.dot(oh, xbuf_ref[...].astype(BF16), preferred_element_type=F32)
                o_ref[...] += aff_ref[:, e:e + 1] * contrib


def moe_combine(starts, x1, aff, slot, ye, *, tm, rows):
    s, d = x1.shape
    ne = aff.shape[1]
    nt = s // tm
    assert V7X_LANES % rows == 0 and ne % (V7X_LANES // rows) == 0
    grid_spec = pltpu.PrefetchScalarGridSpec(
        num_scalar_prefetch=1,
        grid=(nt,),
        in_specs=[pl.BlockSpec((tm, d), lambda t, st: (t, 0)),
                  pl.BlockSpec((tm, ne), lambda t, st: (t, 0)),
                  pl.BlockSpec((tm, ne), lambda t, st: (t, 0)),
                  pl.BlockSpec(memory_space=pl.ANY)],
        out_specs=pl.BlockSpec((tm, d), lambda t, st: (t, 0)),
        scratch_shapes=[pltpu.VMEM((2, ne, rows, d), F32), pltpu.VMEM((rows, d), F32),
                        pltpu.SemaphoreType.DMA((2, ne)), pltpu.SemaphoreType.DMA(())],
    )
    return pl.pallas_call(
        functools.partial(_combine_kernel, ne=ne, rows=rows, nt=nt),
        out_shape=jax.ShapeDtypeStruct((s, d), F32),
        grid_spec=grid_spec,
        compiler_params=_cp("arbitrary"),
        name="moe_combine",
    )(starts.reshape(-1), x1, aff, slot, ye)


TM_PROJ, TN_PROJ = 1024, 1024
TM_ROW = 256
TM_OUT = 512
ATTN_TILE = 512
SEL_BLK = 256
FFN_CHUNK = 256
COMBINE_ROWS = 64


def _layer(x, l, p, bias, far_bias):
    s = x.shape[0]
    lam_init = 0.8 - 0.6 * math.exp(-0.3 * l)
    proj, xcat = norm_matmul(x, p["norm_mix"], p["w_in"].astype(BF16), tm=min(TM_PROJ, s), tn=TN_PROJ)

    ops = s5_operators(p["ssm_a_re"], p["ssm_a_im"], p["ssm_log_dt"], p["ssm_b_re"], p["ssm_b_im"],
                       p["ssm_c_re"], p["ssm_c_im"])
    ycat = s5_mixer(xcat, ops, p["ssm_d"])
    yg = glu(ycat, p["w_glu"], l, tm=min(TM_PROJ, s))

    qn, k4, vt4 = attn_prep(proj, p["q_gain"], p["k_gain"], tk=ATTN_TILE)
    lam = (jnp.exp(jnp.sum(p["lambda_q1"].astype(F32) * p["lambda_k1"].astype(F32)))
           - jnp.exp(jnp.sum(p["lambda_q2"].astype(F32) * p["lambda_k2"].astype(F32))) + lam_init)
    ao = diff_attention(qn, k4, vt4, bias, far_bias, lam, p["subln_gain"], lam_init=lam_init,
                        tq=ATTN_TILE, tk=ATTN_TILE)

    merged = gated_merge(yg, ao, proj, p["w_ssm_branch"], p["w_attn_branch"], l,
                         tm=min(TM_PROJ, s), tn=TN_PROJ)
    x1, h2, afft = out_proj_router(merged, x, p["w_out"].astype(BF16), p["norm_ffn"], p["w_router"],
                                   tm=min(TM_OUT, s))

    cap = CAPACITY_FACTOR * s // N_EXPERTS
    slot_t, cum_t, idx = expert_select(afft, cap=cap, blk=SEL_BLK)
    ye = expert_ffn(idx, h2, p["w_expert_gate"], p["w_expert_up"], p["w_expert_down"], l,
                    cap=cap, pad=COMBINE_ROWS, fc=FFN_CHUNK)
    starts = jnp.concatenate([cum_t[:, ::TM_ROW].T, jnp.full((1, N_EXPERTS), cap, I32)], axis=0)
    return moe_combine(starts, x1, afft.T, slot_t.T, ye, tm=TM_ROW, rows=COMBINE_ROWS)


_LAYER_PARAMS = ("w_in", "ssm_a_re", "ssm_a_im", "ssm_log_dt", "ssm_b_re", "ssm_b_im", "ssm_c_re",
                 "ssm_c_im", "ssm_d", "q_gain", "k_gain", "lambda_q1",
                 "lambda_k1", "lambda_q2", "lambda_k2", "subln_gain", "w_out",
                 "norm_mix", "norm_ffn", "w_router")
_STACKED_PARAMS = ("w_expert_gate", "w_expert_up", "w_expert_down", "w_glu", "w_ssm_branch", "w_attn_branch")


def kernel(x, w_in, ssm_a_re, ssm_a_im, ssm_log_dt, ssm_b_re, ssm_b_im, ssm_c_re, ssm_c_im, ssm_d, w_glu, w_ssm_branch, q_gain, k_gain, lambda_q1, lambda_k1, lambda_q2, lambda_k2, subln_gain, w_attn_branch, rel_bias, w_out, norm_mix, norm_ffn, w_router, w_expert_gate, w_expert_up, w_expert_down):
    args = dict(locals())
    b = x.shape[0]
    bias = bias_tiles(rel_bias, tk=ATTN_TILE, tq=ATTN_TILE)
    half = REL_BUCKETS // 2
    far_bias = jnp.stack([rel_bias[half - 1], rel_bias[REL_BUCKETS - 1]], axis=1).astype(F32) * LOG2E
    outs = []
    for bi in range(b):
        xb = x[bi].astype(F32)
        for l in range(DEPTH):
            p = {k: args[k][l] for k in _LAYER_PARAMS}
            p.update({k: args[k] for k in _STACKED_PARAMS})
            xb = _layer(xb, l, p, bias, far_bias)
        outs.append(xb)
    return jnp.stack(outs, axis=0).astype(x.dtype)
```

```python
import functools
import math

import jax
import jax.numpy as jnp
import numpy as np
from jax import lax
from jax.experimental import pallas as pl
from jax.experimental.pallas import tpu as pltpu

F32 = jnp.float32
BF16 = jnp.bfloat16
I32 = jnp.int32

D_MODEL = 2048
DEPTH = 2
SSM_WIDTH = D_MODEL // 2
SSM_GROUP = 16
SSM_GROUPS = SSM_WIDTH // SSM_GROUP
SSM_STATE = 64
ATTN_HEADS = 8
QK_DIM = 64
V_DIM = 2 * QK_DIM
ATTN_WIDTH = ATTN_HEADS * V_DIM
QK_COLS = ATTN_HEADS * 2 * QK_DIM
REL_BUCKETS = 32
REL_MAX_DIST = 128
N_EXPERTS = 16
CAPACITY_FACTOR = 2
EXPERT_FF = D_MODEL
NORM_EPS = 1e-6
IN_COLS = SSM_WIDTH + 2 * QK_COLS + ATTN_WIDTH + 2 * D_MODEL
COL_Q = SSM_WIDTH
COL_K = COL_Q + QK_COLS
COL_V = COL_K + QK_COLS
COL_GS = COL_V + ATTN_WIDTH
COL_GA = COL_GS + D_MODEL

V7X_LANES = 128
F32_SUBLANES = 8
V7X_VMEM_BYTES = 64 * 1024 * 1024
VMEM_LIMIT = 56 * 1024 * 1024
LOG2E = 1.4426950408889634

S5_CHUNK = 16
S5_GB = 8
NEG_BIG = -1e30


def _cp(*sem):
    return pltpu.CompilerParams(dimension_semantics=sem, vmem_limit_bytes=VMEM_LIMIT)


def _norm_matmul_kernel(x_ref, g_ref, w_ref, o_ref, xcat_ref, h_ref, u_ref):
    j = pl.program_id(1)

    @pl.when(j == 0)
    def _():
        x = x_ref[...]
        r = lax.rsqrt(jnp.mean(x * x, axis=-1, keepdims=True) + NORM_EPS)
        h_ref[...] = (x * r * g_ref[...]).astype(BF16)

    res = jnp.dot(h_ref[...], w_ref[...], preferred_element_type=F32)
    o_ref[...] = res.astype(o_ref.dtype)

    @pl.when(j == 0)
    def _():
        cn = u_ref.shape[1] // S5_CHUNK
        for slab in range(u_ref.shape[0]):
            u_ref[slab] = res[:, slab * V7X_LANES:(slab + 1) * V7X_LANES]
            for t in range(S5_CHUNK):
                piece = u_ref[slab, pl.ds(t, cn, stride=S5_CHUNK), :]
                xcat_ref[slab, :, t * V7X_LANES:(t + 1) * V7X_LANES] = piece.astype(xcat_ref.dtype)


def norm_matmul(x, gain, w, *, tm, tn):
    s, d = x.shape
    n = w.shape[1]
    assert tn == SSM_WIDTH and tm % (S5_CHUNK * F32_SUBLANES) == 0
    nslab = SSM_WIDTH // V7X_LANES
    return pl.pallas_call(
        _norm_matmul_kernel,
        out_shape=(jax.ShapeDtypeStruct((s, n), BF16),
                   jax.ShapeDtypeStruct((nslab, s // S5_CHUNK, S5_CHUNK * V7X_LANES), BF16)),
        grid=(s // tm, n // tn),
        in_specs=[pl.BlockSpec((tm, d), lambda i, j: (i, 0)),
                  pl.BlockSpec((1, d), lambda i, j: (0, 0)),
                  pl.BlockSpec((d, tn), lambda i, j: (0, j))],
        out_specs=(pl.BlockSpec((tm, tn), lambda i, j: (i, j)),
                   pl.BlockSpec((nslab, tm // S5_CHUNK, S5_CHUNK * V7X_LANES), lambda i, j: (0, i, 0))),
        scratch_shapes=[pltpu.VMEM((tm, d), BF16), pltpu.VMEM((nslab, tm, V7X_LANES), F32)],
        compiler_params=_cp("parallel", "arbitrary"),
        name="norm_in_proj",
    )(x, gain.reshape(1, d).astype(F32), w)


def _attn_prep_kernel(q_ref, k_ref, v_ref, gq_ref, gk_ref, qo_ref, ko_ref, vo_ref, kn_ref):
    lane = lax.broadcasted_iota(I32, (1, V7X_LANES), 1)
    lo_mask = lane < QK_DIM

    def norm(src_ref, g_ref, a):
        x = src_ref[:, a * V7X_LANES:(a + 1) * V7X_LANES].astype(F32)
        ss = x * x
        lo = jnp.sum(jnp.where(lo_mask, ss, 0.0), axis=-1, keepdims=True)
        hi = jnp.sum(jnp.where(lo_mask, 0.0, ss), axis=-1, keepdims=True)
        ms = jnp.where(lo_mask, lo, hi) * (1.0 / QK_DIM)
        return x * lax.rsqrt(ms + NORM_EPS) * g_ref[...]

    for a in range(ATTN_HEADS):
        qo_ref[:, a * V7X_LANES:(a + 1) * V7X_LANES] = norm(q_ref, gq_ref, a).astype(qo_ref.dtype)
        kb = norm(k_ref, gk_ref, a).astype(ko_ref.dtype)
        ko_ref[a, 0] = kb
        kk = kb.astype(F32) * kb.astype(F32)
        n_lo = jnp.max(jnp.sum(jnp.where(lo_mask, kk, 0.0), axis=-1, keepdims=True), axis=0, keepdims=True)
        n_hi = jnp.max(jnp.sum(jnp.where(lo_mask, 0.0, kk), axis=-1, keepdims=True), axis=0, keepdims=True)
        kn_ref[0, a:a + 1, :] = jnp.where(lane == 0, n_lo, jnp.where(lane == 1, n_hi, 0.0))
        v = v_ref[:, a * V7X_LANES:(a + 1) * V7X_LANES].astype(F32)
        vo_ref[a, 0] = v.T.astype(vo_ref.dtype)


def attn_prep(proj, q_gain, k_gain, *, tk):
    s = proj.shape[0]
    nk = s // tk
    gq = (jnp.tile(q_gain.astype(F32), 2) * (QK_DIM ** -0.5 * LOG2E)).reshape(1, V7X_LANES)
    gk = jnp.tile(k_gain.astype(F32), 2).reshape(1, V7X_LANES)
    cq, ck, cv = COL_Q // QK_COLS, COL_K // QK_COLS, COL_V // ATTN_WIDTH
    qn, k4, vt4, kn2 = pl.pallas_call(
        _attn_prep_kernel,
        out_shape=(jax.ShapeDtypeStruct((s, QK_COLS), BF16),
                   jax.ShapeDtypeStruct((ATTN_HEADS, nk, tk, V_DIM), BF16),
                   jax.ShapeDtypeStruct((ATTN_HEADS, nk, V_DIM, tk), BF16),
                   jax.ShapeDtypeStruct((nk, ATTN_HEADS, V7X_LANES), F32)),
        grid=(nk,),
        in_specs=[pl.BlockSpec((tk, QK_COLS), lambda i: (i, cq)),
                  pl.BlockSpec((tk, QK_COLS), lambda i: (i, ck)),
                  pl.BlockSpec((tk, ATTN_WIDTH), lambda i: (i, cv)),
                  pl.BlockSpec((1, V7X_LANES), lambda i: (0, 0)),
                  pl.BlockSpec((1, V7X_LANES), lambda i: (0, 0))],
        out_specs=(pl.BlockSpec((tk, QK_COLS), lambda i: (i, 0)),
                   pl.BlockSpec((ATTN_HEADS, 1, tk, V_DIM), lambda i: (0, i, 0, 0)),
                   pl.BlockSpec((ATTN_HEADS, 1, V_DIM, tk), lambda i: (0, i, 0, 0)),
                   pl.BlockSpec((1, ATTN_HEADS, V7X_LANES), lambda i: (i, 0, 0))),
        compiler_params=_cp("parallel"),
        name="attn_prep",
    )(proj, proj, proj, gq, gk)
    kmax = jnp.sqrt(jnp.max(kn2[:, :, :2], axis=0))
    return qn, k4, vt4, kmax


ATTN_NEAR = 3
ATTN_BIAS_TILES = 5
ATTN_UNROLL = 6


def _bias_tile_kernel(rb_ref, o_ref, *, tk, tq):
    h = pl.program_id(0)
    d = pl.program_id(1)
    w = tk + tq
    m = lax.broadcasted_iota(I32, (F32_SUBLANES, w), 1)
    rel = (d - ATTN_BIAS_TILES // 2) * tk + (tk - 1) - m
    half = REL_BUCKETS // 2
    exact = half // 2
    side = jnp.where(rel > 0, half, 0).astype(I32)
    n = jnp.abs(rel)
    nf = jnp.maximum(n, 1).astype(F32)
    large = exact + (jnp.log(nf / exact) / math.log(REL_MAX_DIST / exact) * (half - exact)).astype(I32)
    large = jnp.minimum(large, half - 1)
    bucket = side + jnp.where(n < exact, n, large).astype(I32)
    val = jnp.zeros((F32_SUBLANES, w), F32)
    for b in range(REL_BUCKETS):
        val = jnp.where(bucket == b, rb_ref[b, h], val)
    table = jnp.broadcast_to(val[0:1, :] * LOG2E, (tk, w))
    o_ref[0, 0] = pltpu.roll(table, w - tk + 1, 1, stride=1, stride_axis=0)[:, :tq]


def bias_tiles(rel_bias, *, tk, tq):
    assert tk == tq and (tk + tq) & (tk + tq - 1) == 0
    return pl.pallas_call(
        functools.partial(_bias_tile_kernel, tk=tk, tq=tq),
        out_shape=jax.ShapeDtypeStruct((ATTN_HEADS, ATTN_BIAS_TILES, tk, tq), F32),
        grid=(ATTN_HEADS, ATTN_BIAS_TILES),
        in_specs=[pl.BlockSpec(memory_space=pltpu.SMEM)],
        out_specs=pl.BlockSpec((1, 1, tk, tq), lambda h, d: (h, d, 0, 0)),
        compiler_params=_cp("parallel", "parallel"),
        name="t5_bias_tiles",
    )(rel_bias.astype(F32))


FAST_MIN_SUM = 2.0 ** -90


def _attn_kernel(lam_ref, far_ref, kmax_ref, q_ref, k_ref, vt_ref, bias_ref, g_ref, o_ref,
                 s_ref, cm_ref, m_ref, l_ref, acc_ref, p_ref, *, nk, tk, tq, out_scale):
    h = pl.program_id(0)
    i = pl.program_id(1)
    lane = lax.broadcasted_iota(I32, (1, V7X_LANES), 1)
    q = q_ref[...]
    zero = jnp.zeros_like(q)
    qmaps = (jnp.where(lane < QK_DIM, q, zero), jnp.where(lane < QK_DIM, zero, q))

    n0 = jnp.clip(i - 1, 0, nk - ATTN_NEAR)
    nfar = nk - ATTN_NEAR

    def far_tile(t):
        return jnp.where(t < n0, t, t + ATTN_NEAR)

    def far_const(j):
        return jnp.where(j < i, far_ref[h, 0], far_ref[h, 1])

    def near_bias(j):
        return bias_ref[0, j - i + ATTN_BIAS_TILES // 2]

    qf = q.astype(F32)
    row8 = lax.broadcasted_iota(I32, (F32_SUBLANES, V7X_LANES), 0)
    lane8 = lax.broadcasted_iota(I32, (F32_SUBLANES, V7X_LANES), 1)
    pick = jnp.where(jnp.logical_or(jnp.logical_and(row8 == 0, lane8 < QK_DIM),
                                    jnp.logical_and(row8 == 1, lane8 >= QK_DIM)), 1.0, 0.0)
    qn2 = lax.dot_general(pick, qf * qf, (((1,), (1,)), ((), ())), preferred_element_type=F32,
                          precision=lax.Precision.HIGHEST)
    shift = [jnp.sqrt(qn2[c:c + 1, :]) * kmax_ref[h, c] + far_ref[h, 2] for c in range(2)]
    l_ref[...] = jnp.zeros(l_ref.shape, F32)
    acc_ref[...] = jnp.zeros(acc_ref.shape, F32)

    def fast_probs(j, slot, bias_tile, bias_const):
        kt = k_ref[0, j]
        for c in range(2):
            s = lax.dot_general(kt, qmaps[c], (((1,), (1,)), ((), ())), preferred_element_type=F32)
            if bias_tile is not None:
                s = s + bias_tile
            p = jnp.exp2(s - (shift[c] - bias_const))
            l_ref[c:c + 1, :] += jnp.sum(p, axis=0, keepdims=True)
            p_ref[slot, c] = p.astype(BF16)

    def fast_pv(j, slot):
        vt = vt_ref[0, j]
        for c in range(2):
            acc_ref[c] += jnp.dot(vt, p_ref[slot, c], preferred_element_type=F32)

    fast_probs(n0, 0, near_bias(n0), 0.0)
    for w in range(1, ATTN_NEAR):
        fast_probs(n0 + w, w % 2, near_bias(n0 + w), 0.0)
        fast_pv(n0 + w - 1, (w - 1) % 2)
    j0 = far_tile(0)
    fast_probs(j0, ATTN_NEAR % 2, None, far_const(j0))
    fast_pv(n0 + ATTN_NEAR - 1, (ATTN_NEAR - 1) % 2)

    def fast_group(u, jprev):
        for w in range(ATTN_UNROLL):
            jn = far_tile(ATTN_UNROLL * u + w + 1)
            fast_probs(jn, (ATTN_NEAR + 1 + w) % 2, None, far_const(jn))
            fast_pv(jprev, (ATTN_NEAR + w) % 2)
            jprev = jn
        return jprev

    jl = lax.fori_loop(0, (nfar - 1) // ATTN_UNROLL, fast_group, j0)
    fast_pv(jl, (ATTN_NEAR + nfar - 1) % 2)

    lmin = jnp.min(jnp.minimum(l_ref[0:1, :], l_ref[1:2, :]))

    @pl.when(jnp.logical_not(lmin >= FAST_MIN_SUM))
    def _():
        _attn_running_max(h, i, n0, nfar, far_tile, far_const, near_bias, qmaps, k_ref, vt_ref,
                          s_ref, cm_ref, m_ref, l_ref, acc_ref, tq=tq)

    o1 = acc_ref[0] / l_ref[0:1, :]
    o2 = acc_ref[1] / l_ref[1:2, :]
    o = o1 - lam_ref[0] * o2
    r = lax.rsqrt(jnp.mean(o * o, axis=0, keepdims=True) + NORM_EPS)
    o = o * r * g_ref[...] * out_scale
    o_ref[...] = o.T.astype(o_ref.dtype)


def _attn_running_max(h, i, n0, nfar, far_tile, far_const, near_bias, qmaps, k_ref, vt_ref,
                      s_ref, cm_ref, m_ref, l_ref, acc_ref, *, tq):
    m_ref[...] = jnp.full(m_ref.shape, NEG_BIG, F32)
    l_ref[...] = jnp.zeros(l_ref.shape, F32)
    acc_ref[...] = jnp.zeros(acc_ref.shape, F32)

    def scores(j, slot, bias_tile, bias_const):
        kt = k_ref[0, j]
        for c in range(2):
            s = lax.dot_general(kt, qmaps[c], (((1,), (1,)), ((), ())),
                                preferred_element_type=F32)
            if bias_tile is not None:
                s = s + bias_tile
            s_ref[slot, c] = s
            cm_ref[slot, c:c + 1, :] = jnp.max(s, axis=0, keepdims=True) + bias_const
        cm_ref[slot, 2:3, :] = jnp.zeros((1, tq), F32) + bias_const

    def absorb(j, slot):
        vt = vt_ref[0, j]
        cb = cm_ref[slot, 2:3, :]
        for c in range(2):
            m_old = m_ref[c:c + 1, :]
            m_new = jnp.maximum(m_old, cm_ref[slot, c:c + 1, :])
            alpha = jnp.exp2(m_old - m_new)
            p = jnp.exp2(s_ref[slot, c] - (m_new - cb))
            l_ref[c:c + 1, :] = alpha * l_ref[c:c + 1, :] + jnp.sum(p, axis=0, keepdims=True)
            acc_ref[c] = alpha * acc_ref[c] + jnp.dot(vt, p.astype(BF16), preferred_element_type=F32)
            m_ref[c:c + 1, :] = m_new

    scores(n0, 0, near_bias(n0), 0.0)
    scores(n0 + 1, 1, near_bias(n0 + 1), 0.0)
    absorb(n0, 0)
    scores(n0 + 2, 0, near_bias(n0 + 2), 0.0)
    absorb(n0 + 1, 1)
    j0 = far_tile(0)
    scores(j0, 1, None, far_const(j0))
    absorb(n0 + 2, 0)

    def group(u, jprev):
        for w in range(ATTN_UNROLL):
            jn = far_tile(ATTN_UNROLL * u + w + 1)
            scores(jn, w % 2, None, far_const(jn))
            absorb(jprev, (w + 1) % 2)
            jprev = jn
        return jprev

    jlast = lax.fori_loop(0, (nfar - 1) // ATTN_UNROLL, group, j0)
    absorb(jlast, 1)


def attn_bias_consts(rel_bias):
    half = REL_BUCKETS // 2
    rb = rel_bias.astype(F32) * LOG2E
    return jnp.stack([rb[half - 1], rb[REL_BUCKETS - 1], jnp.max(rb, axis=0)], axis=1)


def diff_attention(qn, k4, vt4, bias, far_bias, kmax, lam, subln_gain, *, lam_init, tq, tk):
    s = qn.shape[0]
    nk = s // tk
    assert tq == tk and nk > ATTN_NEAR and (nk - ATTN_NEAR - 1) % ATTN_UNROLL == 0
    kern = functools.partial(_attn_kernel, nk=nk, tk=tk, tq=tq, out_scale=1.0 - lam_init)
    return pl.pallas_call(
        kern,
        out_shape=jax.ShapeDtypeStruct((s, ATTN_WIDTH), BF16),
        grid=(ATTN_HEADS, s // tq),
        in_specs=[pl.BlockSpec(memory_space=pltpu.SMEM),
                  pl.BlockSpec(memory_space=pltpu.SMEM),
                  pl.BlockSpec(memory_space=pltpu.SMEM),
                  pl.BlockSpec((tq, V_DIM), lambda h, i: (i, h)),
                  pl.BlockSpec((1, nk, tk, V_DIM), lambda h, i: (h, 0, 0, 0)),
                  pl.BlockSpec((1, nk, V_DIM, tk), lambda h, i: (h, 0, 0, 0)),
                  pl.BlockSpec((1, ATTN_BIAS_TILES, tk, tq), lambda h, i: (h, 0, 0, 0)),
                  pl.BlockSpec((V_DIM, 1), lambda h, i: (0, 0))],
        out_specs=pl.BlockSpec((tq, V_DIM), lambda h, i: (i, h)),
        scratch_shapes=[pltpu.VMEM((2, 2, tk, tq), F32),
                        pltpu.VMEM((2, F32_SUBLANES, tq), F32),
                        pltpu.VMEM((F32_SUBLANES, tq), F32),
                        pltpu.VMEM((F32_SUBLANES, tq), F32),
                        pltpu.VMEM((2, V_DIM, tq), F32),
                        pltpu.VMEM((2, 2, tk, tq), BF16)],
        compiler_params=_cp("parallel", "arbitrary"),
        name="diff_attention",
    )(lam.reshape(1).astype(F32), far_bias, kmax, qn, k4, vt4, bias,
      subln_gain.reshape(V_DIM, 1).astype(F32))


def s5_operators(a_re, a_im, log_dt, b_re, b_im, c_re, c_im):
    t_len, hp = S5_CHUNK, lax.Precision.HIGHEST
    a_re, a_im = a_re.astype(F32), a_im.astype(F32)
    dt = jnp.exp(log_dt.astype(F32))[..., None]
    steps = jnp.arange(t_len + 1, dtype=F32)[:, None, None, None]
    mag = jnp.exp(a_re * dt * steps)
    ang = a_im * dt * steps
    pw_re, pw_im = mag * jnp.cos(ang), mag * jnp.sin(ang)
    den = a_re * a_re + a_im * a_im
    nr, ni = pw_re[1] - 1.0, pw_im[1]
    coef_re = ((nr * a_re + ni * a_im) / den)[..., None]
    coef_im = ((ni * a_re - nr * a_im) / den)[..., None]
    b_re, b_im = b_re.astype(F32), b_im.astype(F32)
    bb_re = coef_re * b_re - coef_im * b_im
    bb_im = coef_re * b_im + coef_im * b_re
    c_re, c_im = c_re.astype(F32), c_im.astype(F32)

    g, n_st, p_ch = a_re.shape[1], SSM_STATE, SSM_GROUP
    tp = t_len * p_ch
    pwt_re = jnp.transpose(pw_re, (1, 2, 3, 0))
    pwt_im = jnp.transpose(pw_im, (1, 2, 3, 0))
    ct_re = jnp.transpose(c_re, (0, 1, 3, 2))
    ct_im = jnp.transpose(c_im, (0, 1, 3, 2))
    bbt_re = jnp.transpose(bb_re, (0, 1, 3, 2))
    bbt_im = jnp.transpose(bb_im, (0, 1, 3, 2))
    cp_re = ct_re[:, :, :, None, :] * pwt_re[..., None] - ct_im[:, :, :, None, :] * pwt_im[..., None]
    cp_im = ct_re[:, :, :, None, :] * pwt_im[..., None] + ct_im[:, :, :, None, :] * pwt_re[..., None]

    zlag = jnp.zeros((g, n_st, t_len - 1, p_ch), F32)

    def lagged(cp):
        return (jnp.concatenate([zlag, cp[0, :, :, :t_len]], axis=2),
                jnp.concatenate([jnp.flip(cp[1, :, :, :t_len], axis=2), zlag], axis=2))

    rf_re, rb_re = lagged(cp_re)
    rf_im, rb_im = lagged(cp_im)
    r_cat = jnp.concatenate([rf_re, rf_im, rb_re, rb_im], axis=1).reshape(g, 4 * n_st, (2 * t_len - 1) * p_ch)
    a_cat = jnp.concatenate([bbt_re[0], -bbt_im[0], bbt_re[1], -bbt_im[1]], axis=-1)
    kp = jnp.einsum('gpk,gkx->gpx', a_cat, r_cat, precision=hp)
    toep = jnp.stack([kp[:, :, (t_len - 1 - j) * p_ch:(t_len - 1 - j) * p_ch + tp] for j in range(t_len)],
                     axis=1).reshape(g, tp, tp).astype(BF16)

    def seg_powers(d, reverse):
        pr = jnp.transpose(pw_re[:t_len, d], (1, 0, 2))
        pi = jnp.transpose(pw_im[:t_len, d], (1, 0, 2))
        if reverse:
            pr, pi = jnp.flip(pr, axis=1), jnp.flip(pi, axis=1)
        return pr, pi

    prf, pif = seg_powers(0, True)
    prb, pib = seg_powers(1, False)
    pa = jnp.concatenate([prf, pif, pif, prf, prb, pib, pib, prb], axis=-1)
    pb = jnp.concatenate([-pif, prf, prf, -pif, -pib, prb, prb, -pib], axis=-1)
    br = jnp.concatenate([bbt_re[0]] * 4 + [bbt_re[1]] * 4, axis=-1)
    bi = jnp.concatenate([bbt_im[0]] * 4 + [bbt_im[1]] * 4, axis=-1)
    smap = (br[:, None] * pa[:, :, None, :] + bi[:, None] * pb[:, :, None, :]).reshape(g, tp, 8 * n_st)
    smap = smap.astype(BF16)

    def out_map(d, reverse):
        wr, wi = cp_re[d][:, :, 1:t_len + 1], cp_im[d][:, :, 1:t_len + 1]
        if reverse:
            wr, wi = jnp.flip(wr, axis=2), jnp.flip(wi, axis=2)
        return [wr.reshape(g, n_st, tp), -wi.reshape(g, n_st, tp)]

    mc = jnp.concatenate(out_map(0, False) + out_map(1, True), axis=1).astype(BF16)

    def carry(d):
        ar, ai = pw_re[t_len, d], pw_im[t_len, d]
        return [jnp.concatenate([ar, ar], -1), jnp.concatenate([-ai, ai], -1),
                jnp.concatenate([ai, -ai], -1)]

    coef = jnp.stack(carry(0) + carry(1), axis=0)
    return toep, smap, mc, coef


def _gelu_tanh(x):
    return 0.5 * x * (1.0 + jnp.tanh(math.sqrt(2.0 / math.pi) * (x + 0.044715 * (x * x * x))))


def _s5_kernel(x_ref, sel_ref, toep_ref, smap_ref, mc_ref, coef_ref, d_ref, y_ref,
               u_ref, ef_ref, efs_ref, eb_ref, ebs_ref, *, cn):
    gb = toep_ref.shape[0]
    tp = toep_ref.shape[1]
    n2 = 2 * SSM_STATE
    xcat = x_ref[0]

    def sel(g):
        off = (gb - 1 - g) * SSM_GROUP
        return sel_ref[off:off + xcat.shape[1], :]

    for g in range(gb):
        u_ref[g] = jnp.dot(xcat, sel(g), preferred_element_type=F32).astype(BF16)
        e = jnp.dot(u_ref[g], smap_ref[g], preferred_element_type=F32)
        for r, ref in enumerate((ef_ref, efs_ref, eb_ref, ebs_ref)):
            ref[pl.ds(g, cn, stride=gb), :] = e[:, r * n2:(r + 1) * n2]

    cf, cfs, cfw = coef_ref[0], coef_ref[1], coef_ref[2]
    cb, cbs, cbw = coef_ref[3], coef_ref[4], coef_ref[5]

    def step(c, carry):
        s, sw, r, rw = carry
        fo = pl.multiple_of(c * gb, gb)
        bo = pl.multiple_of((cn - 1 - c) * gb, gb)
        e, es = ef_ref[pl.ds(fo, gb), :], efs_ref[pl.ds(fo, gb), :]
        ef_ref[pl.ds(fo, gb), :] = s
        s, sw = cf * s + cfs * sw + e, cf * sw + cfw * s + es
        e, es = eb_ref[pl.ds(bo, gb), :], ebs_ref[pl.ds(bo, gb), :]
        eb_ref[pl.ds(bo, gb), :] = r
        r, rw = cb * r + cbs * rw + e, cb * rw + cbw * r + es
        return s, sw, r, rw

    z = jnp.zeros((gb, n2), F32)
    lax.fori_loop(0, cn, step, (z, z, z, z))

    ycat = None
    for g in range(gb):
        u = u_ref[g]
        st = jnp.concatenate([ef_ref[pl.ds(g, cn, stride=gb), :], eb_ref[pl.ds(g, cn, stride=gb), :]],
                             axis=1).astype(BF16)
        y = (jnp.dot(u, toep_ref[g], preferred_element_type=F32)
             + jnp.dot(st, mc_ref[g], preferred_element_type=F32)
             + u.astype(F32) * d_ref[g])
        placed = lax.dot_general(_gelu_tanh(y).astype(BF16), sel(g), (((1,), (1,)), ((), ())),
                                 preferred_element_type=F32)
        ycat = placed if ycat is None else ycat + placed
    y_ref[0] = ycat.astype(y_ref.dtype)


def s5_lane_selector():
    gb, p, t_len = S5_GB, SSM_GROUP, S5_CHUNK
    r = jnp.arange(t_len * V7X_LANES + (gb - 1) * p)[:, None] - (gb - 1) * p
    c = jnp.arange(t_len * p)[None, :]
    hit = (r >= 0) & (r // V7X_LANES == c // p) & ((r % V7X_LANES) // p == 0) & (r % p == c % p)
    return hit.astype(BF16)


def s5_mixer(xcat, ops, ssm_d):
    toep, smap, mc, coef = ops
    g, p, t_len = SSM_GROUPS, SSM_GROUP, S5_CHUNK
    nslab, cn, _ = xcat.shape
    tp = t_len * p
    gb = S5_GB
    assert gb * p == V7X_LANES and nslab == g // gb
    sel = s5_lane_selector()
    dsk = jnp.tile(ssm_d.astype(F32).reshape(g, 1, p), (1, 1, t_len))
    ycat = pl.pallas_call(
        functools.partial(_s5_kernel, cn=cn),
        out_shape=jax.ShapeDtypeStruct((nslab, cn, t_len * V7X_LANES), BF16),
        grid=(nslab,),
        in_specs=[pl.BlockSpec((1, cn, t_len * V7X_LANES), lambda i: (i, 0, 0)),
                  pl.BlockSpec(sel.shape, lambda i: (0, 0)),
                  pl.BlockSpec((gb, tp, tp), lambda i: (i, 0, 0)),
                  pl.BlockSpec((gb, tp, smap.shape[2]), lambda i: (i, 0, 0)),
                  pl.BlockSpec((gb, mc.shape[1], tp), lambda i: (i, 0, 0)),
                  pl.BlockSpec((6, gb, 2 * SSM_STATE), lambda i: (0, i, 0)),
                  pl.BlockSpec((gb, 1, tp), lambda i: (i, 0, 0))],
        out_specs=pl.BlockSpec((1, cn, t_len * V7X_LANES), lambda i: (i, 0, 0)),
        scratch_shapes=[pltpu.VMEM((gb, cn, tp), BF16)]
        + [pltpu.VMEM((cn * gb, 2 * SSM_STATE), F32) for _ in range(4)],
        compiler_params=_cp("parallel"),
        name="s5_chunked_scan",
    )(xcat, sel, toep, smap, mc, coef, dsk)
    return ycat


def _sigmoid(x):
    return 1.0 / (1.0 + jnp.exp(-x))


def _glu_kernel(ycat_ref, w_ref, o_ref, y_ref):
    cn = ycat_ref.shape[1]
    for slab in range(ycat_ref.shape[0]):
        for t in range(S5_CHUNK):
            piece = ycat_ref[slab, :, t * V7X_LANES:(t + 1) * V7X_LANES].astype(F32)
            y_ref[slab, pl.ds(t, cn, stride=S5_CHUNK), :] = piece
    y = jnp.concatenate([y_ref[slab] for slab in range(ycat_ref.shape[0])], axis=1)
    z = jnp.dot(y.astype(BF16), w_ref[0].astype(BF16), preferred_element_type=F32)
    o_ref[...] = (y * _sigmoid(z)).astype(o_ref.dtype)


def glu(ycat, w, layer, *, tm):
    nslab, cn, _ = ycat.shape
    s, d = cn * S5_CHUNK, nslab * V7X_LANES
    return pl.pallas_call(
        _glu_kernel,
        out_shape=jax.ShapeDtypeStruct((s, d), BF16),
        grid=(s // tm,),
        in_specs=[pl.BlockSpec((nslab, tm // S5_CHUNK, S5_CHUNK * V7X_LANES), lambda i: (0, i, 0)),
                  pl.BlockSpec((1, d, d), lambda i: (layer, 0, 0))],
        out_specs=pl.BlockSpec((tm, d), lambda i: (i, 0)),
        scratch_shapes=[pltpu.VMEM((nslab, tm, V7X_LANES), F32)],
        compiler_params=_cp("parallel"),
        name="half_glu",
    )(ycat, w)


def _merge_kernel(yg_ref, ao_ref, gs_ref, ga_ref, ws_ref, wa_ref, o_ref):
    a = jnp.dot(yg_ref[...], ws_ref[0].astype(BF16), preferred_element_type=F32)
    b = jnp.dot(ao_ref[...], wa_ref[0].astype(BF16), preferred_element_type=F32)
    o = _sigmoid(gs_ref[...].astype(F32)) * a + _sigmoid(ga_ref[...].astype(F32)) * b
    o_ref[...] = o.astype(o_ref.dtype)


def gated_merge(yg, ao, proj, ws, wa, layer, *, tm, tn):
    s, k = yg.shape
    n = ws.shape[2]
    cs, ca = COL_GS // tn, COL_GA // tn
    return pl.pallas_call(
        _merge_kernel,
        out_shape=jax.ShapeDtypeStruct((s, n), BF16),
        grid=(s // tm, n // tn),
        in_specs=[pl.BlockSpec((tm, k), lambda i, j: (i, 0)),
                  pl.BlockSpec((tm, k), lambda i, j: (i, 0)),
                  pl.BlockSpec((tm, tn), lambda i, j: (i, cs + j)),
                  pl.BlockSpec((tm, tn), lambda i, j: (i, ca + j)),
                  pl.BlockSpec((1, k, tn), lambda i, j: (layer, 0, j)),
                  pl.BlockSpec((1, k, tn), lambda i, j: (layer, 0, j))],
        out_specs=pl.BlockSpec((tm, tn), lambda i, j: (i, j)),
        compiler_params=_cp("parallel", "arbitrary"),
        name="gated_merge",
    )(yg, ao, proj, proj, ws, wa)


def _out_router_kernel(m_ref, x_ref, w_ref, g_ref, wrt_ref, xo_ref, h_ref, afft_ref):
    x1 = x_ref[...] + jnp.dot(m_ref[...], w_ref[...], preferred_element_type=F32)
    xo_ref[...] = x1
    r = lax.rsqrt(jnp.mean(x1 * x1, axis=-1, keepdims=True) + NORM_EPS)
    h = x1 * r * g_ref[...]
    h_ref[...] = h
    lgt = lax.dot_general(wrt_ref[...], h, (((1,), (1,)), ((), ())),
                          preferred_element_type=F32, precision=lax.Precision.HIGHEST)
    et = jnp.exp(lgt - jnp.max(lgt, axis=0, keepdims=True))
    afft_ref[...] = et / jnp.sum(et, axis=0, keepdims=True)


def out_proj_router(merged, x, w_out, gain, w_router, *, tm):
    s, d = x.shape
    e = w_router.shape[1]
    return pl.pallas_call(
        _out_router_kernel,
        out_shape=(jax.ShapeDtypeStruct((s, d), F32), jax.ShapeDtypeStruct((s, d), F32),
                   jax.ShapeDtypeStruct((e, s), F32)),
        grid=(s // tm,),
        in_specs=[pl.BlockSpec((tm, d), lambda i: (i, 0)),
                  pl.BlockSpec((tm, d), lambda i: (i, 0)),
                  pl.BlockSpec((d, d), lambda i: (0, 0)),
                  pl.BlockSpec((1, d), lambda i: (0, 0)),
                  pl.BlockSpec((e, d), lambda i: (0, 0))],
        out_specs=(pl.BlockSpec((tm, d), lambda i: (i, 0)),
                   pl.BlockSpec((tm, d), lambda i: (i, 0)),
                   pl.BlockSpec((e, tm), lambda i: (0, i))),
        compiler_params=_cp("parallel"),
        name="out_proj_router",
    )(merged, x, w_out, gain.reshape(1, d).astype(F32), w_router.astype(F32).T)


SLOT_LO_BITS = 6
SLOT_LO = 1 << SLOT_LO_BITS


def _select_kernel(afft_ref, slot_ref, cum_ref, idx_ref, acc_ref, *, s, cap, blk):
    ne = afft_ref.shape[0]

    def bit_body(b, thr):
        cand = thr | jnp.left_shift(jnp.ones((ne, 1), I32), 30 - b)
        keys = pltpu.bitcast(afft_ref[...], I32)
        cnt = jnp.sum((keys >= cand).astype(I32), axis=1, keepdims=True)
        return jnp.where(cnt >= cap, cand, thr)

    thr = lax.fori_loop(0, 31, bit_body, jnp.zeros((ne, 1), I32))
    keys = pltpu.bitcast(afft_ref[...], I32)
    need = cap - jnp.sum((keys > thr).astype(I32), axis=1, keepdims=True)

    ri = lax.broadcasted_iota(I32, (blk, blk), 0)
    ci = lax.broadcasted_iota(I32, (blk, blk), 1)
    upper = jnp.where(ri < ci, 1.0, 0.0).astype(BF16)
    na = cap // SLOT_LO
    acol = lax.broadcasted_iota(I32, (na, 1), 0)
    bcol = lax.broadcasted_iota(I32, (SLOT_LO, 1), 0)
    tlane = lax.broadcasted_iota(I32, (1, blk), 1)
    acc_ref[...] = jnp.zeros(acc_ref.shape, F32)

    def blk_body(b, carry):
        ceq, csel = carry
        off = pl.multiple_of(b * blk, blk)
        kb = pltpu.bitcast(afft_ref[:, pl.ds(off, blk)], I32)
        gt = kb > thr
        eq = kb == thr
        eqf = jnp.where(eq, 1.0, 0.0)
        rank_eq = jnp.dot(eqf.astype(BF16), upper, preferred_element_type=F32) + ceq
        sel = jnp.logical_or(gt, jnp.logical_and(eq, rank_eq < need.astype(F32)))
        self_ = jnp.where(sel, 1.0, 0.0)
        cum = jnp.dot(self_.astype(BF16), upper, preferred_element_type=F32) + csel
        cum_i = cum.astype(I32)
        cum_ref[:, pl.ds(off, blk)] = cum_i
        slot = jnp.where(sel, cum_i, -1)
        slot_ref[:, pl.ds(off, blk)] = slot
        tok = off + tlane
        hi = (tok // V7X_LANES).astype(F32)
        lo = (tok % V7X_LANES).astype(F32)
        for e in range(ne):
            srow = slot[e:e + 1, :]
            in_a = lax.shift_right_arithmetic(srow, SLOT_LO_BITS) == acol
            lhs = jnp.concatenate([jnp.where(in_a, hi, 0.0), jnp.where(in_a, lo, 0.0)],
                                  axis=0).astype(BF16)
            rhs = jnp.where((srow & (SLOT_LO - 1)) == bcol, 1.0, 0.0).astype(BF16)
            acc_ref[e] += lax.dot_general(lhs, rhs, (((1,), (1,)), ((), ())), preferred_element_type=F32)
        return (ceq + jnp.sum(eqf, axis=1, keepdims=True), csel + jnp.sum(self_, axis=1, keepdims=True))

    z = jnp.zeros((ne, 1), F32)
    lax.fori_loop(0, s // blk, blk_body, (z, z))
    a = acc_ref[...]
    idx_ref[...] = (a[:, :na, :] * float(V7X_LANES) + a[:, na:, :]).astype(I32)


def expert_select(afft, *, cap, blk):
    ne, s = afft.shape
    assert cap % SLOT_LO == 0
    return pl.pallas_call(
        functools.partial(_select_kernel, s=s, cap=cap, blk=blk),
        out_shape=(jax.ShapeDtypeStruct((ne, s), I32), jax.ShapeDtypeStruct((ne, s), I32),
                   jax.ShapeDtypeStruct((ne, cap // SLOT_LO, SLOT_LO), I32)),
        scratch_shapes=[pltpu.VMEM((ne, 2 * (cap // SLOT_LO), SLOT_LO), F32)],
        compiler_params=pltpu.CompilerParams(vmem_limit_bytes=VMEM_LIMIT),
        name="expert_select",
    )(afft)


GATHER_UNROLL = 8


def _ffn_kernel(idx_ref, h_hbm, wg_ref, wu_ref, wd_ref, y_ref, xg32_ref, xg_ref, sem, *, cap):
    e = pl.program_id(0)
    f = pl.program_id(1)

    def row_copy(r):
        tok = idx_ref[e * cap + r]
        return pltpu.make_async_copy(h_hbm.at[pl.ds(tok, 1)], xg32_ref.at[pl.ds(r, 1)], sem)

    @pl.when(f == 0)
    def _():
        def start(rb, c):
            for w in range(GATHER_UNROLL):
                row_copy(rb * GATHER_UNROLL + w).start()
            return c

        lax.fori_loop(0, cap // GATHER_UNROLL, start, 0)
        y_ref[...] = jnp.zeros(y_ref.shape, F32)
        pltpu.make_async_copy(h_hbm.at[pl.ds(0, cap)], xg32_ref, sem).wait()
        xg_ref[...] = xg32_ref[...].astype(BF16)

    xg = xg_ref[...]
    a = jnp.dot(xg, wg_ref[0, 0].astype(BF16), preferred_element_type=F32)
    b = jnp.dot(xg, wu_ref[0, 0].astype(BF16), preferred_element_type=F32)
    hid = (a * _sigmoid(a) * b).astype(BF16)
    y_ref[0, 0:cap, :] += jnp.dot(hid, wd_ref[0, 0].astype(BF16), preferred_element_type=F32)


def expert_ffn(idx, h2, wg, wu, wd, layer, *, cap, pad, fc):
    _, ne, d, ff = wg.shape
    grid_spec = pltpu.PrefetchScalarGridSpec(
        num_scalar_prefetch=1,
        grid=(ne, ff // fc),
        in_specs=[pl.BlockSpec(memory_space=pl.ANY),
                  pl.BlockSpec((1, 1, d, fc), lambda e, f, idx: (layer, e, 0, f)),
                  pl.BlockSpec((1, 1, d, fc), lambda e, f, idx: (layer, e, 0, f)),
                  pl.BlockSpec((1, 1, fc, d), lambda e, f, idx: (layer, e, f, 0))],
        out_specs=pl.BlockSpec((1, cap + pad, d), lambda e, f, idx: (e, 0, 0)),
        scratch_shapes=[pltpu.VMEM((cap, d), F32), pltpu.VMEM((cap, d), BF16),
                        pltpu.SemaphoreType.DMA(())],
    )
    return pl.pallas_call(
        functools.partial(_ffn_kernel, cap=cap),
        out_shape=jax.ShapeDtypeStruct((ne, cap + pad, d), F32),
        grid_spec=grid_spec,
        compiler_params=_cp("arbitrary", "arbitrary"),
        name="expert_ffn",
    )(idx.reshape(-1), h2, wg, wu, wd)


def _combine_kernel(st_ref, x_ref, aff_ref, slot_ref, y_hbm, o_ref, buf_ref, xbuf_ref, sem, xsem,
                    *, ne, rows, nt):
    t = pl.program_id(0)
    tm = x_ref.shape[0]

    def chunk_start(tt, e, c):
        st8 = (st_ref[tt * ne + e] // F32_SUBLANES) * F32_SUBLANES
        return pl.multiple_of(st8 + c * rows, F32_SUBLANES)

    def first_copy(tt, e, slot):
        return pltpu.make_async_copy(y_hbm.at[e, pl.ds(chunk_start(tt, e, 0), rows)],
                                     buf_ref.at[slot, e], sem.at[slot, e])

    @pl.when(t == 0)
    def _():
        for e in range(ne):
            first_copy(0, e, 0).start()

    @pl.when(t + 1 < nt)
    def _():
        for e in range(ne):
            first_copy(t + 1, e, (t + 1) % 2).start()

    slot = t % 2
    for e in range(ne):
        first_copy(t, e, slot).wait()

    pair = V7X_LANES // rows
    lane = lax.broadcasted_iota(I32, (1, V7X_LANES), 1)
    hi_parts, lo_parts = [], []
    for a in range(ne // pair):
        rel = jnp.zeros((tm, V7X_LANES), I32)
        gate = jnp.zeros((tm, V7X_LANES), F32)
        for b in range(pair):
            e = a * pair + b
            in_e = jnp.logical_and(lane >= b * rows, lane < (b + 1) * rows)
            rel = jnp.where(in_e, slot_ref[:, e:e + 1] - chunk_start(t, e, 0) + b * rows, rel)
            gate = jnp.where(in_e, aff_ref[:, e:e + 1], gate)
        gate = jnp.where(rel == lane, gate, 0.0)
        g_hi = gate.astype(BF16)
        hi_parts.append(g_hi)
        lo_parts.append((gate - g_hi.astype(F32)).astype(BF16))
    rhs = buf_ref[slot].reshape(ne * rows, x_ref.shape[1]).astype(BF16)
    o_ref[...] = (x_ref[...]
                  + jnp.dot(jnp.concatenate(hi_parts, axis=1), rhs, preferred_element_type=F32)
                  + jnp.dot(jnp.concatenate(lo_parts, axis=1), rhs, preferred_element_type=F32))

    lane_r = lax.broadcasted_iota(I32, (1, rows), 1)
    for e in range(ne):
        end = st_ref[(t + 1) * ne + e]
        for c in range(1, tm // rows + 1):

            @pl.when(end > chunk_start(t, e, c))
            def _():
                cp = pltpu.make_async_copy(y_hbm.at[e, pl.ds(chunk_start(t, e, c), rows)], xbuf_ref, xsem)
                cp.start()
                cp.wait()
                rel = slot_ref[:, e:e + 1] - chunk_start(t, e, c)
                oh = jnp.where(rel == lane_r, 1.0, 0.0).astype(BF16)
                contrib = jnp.dot(oh, xbuf_ref[...].astype(BF16), preferred_element_type=F32)
                o_ref[...] += aff_ref[:, e:e + 1] * contrib


def moe_combine(starts, x1, aff, slot, ye, *, tm, rows):
    s, d = x1.shape
    ne = aff.shape[1]
    nt = s // tm
    assert V7X_LANES % rows == 0 and ne % (V7X_LANES // rows) == 0
    grid_spec = pltpu.PrefetchScalarGridSpec(
        num_scalar_prefetch=1,
        grid=(nt,),
        in_specs=[pl.BlockSpec((tm, d), lambda t, st: (t, 0)),
                  pl.BlockSpec((tm, ne), lambda t, st: (t, 0)),
                  pl.BlockSpec((tm, ne), lambda t, st: (t, 0)),
                  pl.BlockSpec(memory_space=pl.ANY)],
        out_specs=pl.BlockSpec((tm, d), lambda t, st: (t, 0)),
        scratch_shapes=[pltpu.VMEM((2, ne, rows, d), F32), pltpu.VMEM((rows, d), F32),
                        pltpu.SemaphoreType.DMA((2, ne)), pltpu.SemaphoreType.DMA(())],
    )
    return pl.pallas_call(
        functools.partial(_combine_kernel, ne=ne, rows=rows, nt=nt),
        out_shape=jax.ShapeDtypeStruct((s, d), F32),
        grid_spec=grid_spec,
        compiler_params=_cp("arbitrary"),
        name="moe_combine",
    )(starts.reshape(-1), x1, aff, slot, ye)


TM_PROJ, TN_PROJ = 1024, 1024
TM_ROW = 256
TM_OUT = 512
ATTN_TILE = 512
SEL_BLK = 256
FFN_CHUNK = 256
COMBINE_ROWS = 64


def _layer(x, l, p, bias, far_bias):
    s = x.shape[0]
    lam_init = 0.8 - 0.6 * math.exp(-0.3 * l)
    proj, xcat = norm_matmul(x, p["norm_mix"], p["w_in"].astype(BF16), tm=min(TM_PROJ, s), tn=TN_PROJ)

    ops = s5_operators(p["ssm_a_re"], p["ssm_a_im"], p["ssm_log_dt"], p["ssm_b_re"], p["ssm_b_im"],
                       p["ssm_c_re"], p["ssm_c_im"])
    ycat = s5_mixer(xcat, ops, p["ssm_d"])
    yg = glu(ycat, p["w_glu"], l, tm=min(TM_PROJ, s))

    qn, k4, vt4, kmax = attn_prep(proj, p["q_gain"], p["k_gain"], tk=ATTN_TILE)
    lam = (jnp.exp(jnp.sum(p["lambda_q1"].astype(F32) * p["lambda_k1"].astype(F32)))
           - jnp.exp(jnp.sum(p["lambda_q2"].astype(F32) * p["lambda_k2"].astype(F32))) + lam_init)
    ao = diff_attention(qn, k4, vt4, bias, far_bias, kmax, lam, p["subln_gain"], lam_init=lam_init,
                        tq=ATTN_TILE, tk=ATTN_TILE)

    merged = gated_merge(yg, ao, proj, p["w_ssm_branch"], p["w_attn_branch"], l,
                         tm=min(TM_PROJ, s), tn=TN_PROJ)
    x1, h2, afft = out_proj_router(merged, x, p["w_out"].astype(BF16), p["norm_ffn"], p["w_router"],
                                   tm=min(TM_OUT, s))

    cap = CAPACITY_FACTOR * s // N_EXPERTS
    slot_t, cum_t, idx = expert_select(afft, cap=cap, blk=SEL_BLK)
    ye = expert_ffn(idx, h2, p["w_expert_gate"], p["w_expert_up"], p["w_expert_down"], l,
                    cap=cap, pad=COMBINE_ROWS, fc=FFN_CHUNK)
    starts = jnp.concatenate([cum_t[:, ::TM_ROW].T, jnp.full((1, N_EXPERTS), cap, I32)], axis=0)
    return moe_combine(starts, x1, afft.T, slot_t.T, ye, tm=TM_ROW, rows=COMBINE_ROWS)


_LAYER_PARAMS = ("w_in", "ssm_a_re", "ssm_a_im", "ssm_log_dt", "ssm_b_re", "ssm_b_im", "ssm_c_re",
                 "ssm_c_im", "ssm_d", "q_gain", "k_gain", "lambda_q1",
                 "lambda_k1", "lambda_q2", "lambda_k2", "subln_gain", "w_out",
                 "norm_mix", "norm_ffn", "w_router")
_STACKED_PARAMS = ("w_expert_gate", "w_expert_up", "w_expert_down", "w_glu", "w_ssm_branch", "w_attn_branch")


def kernel(x, w_in, ssm_a_re, ssm_a_im, ssm_log_dt, ssm_b_re, ssm_b_im, ssm_c_re, ssm_c_im, ssm_d, w_glu, w_ssm_branch, q_gain, k_gain, lambda_q1, lambda_k1, lambda_q2, lambda_k2, subln_gain, w_attn_branch, rel_bias, w_out, norm_mix, norm_ffn, w_router, w_expert_gate, w_expert_up, w_expert_down):
    args = dict(locals())
    b = x.shape[0]
    bias = bias_tiles(rel_bias, tk=ATTN_TILE, tq=ATTN_TILE)
    far_bias = attn_bias_consts(rel_bias)
    outs = []
    for bi in range(b):
        xb = x[bi].astype(F32)
        for l in range(DEPTH):
            p = {k: args[k][l] for k in _LAYER_PARAMS}
            p.update({k: args[k] for k in _STACKED_PARAMS})
            xb = _layer(xb, l, p, bias, far_bias)
        outs.append(xb)
    return jnp.stack(outs, axis=0).astype(x.dtype)
```

```python
import functools
import math

import jax
import jax.numpy as jnp
import numpy as np
from jax import lax
from jax.experimental import pallas as pl
from jax.experimental.pallas import tpu as pltpu

F32 = jnp.float32
BF16 = jnp.bfloat16
I32 = jnp.int32

D_MODEL = 2048
DEPTH = 2
SSM_WIDTH = D_MODEL // 2
SSM_GROUP = 16
SSM_GROUPS = SSM_WIDTH // SSM_GROUP
SSM_STATE = 64
ATTN_HEADS = 8
QK_DIM = 64
V_DIM = 2 * QK_DIM
ATTN_WIDTH = ATTN_HEADS * V_DIM
QK_COLS = ATTN_HEADS * 2 * QK_DIM
REL_BUCKETS = 32
REL_MAX_DIST = 128
N_EXPERTS = 16
CAPACITY_FACTOR = 2
EXPERT_FF = D_MODEL
NORM_EPS = 1e-6
IN_COLS = SSM_WIDTH + 2 * QK_COLS + ATTN_WIDTH + 2 * D_MODEL
COL_Q = SSM_WIDTH
COL_K = COL_Q + QK_COLS
COL_V = COL_K + QK_COLS
COL_GS = COL_V + ATTN_WIDTH
COL_GA = COL_GS + D_MODEL

V7X_LANES = 128
F32_SUBLANES = 8
V7X_VMEM_BYTES = 64 * 1024 * 1024
VMEM_LIMIT = 56 * 1024 * 1024
LOG2E = 1.4426950408889634

S5_CHUNK = 16
S5_GB = 8
NEG_BIG = -1e30


def _cp(*sem):
    return pltpu.CompilerParams(dimension_semantics=sem, vmem_limit_bytes=VMEM_LIMIT)


def _norm_matmul_kernel(x_ref, g_ref, w_ref, o_ref, xcat_ref, h_ref, u_ref):
    j = pl.program_id(1)

    @pl.when(j == 0)
    def _():
        x = x_ref[...]
        r = lax.rsqrt(jnp.mean(x * x, axis=-1, keepdims=True) + NORM_EPS)
        h_ref[...] = (x * r * g_ref[...]).astype(BF16)

    res = jnp.dot(h_ref[...], w_ref[...], preferred_element_type=F32)
    o_ref[...] = res.astype(o_ref.dtype)

    @pl.when(j == 0)
    def _():
        cn = u_ref.shape[1] // S5_CHUNK
        for slab in range(u_ref.shape[0]):
            u_ref[slab] = res[:, slab * V7X_LANES:(slab + 1) * V7X_LANES]
            for t in range(S5_CHUNK):
                piece = u_ref[slab, pl.ds(t, cn, stride=S5_CHUNK), :]
                xcat_ref[slab, :, t * V7X_LANES:(t + 1) * V7X_LANES] = piece.astype(xcat_ref.dtype)


def norm_matmul(x, gain, w, *, tm, tn):
    s, d = x.shape
    n = w.shape[1]
    assert tn == SSM_WIDTH and tm % (S5_CHUNK * F32_SUBLANES) == 0
    nslab = SSM_WIDTH // V7X_LANES
    return pl.pallas_call(
        _norm_matmul_kernel,
        out_shape=(jax.ShapeDtypeStruct((s, n), BF16),
                   jax.ShapeDtypeStruct((nslab, s // S5_CHUNK, S5_CHUNK * V7X_LANES), BF16)),
        grid=(s // tm, n // tn),
        in_specs=[pl.BlockSpec((tm, d), lambda i, j: (i, 0)),
                  pl.BlockSpec((1, d), lambda i, j: (0, 0)),
                  pl.BlockSpec((d, tn), lambda i, j: (0, j))],
        out_specs=(pl.BlockSpec((tm, tn), lambda i, j: (i, j)),
                   pl.BlockSpec((nslab, tm // S5_CHUNK, S5_CHUNK * V7X_LANES), lambda i, j: (0, i, 0))),
        scratch_shapes=[pltpu.VMEM((tm, d), BF16), pltpu.VMEM((nslab, tm, V7X_LANES), F32)],
        compiler_params=_cp("parallel", "arbitrary"),
        name="norm_in_proj",
    )(x, gain.reshape(1, d).astype(F32), w)


def _attn_prep_kernel(q_ref, k_ref, v_ref, gq_ref, gk_ref, qo_ref, ko_ref, vo_ref, kn_ref):
    lane = lax.broadcasted_iota(I32, (1, V7X_LANES), 1)
    lo_mask = lane < QK_DIM

    def norm(src_ref, g_ref, a):
        x = src_ref[:, a * V7X_LANES:(a + 1) * V7X_LANES].astype(F32)
        ss = x * x
        lo = jnp.sum(jnp.where(lo_mask, ss, 0.0), axis=-1, keepdims=True)
        hi = jnp.sum(jnp.where(lo_mask, 0.0, ss), axis=-1, keepdims=True)
        ms = jnp.where(lo_mask, lo, hi) * (1.0 / QK_DIM)
        return x * lax.rsqrt(ms + NORM_EPS) * g_ref[...]

    for a in range(ATTN_HEADS):
        qo_ref[:, a * V7X_LANES:(a + 1) * V7X_LANES] = norm(q_ref, gq_ref, a).astype(qo_ref.dtype)
        kb = norm(k_ref, gk_ref, a).astype(ko_ref.dtype)
        ko_ref[a, 0] = kb
        kk = kb.astype(F32) * kb.astype(F32)
        n_lo = jnp.max(jnp.sum(jnp.where(lo_mask, kk, 0.0), axis=-1, keepdims=True), axis=0, keepdims=True)
        n_hi = jnp.max(jnp.sum(jnp.where(lo_mask, 0.0, kk), axis=-1, keepdims=True), axis=0, keepdims=True)
        kn_ref[0, a:a + 1, :] = jnp.where(lane == 0, n_lo, jnp.where(lane == 1, n_hi, 0.0))
        v = v_ref[:, a * V7X_LANES:(a + 1) * V7X_LANES].astype(F32)
        vo_ref[a, 0] = v.T.astype(vo_ref.dtype)


def attn_prep(proj, q_gain, k_gain, *, tk):
    s = proj.shape[0]
    nk = s // tk
    gq = (jnp.tile(q_gain.astype(F32), 2) * (QK_DIM ** -0.5 * LOG2E)).reshape(1, V7X_LANES)
    gk = jnp.tile(k_gain.astype(F32), 2).reshape(1, V7X_LANES)
    cq, ck, cv = COL_Q // QK_COLS, COL_K // QK_COLS, COL_V // ATTN_WIDTH
    qn, k4, vt4, kn2 = pl.pallas_call(
        _attn_prep_kernel,
        out_shape=(jax.ShapeDtypeStruct((s, QK_COLS), BF16),
                   jax.ShapeDtypeStruct((ATTN_HEADS, nk, tk, V_DIM), BF16),
                   jax.ShapeDtypeStruct((ATTN_HEADS, nk, V_DIM, tk), BF16),
                   jax.ShapeDtypeStruct((nk, ATTN_HEADS, V7X_LANES), F32)),
        grid=(nk,),
        in_specs=[pl.BlockSpec((tk, QK_COLS), lambda i: (i, cq)),
                  pl.BlockSpec((tk, QK_COLS), lambda i: (i, ck)),
                  pl.BlockSpec((tk, ATTN_WIDTH), lambda i: (i, cv)),
                  pl.BlockSpec((1, V7X_LANES), lambda i: (0, 0)),
                  pl.BlockSpec((1, V7X_LANES), lambda i: (0, 0))],
        out_specs=(pl.BlockSpec((tk, QK_COLS), lambda i: (i, 0)),
                   pl.BlockSpec((ATTN_HEADS, 1, tk, V_DIM), lambda i: (0, i, 0, 0)),
                   pl.BlockSpec((ATTN_HEADS, 1, V_DIM, tk), lambda i: (0, i, 0, 0)),
                   pl.BlockSpec((1, ATTN_HEADS, V7X_LANES), lambda i: (i, 0, 0))),
        compiler_params=_cp("parallel"),
        name="attn_prep",
    )(proj, proj, proj, gq, gk)
    kmax = jnp.sqrt(jnp.max(kn2[:, :, :2], axis=0))
    return qn, k4, vt4, kmax


ATTN_NEAR = 3
ATTN_BIAS_TILES = 5
ATTN_UNROLL = 6


def _bias_tile_kernel(rb_ref, o_ref, *, tk, tq):
    h = pl.program_id(0)
    d = pl.program_id(1)
    w = tk + tq
    m = lax.broadcasted_iota(I32, (F32_SUBLANES, w), 1)
    rel = (d - ATTN_BIAS_TILES // 2) * tk + (tk - 1) - m
    half = REL_BUCKETS // 2
    exact = half // 2
    side = jnp.where(rel > 0, half, 0).astype(I32)
    n = jnp.abs(rel)
    nf = jnp.maximum(n, 1).astype(F32)
    large = exact + (jnp.log(nf / exact) / math.log(REL_MAX_DIST / exact) * (half - exact)).astype(I32)
    large = jnp.minimum(large, half - 1)
    bucket = side + jnp.where(n < exact, n, large).astype(I32)
    val = jnp.zeros((F32_SUBLANES, w), F32)
    for b in range(REL_BUCKETS):
        val = jnp.where(bucket == b, rb_ref[b, h], val)
    table = jnp.broadcast_to(val[0:1, :] * LOG2E, (tk, w))
    o_ref[0, 0] = pltpu.roll(table, w - tk + 1, 1, stride=1, stride_axis=0)[:, :tq]


def bias_tiles(rel_bias, *, tk, tq):
    assert tk == tq and (tk + tq) & (tk + tq - 1) == 0
    return pl.pallas_call(
        functools.partial(_bias_tile_kernel, tk=tk, tq=tq),
        out_shape=jax.ShapeDtypeStruct((ATTN_HEADS, ATTN_BIAS_TILES, tk, tq), F32),
        grid=(ATTN_HEADS, ATTN_BIAS_TILES),
        in_specs=[pl.BlockSpec(memory_space=pltpu.SMEM)],
        out_specs=pl.BlockSpec((1, 1, tk, tq), lambda h, d: (h, d, 0, 0)),
        compiler_params=_cp("parallel", "parallel"),
        name="t5_bias_tiles",
    )(rel_bias.astype(F32))


FAST_MIN_SUM = 2.0 ** -90


def _attn_kernel(lam_ref, far_ref, kmax_ref, q_ref, k_ref, vt_ref, bias_ref, g_ref, o_ref,
                 s_ref, cm_ref, m_ref, l_ref, acc_ref, p_ref, *, nk, tk, tq, out_scale):
    h = pl.program_id(0)
    i = pl.program_id(1)
    lane = lax.broadcasted_iota(I32, (1, V7X_LANES), 1)
    q = q_ref[...]
    zero = jnp.zeros_like(q)
    qmaps = (jnp.where(lane < QK_DIM, q, zero), jnp.where(lane < QK_DIM, zero, q))

    n0 = jnp.clip(i - 1, 0, nk - ATTN_NEAR)
    nfar = nk - ATTN_NEAR

    def far_tile(t):
        return jnp.where(t < n0, t, t + ATTN_NEAR)

    def far_const(j):
        return jnp.where(j < i, far_ref[h, 0], far_ref[h, 1])

    def near_bias(j):
        return bias_ref[0, j - i + ATTN_BIAS_TILES // 2]

    qf = q.astype(F32)
    row8 = lax.broadcasted_iota(I32, (F32_SUBLANES, V7X_LANES), 0)
    lane8 = lax.broadcasted_iota(I32, (F32_SUBLANES, V7X_LANES), 1)
    pick = jnp.where(jnp.logical_or(jnp.logical_and(row8 == 0, lane8 < QK_DIM),
                                    jnp.logical_and(row8 == 1, lane8 >= QK_DIM)), 1.0, 0.0)
    qn2 = lax.dot_general(pick, qf * qf, (((1,), (1,)), ((), ())), preferred_element_type=F32,
                          precision=lax.Precision.HIGHEST)
    shift = [jnp.sqrt(qn2[c:c + 1, :]) * kmax_ref[h, c] + far_ref[h, 2] for c in range(2)]
    l_ref[...] = jnp.zeros(l_ref.shape, F32)
    acc_ref[...] = jnp.zeros(acc_ref.shape, F32)

    def fast_probs(j, slot, bias_tile, bias_const):
        kt = k_ref[0, j]
        for c in range(2):
            s = lax.dot_general(kt, qmaps[c], (((1,), (1,)), ((), ())), preferred_element_type=F32)
            if bias_tile is not None:
                s = s + bias_tile
            p = jnp.exp2(s - (shift[c] - bias_const))
            l_ref[c:c + 1, :] += jnp.sum(p, axis=0, keepdims=True)
            p_ref[slot, c] = p.astype(BF16)

    def fast_pv(j, slot):
        vt = vt_ref[0, j]
        for c in range(2):
            acc_ref[c] += jnp.dot(vt, p_ref[slot, c], preferred_element_type=F32)

    fast_probs(n0, 0, near_bias(n0), 0.0)
    for w in range(1, ATTN_NEAR):
        fast_probs(n0 + w, w % 2, near_bias(n0 + w), 0.0)
        fast_pv(n0 + w - 1, (w - 1) % 2)
    j0 = far_tile(0)
    fast_probs(j0, ATTN_NEAR % 2, None, far_const(j0))
    fast_pv(n0 + ATTN_NEAR - 1, (ATTN_NEAR - 1) % 2)

    def fast_group(u, jprev):
        for w in range(ATTN_UNROLL):
            jn = far_tile(ATTN_UNROLL * u + w + 1)
            fast_probs(jn, (ATTN_NEAR + 1 + w) % 2, None, far_const(jn))
            fast_pv(jprev, (ATTN_NEAR + w) % 2)
            jprev = jn
        return jprev

    jl = lax.fori_loop(0, (nfar - 1) // ATTN_UNROLL, fast_group, j0)
    fast_pv(jl, (ATTN_NEAR + nfar - 1) % 2)

    lmin = jnp.min(jnp.minimum(l_ref[0:1, :], l_ref[1:2, :]))

    @pl.when(jnp.logical_not(lmin >= FAST_MIN_SUM))
    def _():
        _attn_running_max(h, i, n0, nfar, far_tile, far_const, near_bias, qmaps, k_ref, vt_ref,
                          s_ref, cm_ref, m_ref, l_ref, acc_ref, tq=tq)

    o1 = acc_ref[0] / l_ref[0:1, :]
    o2 = acc_ref[1] / l_ref[1:2, :]
    o = o1 - lam_ref[0] * o2
    r = lax.rsqrt(jnp.mean(o * o, axis=0, keepdims=True) + NORM_EPS)
    o = o * r * g_ref[...] * out_scale
    o_ref[...] = o.T.astype(o_ref.dtype)


def _attn_running_max(h, i, n0, nfar, far_tile, far_const, near_bias, qmaps, k_ref, vt_ref,
                      s_ref, cm_ref, m_ref, l_ref, acc_ref, *, tq):
    m_ref[...] = jnp.full(m_ref.shape, NEG_BIG, F32)
    l_ref[...] = jnp.zeros(l_ref.shape, F32)
    acc_ref[...] = jnp.zeros(acc_ref.shape, F32)

    def scores(j, slot, bias_tile, bias_const):
        kt = k_ref[0, j]
        for c in range(2):
            s = lax.dot_general(kt, qmaps[c], (((1,), (1,)), ((), ())),
                                preferred_element_type=F32)
            if bias_tile is not None:
                s = s + bias_tile
            s_ref[slot, c] = s
            cm_ref[slot, c:c + 1, :] = jnp.max(s, axis=0, keepdims=True) + bias_const
        cm_ref[slot, 2:3, :] = jnp.zeros((1, tq), F32) + bias_const

    def absorb(j, slot):
        vt = vt_ref[0, j]
        cb = cm_ref[slot, 2:3, :]
        for c in range(2):
            m_old = m_ref[c:c + 1, :]
            m_new = jnp.maximum(m_old, cm_ref[slot, c:c + 1, :])
            alpha = jnp.exp2(m_old - m_new)
            p = jnp.exp2(s_ref[slot, c] - (m_new - cb))
            l_ref[c:c + 1, :] = alpha * l_ref[c:c + 1, :] + jnp.sum(p, axis=0, keepdims=True)
            acc_ref[c] = alpha * acc_ref[c] + jnp.dot(vt, p.astype(BF16), preferred_element_type=F32)
            m_ref[c:c + 1, :] = m_new

    scores(n0, 0, near_bias(n0), 0.0)
    scores(n0 + 1, 1, near_bias(n0 + 1), 0.0)
    absorb(n0, 0)
    scores(n0 + 2, 0, near_bias(n0 + 2), 0.0)
    absorb(n0 + 1, 1)
    j0 = far_tile(0)
    scores(j0, 1, None, far_const(j0))
    absorb(n0 + 2, 0)

    def group(u, jprev):
        for w in range(ATTN_UNROLL):
            jn = far_tile(ATTN_UNROLL * u + w + 1)
            scores(jn, w % 2, None, far_const(jn))
            absorb(jprev, (w + 1) % 2)
            jprev = jn
        return jprev

    jlast = lax.fori_loop(0, (nfar - 1) // ATTN_UNROLL, group, j0)
    absorb(jlast, 1)


def attn_bias_consts(rel_bias):
    half = REL_BUCKETS // 2
    rb = rel_bias.astype(F32) * LOG2E
    return jnp.stack([rb[half - 1], rb[REL_BUCKETS - 1], jnp.max(rb, axis=0)], axis=1)


def diff_attention(qn, k4, vt4, bias, far_bias, kmax, lam, subln_gain, *, lam_init, tq, tk):
    s = qn.shape[0]
    nk = s // tk
    assert tq == tk and nk > ATTN_NEAR and (nk - ATTN_NEAR - 1) % ATTN_UNROLL == 0
    kern = functools.partial(_attn_kernel, nk=nk, tk=tk, tq=tq, out_scale=1.0 - lam_init)
    return pl.pallas_call(
        kern,
        out_shape=jax.ShapeDtypeStruct((s, ATTN_WIDTH), BF16),
        grid=(ATTN_HEADS, s // tq),
        in_specs=[pl.BlockSpec(memory_space=pltpu.SMEM),
                  pl.BlockSpec(memory_space=pltpu.SMEM),
                  pl.BlockSpec(memory_space=pltpu.SMEM),
                  pl.BlockSpec((tq, V_DIM), lambda h, i: (i, h)),
                  pl.BlockSpec((1, nk, tk, V_DIM), lambda h, i: (h, 0, 0, 0)),
                  pl.BlockSpec((1, nk, V_DIM, tk), lambda h, i: (h, 0, 0, 0)),
                  pl.BlockSpec((1, ATTN_BIAS_TILES, tk, tq), lambda h, i: (h, 0, 0, 0)),
                  pl.BlockSpec((V_DIM, 1), lambda h, i: (0, 0))],
        out_specs=pl.BlockSpec((tq, V_DIM), lambda h, i: (i, h)),
        scratch_shapes=[pltpu.VMEM((2, 2, tk, tq), F32),
                        pltpu.VMEM((2, F32_SUBLANES, tq), F32),
                        pltpu.VMEM((F32_SUBLANES, tq), F32),
                        pltpu.VMEM((F32_SUBLANES, tq), F32),
                        pltpu.VMEM((2, V_DIM, tq), F32),
                        pltpu.VMEM((2, 2, tk, tq), BF16)],
        compiler_params=_cp("parallel", "arbitrary"),
        name="diff_attention",
    )(lam.reshape(1).astype(F32), far_bias, kmax, qn, k4, vt4, bias,
      subln_gain.reshape(V_DIM, 1).astype(F32))


def s5_operators(a_re, a_im, log_dt, b_re, b_im, c_re, c_im):
    t_len, hp = S5_CHUNK, lax.Precision.HIGHEST
    a_re, a_im = a_re.astype(F32), a_im.astype(F32)
    dt = jnp.exp(log_dt.astype(F32))[..., None]
    steps = jnp.arange(t_len + 1, dtype=F32)[:, None, None, None]
    mag = jnp.exp(a_re * dt * steps)
    ang = a_im * dt * steps
    pw_re, pw_im = mag * jnp.cos(ang), mag * jnp.sin(ang)
    den = a_re * a_re + a_im * a_im
    nr, ni = pw_re[1] - 1.0, pw_im[1]
    coef_re = ((nr * a_re + ni * a_im) / den)[..., None]
    coef_im = ((ni * a_re - nr * a_im) / den)[..., None]
    b_re, b_im = b_re.astype(F32), b_im.astype(F32)
    bb_re = coef_re * b_re - coef_im * b_im
    bb_im = coef_re * b_im + coef_im * b_re
    c_re, c_im = c_re.astype(F32), c_im.astype(F32)

    g, n_st, p_ch = a_re.shape[1], SSM_STATE, SSM_GROUP
    tp = t_len * p_ch
    pwt_re = jnp.transpose(pw_re, (1, 2, 3, 0))
    pwt_im = jnp.transpose(pw_im, (1, 2, 3, 0))
    ct_re = jnp.transpose(c_re, (0, 1, 3, 2))
    ct_im = jnp.transpose(c_im, (0, 1, 3, 2))
    bbt_re = jnp.transpose(bb_re, (0, 1, 3, 2))
    bbt_im = jnp.transpose(bb_im, (0, 1, 3, 2))
    cp_re = ct_re[:, :, :, None, :] * pwt_re[..., None] - ct_im[:, :, :, None, :] * pwt_im[..., None]
    cp_im = ct_re[:, :, :, None, :] * pwt_im[..., None] + ct_im[:, :, :, None, :] * pwt_re[..., None]

    zlag = jnp.zeros((g, n_st, t_len - 1, p_ch), F32)

    def lagged(cp):
        return (jnp.concatenate([zlag, cp[0, :, :, :t_len]], axis=2),
                jnp.concatenate([jnp.flip(cp[1, :, :, :t_len], axis=2), zlag], axis=2))

    rf_re, rb_re = lagged(cp_re)
    rf_im, rb_im = lagged(cp_im)
    r_cat = jnp.concatenate([rf_re, rf_im, rb_re, rb_im], axis=1).reshape(g, 4 * n_st, (2 * t_len - 1) * p_ch)
    a_cat = jnp.concatenate([bbt_re[0], -bbt_im[0], bbt_re[1], -bbt_im[1]], axis=-1)
    kp = jnp.einsum('gpk,gkx->gpx', a_cat, r_cat, precision=hp)
    toep = jnp.stack([kp[:, :, (t_len - 1 - j) * p_ch:(t_len - 1 - j) * p_ch + tp] for j in range(t_len)],
                     axis=1).reshape(g, tp, tp).astype(BF16)

    def seg_powers(d, reverse):
        pr = jnp.transpose(pw_re[:t_len, d], (1, 0, 2))
        pi = jnp.transpose(pw_im[:t_len, d], (1, 0, 2))
        if reverse:
            pr, pi = jnp.flip(pr, axis=1), jnp.flip(pi, axis=1)
        return pr, pi

    prf, pif = seg_powers(0, True)
    prb, pib = seg_powers(1, False)
    pa = jnp.concatenate([prf, pif, pif, prf, prb, pib, pib, prb], axis=-1)
    pb = jnp.concatenate([-pif, prf, prf, -pif, -pib, prb, prb, -pib], axis=-1)
    br = jnp.concatenate([bbt_re[0]] * 4 + [bbt_re[1]] * 4, axis=-1)
    bi = jnp.concatenate([bbt_im[0]] * 4 + [bbt_im[1]] * 4, axis=-1)
    smap = (br[:, None] * pa[:, :, None, :] + bi[:, None] * pb[:, :, None, :]).reshape(g, tp, 8 * n_st)
    smap = smap.astype(BF16)

    def out_map(d, reverse):
        wr, wi = cp_re[d][:, :, 1:t_len + 1], cp_im[d][:, :, 1:t_len + 1]
        if reverse:
            wr, wi = jnp.flip(wr, axis=2), jnp.flip(wi, axis=2)
        return [wr.reshape(g, n_st, tp), -wi.reshape(g, n_st, tp)]

    mc = jnp.concatenate(out_map(0, False) + out_map(1, True), axis=1).astype(BF16)

    def carry(d):
        ar, ai = pw_re[t_len, d], pw_im[t_len, d]
        return [jnp.concatenate([ar, ar], -1), jnp.concatenate([-ai, ai], -1),
                jnp.concatenate([ai, -ai], -1)]

    coef = jnp.stack(carry(0) + carry(1), axis=0)
    return toep, smap, mc, coef


def _gelu_tanh(x):
    return 0.5 * x * (1.0 + jnp.tanh(math.sqrt(2.0 / math.pi) * (x + 0.044715 * (x * x * x))))


def _s5_kernel(x_ref, sel_ref, toep_ref, smap_ref, mc_ref, coef_ref, d_ref, y_ref,
               u_ref, ef_ref, efs_ref, eb_ref, ebs_ref, *, cn):
    gb = toep_ref.shape[0]
    tp = toep_ref.shape[1]
    n2 = 2 * SSM_STATE
    xcat = x_ref[0]

    def sel(g):
        off = (gb - 1 - g) * SSM_GROUP
        return sel_ref[off:off + xcat.shape[1], :]

    for g in range(gb):
        u_ref[g] = jnp.dot(xcat, sel(g), preferred_element_type=F32).astype(BF16)
        e = jnp.dot(u_ref[g], smap_ref[g], preferred_element_type=F32)
        for r, ref in enumerate((ef_ref, efs_ref, eb_ref, ebs_ref)):
            ref[pl.ds(g, cn, stride=gb), :] = e[:, r * n2:(r + 1) * n2]

    cf, cfs, cfw = coef_ref[0], coef_ref[1], coef_ref[2]
    cb, cbs, cbw = coef_ref[3], coef_ref[4], coef_ref[5]

    def step(c, carry):
        s, sw, r, rw = carry
        fo = pl.multiple_of(c * gb, gb)
        bo = pl.multiple_of((cn - 1 - c) * gb, gb)
        e, es = ef_ref[pl.ds(fo, gb), :], efs_ref[pl.ds(fo, gb), :]
        ef_ref[pl.ds(fo, gb), :] = s
        s, sw = cf * s + cfs * sw + e, cf * sw + cfw * s + es
        e, es = eb_ref[pl.ds(bo, gb), :], ebs_ref[pl.ds(bo, gb), :]
        eb_ref[pl.ds(bo, gb), :] = r
        r, rw = cb * r + cbs * rw + e, cb * rw + cbw * r + es
        return s, sw, r, rw

    z = jnp.zeros((gb, n2), F32)
    lax.fori_loop(0, cn, step, (z, z, z, z))

    ycat = None
    for g in range(gb):
        u = u_ref[g]
        st = jnp.concatenate([ef_ref[pl.ds(g, cn, stride=gb), :], eb_ref[pl.ds(g, cn, stride=gb), :]],
                             axis=1).astype(BF16)
        y = (jnp.dot(u, toep_ref[g], preferred_element_type=F32)
             + jnp.dot(st, mc_ref[g], preferred_element_type=F32)
             + u.astype(F32) * d_ref[g])
        placed = lax.dot_general(_gelu_tanh(y).astype(BF16), sel(g), (((1,), (1,)), ((), ())),
                                 preferred_element_type=F32)
        ycat = placed if ycat is None else ycat + placed
    y_ref[0] = ycat.astype(y_ref.dtype)


def s5_lane_selector():
    gb, p, t_len = S5_GB, SSM_GROUP, S5_CHUNK
    r = jnp.arange(t_len * V7X_LANES + (gb - 1) * p)[:, None] - (gb - 1) * p
    c = jnp.arange(t_len * p)[None, :]
    hit = (r >= 0) & (r // V7X_LANES == c // p) & ((r % V7X_LANES) // p == 0) & (r % p == c % p)
    return hit.astype(BF16)


def s5_mixer(xcat, ops, ssm_d, layer):
    toep, smap, mc, coef = ops
    g, p, t_len = SSM_GROUPS, SSM_GROUP, S5_CHUNK
    nslab, cn, _ = xcat.shape
    tp = t_len * p
    gb = S5_GB
    assert gb * p == V7X_LANES and nslab == g // gb
    sel = s5_lane_selector()
    dsk = jnp.tile(ssm_d.astype(F32).reshape(g, 1, p), (1, 1, t_len))
    ycat = pl.pallas_call(
        functools.partial(_s5_kernel, cn=cn),
        out_shape=jax.ShapeDtypeStruct((nslab, cn, t_len * V7X_LANES), BF16),
        grid=(nslab,),
        in_specs=[pl.BlockSpec((1, cn, t_len * V7X_LANES), lambda i: (i, 0, 0)),
                  pl.BlockSpec(sel.shape, lambda i: (0, 0)),
                  pl.BlockSpec((None, gb, tp, tp), lambda i: (layer, i, 0, 0)),
                  pl.BlockSpec((None, gb, tp, smap.shape[3]), lambda i: (layer, i, 0, 0)),
                  pl.BlockSpec((None, gb, mc.shape[2], tp), lambda i: (layer, i, 0, 0)),
                  pl.BlockSpec((None, 6, gb, 2 * SSM_STATE), lambda i: (layer, 0, i, 0)),
                  pl.BlockSpec((gb, 1, tp), lambda i: (i, 0, 0))],
        out_specs=pl.BlockSpec((1, cn, t_len * V7X_LANES), lambda i: (i, 0, 0)),
        scratch_shapes=[pltpu.VMEM((gb, cn, tp), BF16)]
        + [pltpu.VMEM((cn * gb, 2 * SSM_STATE), F32) for _ in range(4)],
        compiler_params=_cp("parallel"),
        name="s5_chunked_scan",
    )(xcat, sel, toep, smap, mc, coef, dsk)
    return ycat


def _sigmoid(x):
    return 1.0 / (1.0 + jnp.exp(-x))


def _glu_kernel(ycat_ref, w_ref, o_ref, y_ref):
    cn = ycat_ref.shape[1]
    for slab in range(ycat_ref.shape[0]):
        for t in range(S5_CHUNK):
            piece = ycat_ref[slab, :, t * V7X_LANES:(t + 1) * V7X_LANES].astype(F32)
            y_ref[slab, pl.ds(t, cn, stride=S5_CHUNK), :] = piece
    y = jnp.concatenate([y_ref[slab] for slab in range(ycat_ref.shape[0])], axis=1)
    z = jnp.dot(y.astype(BF16), w_ref[0].astype(BF16), preferred_element_type=F32)
    o_ref[...] = (y * _sigmoid(z)).astype(o_ref.dtype)


def glu(ycat, w, layer, *, tm):
    nslab, cn, _ = ycat.shape
    s, d = cn * S5_CHUNK, nslab * V7X_LANES
    return pl.pallas_call(
        _glu_kernel,
        out_shape=jax.ShapeDtypeStruct((s, d), BF16),
        grid=(s // tm,),
        in_specs=[pl.BlockSpec((nslab, tm // S5_CHUNK, S5_CHUNK * V7X_LANES), lambda i: (0, i, 0)),
                  pl.BlockSpec((1, d, d), lambda i: (layer, 0, 0))],
        out_specs=pl.BlockSpec((tm, d), lambda i: (i, 0)),
        scratch_shapes=[pltpu.VMEM((nslab, tm, V7X_LANES), F32)],
        compiler_params=_cp("parallel"),
        name="half_glu",
    )(ycat, w)


def _merge_kernel(yg_ref, ao_ref, gs_ref, ga_ref, ws_ref, wa_ref, o_ref):
    a = jnp.dot(yg_ref[...], ws_ref[0].astype(BF16), preferred_element_type=F32)
    b = jnp.dot(ao_ref[...], wa_ref[0].astype(BF16), preferred_element_type=F32)
    o = _sigmoid(gs_ref[...].astype(F32)) * a + _sigmoid(ga_ref[...].astype(F32)) * b
    o_ref[...] = o.astype(o_ref.dtype)


def gated_merge(yg, ao, proj, ws, wa, layer, *, tm, tn):
    s, k = yg.shape
    n = ws.shape[2]
    cs, ca = COL_GS // tn, COL_GA // tn
    return pl.pallas_call(
        _merge_kernel,
        out_shape=jax.ShapeDtypeStruct((s, n), BF16),
        grid=(s // tm, n // tn),
        in_specs=[pl.BlockSpec((tm, k), lambda i, j: (i, 0)),
                  pl.BlockSpec((tm, k), lambda i, j: (i, 0)),
                  pl.BlockSpec((tm, tn), lambda i, j: (i, cs + j)),
                  pl.BlockSpec((tm, tn), lambda i, j: (i, ca + j)),
                  pl.BlockSpec((1, k, tn), lambda i, j: (layer, 0, j)),
                  pl.BlockSpec((1, k, tn), lambda i, j: (layer, 0, j))],
        out_specs=pl.BlockSpec((tm, tn), lambda i, j: (i, j)),
        compiler_params=_cp("parallel", "arbitrary"),
        name="gated_merge",
    )(yg, ao, proj, proj, ws, wa)


def _out_router_kernel(m_ref, x_ref, w_ref, g_ref, wrt_ref, xo_ref, h_ref, afft_ref):
    x1 = x_ref[...] + jnp.dot(m_ref[...], w_ref[...], preferred_element_type=F32)
    xo_ref[...] = x1
    r = lax.rsqrt(jnp.mean(x1 * x1, axis=-1, keepdims=True) + NORM_EPS)
    h = x1 * r * g_ref[...]
    h_ref[...] = h
    lgt = lax.dot_general(wrt_ref[...], h, (((1,), (1,)), ((), ())),
                          preferred_element_type=F32, precision=lax.Precision.HIGHEST)
    et = jnp.exp(lgt - jnp.max(lgt, axis=0, keepdims=True))
    afft_ref[...] = et / jnp.sum(et, axis=0, keepdims=True)


def out_proj_router(merged, x, w_out, gain, w_router, *, tm):
    s, d = x.shape
    e = w_router.shape[1]
    return pl.pallas_call(
        _out_router_kernel,
        out_shape=(jax.ShapeDtypeStruct((s, d), F32), jax.ShapeDtypeStruct((s, d), F32),
                   jax.ShapeDtypeStruct((e, s), F32)),
        grid=(s // tm,),
        in_specs=[pl.BlockSpec((tm, d), lambda i: (i, 0)),
                  pl.BlockSpec((tm, d), lambda i: (i, 0)),
                  pl.BlockSpec((d, d), lambda i: (0, 0)),
                  pl.BlockSpec((1, d), lambda i: (0, 0)),
                  pl.BlockSpec((e, d), lambda i: (0, 0))],
        out_specs=(pl.BlockSpec((tm, d), lambda i: (i, 0)),
                   pl.BlockSpec((tm, d), lambda i: (i, 0)),
                   pl.BlockSpec((e, tm), lambda i: (0, i))),
        compiler_params=_cp("parallel"),
        name="out_proj_router",
    )(merged, x, w_out, gain.reshape(1, d).astype(F32), w_router.astype(F32).T)


SLOT_LO_BITS = 6
SLOT_LO = 1 << SLOT_LO_BITS


def _select_kernel(afft_ref, slot_ref, cum_ref, idx_ref, acc_ref, *, s, cap, blk):
    ne = afft_ref.shape[0]

    def bit_body(b, thr):
        cand = thr | jnp.left_shift(jnp.ones((ne, 1), I32), 30 - b)
        keys = pltpu.bitcast(afft_ref[...], I32)
        cnt = jnp.sum((keys >= cand).astype(I32), axis=1, keepdims=True)
        return jnp.where(cnt >= cap, cand, thr)

    thr = lax.fori_loop(0, 31, bit_body, jnp.zeros((ne, 1), I32))
    keys = pltpu.bitcast(afft_ref[...], I32)
    need = cap - jnp.sum((keys > thr).astype(I32), axis=1, keepdims=True)

    ri = lax.broadcasted_iota(I32, (blk, blk), 0)
    ci = lax.broadcasted_iota(I32, (blk, blk), 1)
    upper = jnp.where(ri < ci, 1.0, 0.0).astype(BF16)
    na = cap // SLOT_LO
    acol = lax.broadcasted_iota(I32, (na, 1), 0)
    bcol = lax.broadcasted_iota(I32, (SLOT_LO, 1), 0)
    tlane = lax.broadcasted_iota(I32, (1, blk), 1)
    acc_ref[...] = jnp.zeros(acc_ref.shape, F32)

    def blk_body(b, carry):
        ceq, csel = carry
        off = pl.multiple_of(b * blk, blk)
        kb = pltpu.bitcast(afft_ref[:, pl.ds(off, blk)], I32)
        gt = kb > thr
        eq = kb == thr
        eqf = jnp.where(eq, 1.0, 0.0)
        rank_eq = jnp.dot(eqf.astype(BF16), upper, preferred_element_type=F32) + ceq
        sel = jnp.logical_or(gt, jnp.logical_and(eq, rank_eq < need.astype(F32)))
        self_ = jnp.where(sel, 1.0, 0.0)
        cum = jnp.dot(self_.astype(BF16), upper, preferred_element_type=F32) + csel
        cum_i = cum.astype(I32)
        cum_ref[:, pl.ds(off, blk)] = cum_i
        slot = jnp.where(sel, cum_i, -1)
        slot_ref[:, pl.ds(off, blk)] = slot
        tok = off + tlane
        hi = (tok // V7X_LANES).astype(F32)
        lo = (tok % V7X_LANES).astype(F32)
        for e in range(ne):
            srow = slot[e:e + 1, :]
            in_a = lax.shift_right_arithmetic(srow, SLOT_LO_BITS) == acol
            lhs = jnp.concatenate([jnp.where(in_a, hi, 0.0), jnp.where(in_a, lo, 0.0)],
                                  axis=0).astype(BF16)
            rhs = jnp.where((srow & (SLOT_LO - 1)) == bcol, 1.0, 0.0).astype(BF16)
            acc_ref[e] += lax.dot_general(lhs, rhs, (((1,), (1,)), ((), ())), preferred_element_type=F32)
        return (ceq + jnp.sum(eqf, axis=1, keepdims=True), csel + jnp.sum(self_, axis=1, keepdims=True))

    z = jnp.zeros((ne, 1), F32)
    lax.fori_loop(0, s // blk, blk_body, (z, z))
    a = acc_ref[...]
    idx_ref[...] = (a[:, :na, :] * float(V7X_LANES) + a[:, na:, :]).astype(I32)


def expert_select(afft, *, cap, blk):
    ne, s = afft.shape
    assert cap % SLOT_LO == 0
    return pl.pallas_call(
        functools.partial(_select_kernel, s=s, cap=cap, blk=blk),
        out_shape=(jax.ShapeDtypeStruct((ne, s), I32), jax.ShapeDtypeStruct((ne, s), I32),
                   jax.ShapeDtypeStruct((ne, cap // SLOT_LO, SLOT_LO), I32)),
        scratch_shapes=[pltpu.VMEM((ne, 2 * (cap // SLOT_LO), SLOT_LO), F32)],
        compiler_params=pltpu.CompilerParams(vmem_limit_bytes=VMEM_LIMIT),
        name="expert_select",
    )(afft)


GATHER_UNROLL = 8


def _ffn_kernel(idx_ref, h_hbm, wg_ref, wu_ref, wd_ref, y_ref, xg32_ref, xg_ref, sem, *, cap):
    e = pl.program_id(0)
    f = pl.program_id(1)

    def row_copy(r):
        tok = idx_ref[e * cap + r]
        return pltpu.make_async_copy(h_hbm.at[pl.ds(tok, 1)], xg32_ref.at[pl.ds(r, 1)], sem)

    @pl.when(f == 0)
    def _():
        def start(rb, c):
            for w in range(GATHER_UNROLL):
                row_copy(rb * GATHER_UNROLL + w).start()
            return c

        lax.fori_loop(0, cap // GATHER_UNROLL, start, 0)
        y_ref[...] = jnp.zeros(y_ref.shape, F32)
        pltpu.make_async_copy(h_hbm.at[pl.ds(0, cap)], xg32_ref, sem).wait()
        xg_ref[...] = xg32_ref[...].astype(BF16)

    xg = xg_ref[...]
    a = jnp.dot(xg, wg_ref[0, 0].astype(BF16), preferred_element_type=F32)
    b = jnp.dot(xg, wu_ref[0, 0].astype(BF16), preferred_element_type=F32)
    hid = (a * _sigmoid(a) * b).astype(BF16)
    y_ref[0, 0:cap, :] += jnp.dot(hid, wd_ref[0, 0].astype(BF16), preferred_element_type=F32)


def expert_ffn(idx, h2, wg, wu, wd, layer, *, cap, pad, fc):
    _, ne, d, ff = wg.shape
    grid_spec = pltpu.PrefetchScalarGridSpec(
        num_scalar_prefetch=1,
        grid=(ne, ff // fc),
        in_specs=[pl.BlockSpec(memory_space=pl.ANY),
                  pl.BlockSpec((1, 1, d, fc), lambda e, f, idx: (layer, e, 0, f)),
                  pl.BlockSpec((1, 1, d, fc), lambda e, f, idx: (layer, e, 0, f)),
                  pl.BlockSpec((1, 1, fc, d), lambda e, f, idx: (layer, e, f, 0))],
        out_specs=pl.BlockSpec((1, cap + pad, d), lambda e, f, idx: (e, 0, 0)),
        scratch_shapes=[pltpu.VMEM((cap, d), F32), pltpu.VMEM((cap, d), BF16),
                        pltpu.SemaphoreType.DMA(())],
    )
    return pl.pallas_call(
        functools.partial(_ffn_kernel, cap=cap),
        out_shape=jax.ShapeDtypeStruct((ne, cap + pad, d), F32),
        grid_spec=grid_spec,
        compiler_params=_cp("arbitrary", "arbitrary"),
        name="expert_ffn",
    )(idx.reshape(-1), h2, wg, wu, wd)


def _combine_kernel(st_ref, x_ref, aff_ref, slot_ref, y_hbm, o_ref, buf_ref, xbuf_ref, sem, xsem,
                    *, ne, rows, nt):
    t = pl.program_id(0)
    tm = x_ref.shape[0]

    def chunk_start(tt, e, c):
        st8 = (st_ref[tt * ne + e] // F32_SUBLANES) * F32_SUBLANES
        return pl.multiple_of(st8 + c * rows, F32_SUBLANES)

    def first_copy(tt, e, slot):
        return pltpu.make_async_copy(y_hbm.at[e, pl.ds(chunk_start(tt, e, 0), rows)],
                                     buf_ref.at[slot, e], sem.at[slot, e])

    @pl.when(t == 0)
    def _():
        for e in range(ne):
            first_copy(0, e, 0).start()

    @pl.when(t + 1 < nt)
    def _():
        for e in range(ne):
            first_copy(t + 1, e, (t + 1) % 2).start()

    slot = t % 2
    for e in range(ne):
        first_copy(t, e, slot).wait()

    pair = V7X_LANES // rows
    lane = lax.broadcasted_iota(I32, (1, V7X_LANES), 1)
    hi_parts, lo_parts = [], []
    for a in range(ne // pair):
        rel = jnp.zeros((tm, V7X_LANES), I32)
        gate = jnp.zeros((tm, V7X_LANES), F32)
        for b in range(pair):
            e = a * pair + b
            in_e = jnp.logical_and(lane >= b * rows, lane < (b + 1) * rows)
            rel = jnp.where(in_e, slot_ref[:, e:e + 1] - chunk_start(t, e, 0) + b * rows, rel)
            gate = jnp.where(in_e, aff_ref[:, e:e + 1], gate)
        gate = jnp.where(rel == lane, gate, 0.0)
        g_hi = gate.astype(BF16)
        hi_parts.append(g_hi)
        lo_parts.append((gate - g_hi.astype(F32)).astype(BF16))
    rhs = buf_ref[slot].reshape(ne * rows, x_ref.shape[1]).astype(BF16)
    o_ref[...] = (x_ref[...]
                  + jnp.dot(jnp.concatenate(hi_parts, axis=1), rhs, preferred_element_type=F32)
                  + jnp.dot(jnp.concatenate(lo_parts, axis=1), rhs, preferred_element_type=F32))

    lane_r = lax.broadcasted_iota(I32, (1, rows), 1)
    ends = [st_ref[(t + 1) * ne + e] for e in range(ne)]
    any_extra = functools.reduce(jnp.logical_or, [ends[e] > chunk_start(t, e, 1) for e in range(ne)])

    @pl.when(any_extra)
    def _():
        for e in range(ne):
            for c in range(1, tm // rows + 1):

                @pl.when(ends[e] > chunk_start(t, e, c))
                def _():
                    cp = pltpu.make_async_copy(y_hbm.at[e, pl.ds(chunk_start(t, e, c), rows)],
                                               xbuf_ref, xsem)
                    cp.start()
                    cp.wait()
                    rel = slot_ref[:, e:e + 1] - chunk_start(t, e, c)
                    oh = jnp.where(rel == lane_r, 1.0, 0.0).astype(BF16)
                    contrib = jnp.dot(oh, xbuf_ref[...].astype(BF16), preferred_element_type=F32)
                    o_ref[...] += aff_ref[:, e:e + 1] * contrib


def moe_combine(starts, x1, aff, slot, ye, *, tm, rows):
    s, d = x1.shape
    ne = aff.shape[1]
    nt = s // tm
    assert V7X_LANES % rows == 0 and ne % (V7X_LANES // rows) == 0
    grid_spec = pltpu.PrefetchScalarGridSpec(
        num_scalar_prefetch=1,
        grid=(nt,),
        in_specs=[pl.BlockSpec((tm, d), lambda t, st: (t, 0)),
                  pl.BlockSpec((tm, ne), lambda t, st: (t, 0)),
                  pl.BlockSpec((tm, ne), lambda t, st: (t, 0)),
                  pl.BlockSpec(memory_space=pl.ANY)],
        out_specs=pl.BlockSpec((tm, d), lambda t, st: (t, 0)),
        scratch_shapes=[pltpu.VMEM((2, ne, rows, d), F32), pltpu.VMEM((rows, d), F32),
                        pltpu.SemaphoreType.DMA((2, ne)), pltpu.SemaphoreType.DMA(())],
    )
    return pl.pallas_call(
        functools.partial(_combine_kernel, ne=ne, rows=rows, nt=nt),
        out_shape=jax.ShapeDtypeStruct((s, d), F32),
        grid_spec=grid_spec,
        compiler_params=_cp("arbitrary"),
        name="moe_combine",
    )(starts.reshape(-1), x1, aff, slot, ye)


TM_PROJ, TN_PROJ = 1024, 1024
TM_ROW = 256
TM_OUT = 512
ATTN_TILE = 512
SEL_BLK = 256
FFN_CHUNK = 256
COMBINE_ROWS = 64


def _layer(x, l, p, bias, far_bias):
    s = x.shape[0]
    lam_init = 0.8 - 0.6 * math.exp(-0.3 * l)
    proj, xcat = norm_matmul(x, p["norm_mix"], p["w_in"].astype(BF16), tm=min(TM_PROJ, s), tn=TN_PROJ)

    ycat = s5_mixer(xcat, p["s5_ops"], p["ssm_d"], l)
    yg = glu(ycat, p["w_glu"], l, tm=min(TM_PROJ, s))

    qn, k4, vt4, kmax = attn_prep(proj, p["q_gain"], p["k_gain"], tk=ATTN_TILE)
    lam = (jnp.exp(jnp.sum(p["lambda_q1"].astype(F32) * p["lambda_k1"].astype(F32)))
           - jnp.exp(jnp.sum(p["lambda_q2"].astype(F32) * p["lambda_k2"].astype(F32))) + lam_init)
    ao = diff_attention(qn, k4, vt4, bias, far_bias, kmax, lam, p["subln_gain"], lam_init=lam_init,
                        tq=ATTN_TILE, tk=ATTN_TILE)

    merged = gated_merge(yg, ao, proj, p["w_ssm_branch"], p["w_attn_branch"], l,
                         tm=min(TM_PROJ, s), tn=TN_PROJ)
    x1, h2, afft = out_proj_router(merged, x, p["w_out"].astype(BF16), p["norm_ffn"], p["w_router"],
                                   tm=min(TM_OUT, s))

    cap = CAPACITY_FACTOR * s // N_EXPERTS
    slot_t, cum_t, idx = expert_select(afft, cap=cap, blk=SEL_BLK)
    ye = expert_ffn(idx, h2, p["w_expert_gate"], p["w_expert_up"], p["w_expert_down"], l,
                    cap=cap, pad=COMBINE_ROWS, fc=FFN_CHUNK)
    starts = jnp.concatenate([cum_t[:, ::TM_ROW].T, jnp.full((1, N_EXPERTS), cap, I32)], axis=0)
    return moe_combine(starts, x1, afft.T, slot_t.T, ye, tm=TM_ROW, rows=COMBINE_ROWS)


_LAYER_PARAMS = ("w_in", "ssm_d", "q_gain", "k_gain", "lambda_q1",
                 "lambda_k1", "lambda_q2", "lambda_k2", "subln_gain", "w_out",
                 "norm_mix", "norm_ffn", "w_router")
_STACKED_PARAMS = ("w_expert_gate", "w_expert_up", "w_expert_down", "w_glu", "w_ssm_branch", "w_attn_branch")


def kernel(x, w_in, ssm_a_re, ssm_a_im, ssm_log_dt, ssm_b_re, ssm_b_im, ssm_c_re, ssm_c_im, ssm_d, w_glu, w_ssm_branch, q_gain, k_gain, lambda_q1, lambda_k1, lambda_q2, lambda_k2, subln_gain, w_attn_branch, rel_bias, w_out, norm_mix, norm_ffn, w_router, w_expert_gate, w_expert_up, w_expert_down):
    args = dict(locals())
    b = x.shape[0]
    bias = bias_tiles(rel_bias, tk=ATTN_TILE, tq=ATTN_TILE)
    far_bias = attn_bias_consts(rel_bias)
    s5_ops = jax.vmap(s5_operators)(ssm_a_re, ssm_a_im, ssm_log_dt, ssm_b_re, ssm_b_im, ssm_c_re, ssm_c_im)
    outs = []
    for bi in range(b):
        xb = x[bi].astype(F32)
        for l in range(DEPTH):
            p = {k: args[k][l] for k in _LAYER_PARAMS}
            p.update({k: args[k] for k in _STACKED_PARAMS})
            p["s5_ops"] = s5_ops
            xb = _layer(xb, l, p, bias, far_bias)
        outs.append(xb)
    return jnp.stack(outs, axis=0).astype(x.dtype)
```

```python
import functools
import math

import jax
import jax.numpy as jnp
import numpy as np
from jax import lax
from jax.experimental import pallas as pl
from jax.experimental.pallas import tpu as pltpu

F32 = jnp.float32
BF16 = jnp.bfloat16
I32 = jnp.int32

D_MODEL = 2048
DEPTH = 2
SSM_WIDTH = D_MODEL // 2
SSM_GROUP = 16
SSM_GROUPS = SSM_WIDTH // SSM_GROUP
SSM_STATE = 64
ATTN_HEADS = 8
QK_DIM = 64
V_DIM = 2 * QK_DIM
ATTN_WIDTH = ATTN_HEADS * V_DIM
QK_COLS = ATTN_HEADS * 2 * QK_DIM
REL_BUCKETS = 32
REL_MAX_DIST = 128
N_EXPERTS = 16
CAPACITY_FACTOR = 2
EXPERT_FF = D_MODEL
NORM_EPS = 1e-6
IN_COLS = SSM_WIDTH + 2 * QK_COLS + ATTN_WIDTH + 2 * D_MODEL
COL_Q = SSM_WIDTH
COL_K = COL_Q + QK_COLS
COL_V = COL_K + QK_COLS
COL_GS = COL_V + ATTN_WIDTH
COL_GA = COL_GS + D_MODEL

V7X_LANES = 128
F32_SUBLANES = 8
V7X_VMEM_BYTES = 64 * 1024 * 1024
VMEM_LIMIT = 56 * 1024 * 1024
LOG2E = 1.4426950408889634

S5_CHUNK = 16
S5_GB = 8
NEG_BIG = -1e30


def _cp(*sem):
    return pltpu.CompilerParams(dimension_semantics=sem, vmem_limit_bytes=VMEM_LIMIT)


def _norm_matmul_kernel(x_ref, g_ref, w_ref, o_ref, xcat_ref, h_ref, u_ref):
    j = pl.program_id(1)

    @pl.when(j == 0)
    def _():
        x = x_ref[...]
        r = lax.rsqrt(jnp.mean(x * x, axis=-1, keepdims=True) + NORM_EPS)
        h_ref[...] = (x * r * g_ref[...]).astype(BF16)

    res = jnp.dot(h_ref[...], w_ref[...].astype(BF16), preferred_element_type=F32)
    o_ref[...] = res.astype(o_ref.dtype)

    @pl.when(j == 0)
    def _():
        cn = u_ref.shape[1] // S5_CHUNK
        for slab in range(u_ref.shape[0]):
            u_ref[slab] = res[:, slab * V7X_LANES:(slab + 1) * V7X_LANES]
            for t in range(S5_CHUNK):
                piece = u_ref[slab, pl.ds(t, cn, stride=S5_CHUNK), :]
                xcat_ref[slab, :, t * V7X_LANES:(t + 1) * V7X_LANES] = piece.astype(xcat_ref.dtype)


def norm_matmul(x, gain, w, layer, *, tm, tn):
    s, d = x.shape
    n = w.shape[2]
    assert tn == SSM_WIDTH and tm % (S5_CHUNK * F32_SUBLANES) == 0
    nslab = SSM_WIDTH // V7X_LANES
    return pl.pallas_call(
        _norm_matmul_kernel,
        out_shape=(jax.ShapeDtypeStruct((s, n), BF16),
                   jax.ShapeDtypeStruct((nslab, s // S5_CHUNK, S5_CHUNK * V7X_LANES), BF16)),
        grid=(s // tm, n // tn),
        in_specs=[pl.BlockSpec((tm, d), lambda i, j: (i, 0)),
                  pl.BlockSpec((1, d), lambda i, j: (0, 0)),
                  pl.BlockSpec((None, d, tn), lambda i, j: (layer, 0, j))],
        out_specs=(pl.BlockSpec((tm, tn), lambda i, j: (i, j)),
                   pl.BlockSpec((nslab, tm // S5_CHUNK, S5_CHUNK * V7X_LANES), lambda i, j: (0, i, 0))),
        scratch_shapes=[pltpu.VMEM((tm, d), BF16), pltpu.VMEM((nslab, tm, V7X_LANES), F32)],
        compiler_params=_cp("parallel", "arbitrary"),
        name="norm_in_proj",
    )(x, gain.reshape(1, d).astype(F32), w)


def _attn_prep_kernel(q_ref, k_ref, v_ref, gq_ref, gk_ref, qo_ref, ko_ref, vo_ref, kn_ref):
    lane = lax.broadcasted_iota(I32, (1, V7X_LANES), 1)
    lo_mask = lane < QK_DIM

    def norm(src_ref, g_ref, a):
        x = src_ref[:, a * V7X_LANES:(a + 1) * V7X_LANES].astype(F32)
        ss = x * x
        lo = jnp.sum(jnp.where(lo_mask, ss, 0.0), axis=-1, keepdims=True)
        hi = jnp.sum(jnp.where(lo_mask, 0.0, ss), axis=-1, keepdims=True)
        ms = jnp.where(lo_mask, lo, hi) * (1.0 / QK_DIM)
        return x * lax.rsqrt(ms + NORM_EPS) * g_ref[...]

    for a in range(ATTN_HEADS):
        qo_ref[:, a * V7X_LANES:(a + 1) * V7X_LANES] = norm(q_ref, gq_ref, a).astype(qo_ref.dtype)
        kb = norm(k_ref, gk_ref, a).astype(ko_ref.dtype)
        ko_ref[a, 0] = kb
        kk = kb.astype(F32) * kb.astype(F32)
        n_lo = jnp.max(jnp.sum(jnp.where(lo_mask, kk, 0.0), axis=-1, keepdims=True), axis=0, keepdims=True)
        n_hi = jnp.max(jnp.sum(jnp.where(lo_mask, 0.0, kk), axis=-1, keepdims=True), axis=0, keepdims=True)
        kn_ref[0, a:a + 1, :] = jnp.where(lane == 0, n_lo, jnp.where(lane == 1, n_hi, 0.0))
        v = v_ref[:, a * V7X_LANES:(a + 1) * V7X_LANES].astype(F32)
        vo_ref[a, 0] = v.T.astype(vo_ref.dtype)


def attn_prep(proj, q_gain, k_gain, *, tk):
    s = proj.shape[0]
    nk = s // tk
    gq = (jnp.tile(q_gain.astype(F32), 2) * (QK_DIM ** -0.5 * LOG2E)).reshape(1, V7X_LANES)
    gk = jnp.tile(k_gain.astype(F32), 2).reshape(1, V7X_LANES)
    cq, ck, cv = COL_Q // QK_COLS, COL_K // QK_COLS, COL_V // ATTN_WIDTH
    qn, k4, vt4, kn2 = pl.pallas_call(
        _attn_prep_kernel,
        out_shape=(jax.ShapeDtypeStruct((s, QK_COLS), BF16),
                   jax.ShapeDtypeStruct((ATTN_HEADS, nk, tk, V_DIM), BF16),
                   jax.ShapeDtypeStruct((ATTN_HEADS, nk, V_DIM, tk), BF16),
                   jax.ShapeDtypeStruct((nk, ATTN_HEADS, V7X_LANES), F32)),
        grid=(nk,),
        in_specs=[pl.BlockSpec((tk, QK_COLS), lambda i: (i, cq)),
                  pl.BlockSpec((tk, QK_COLS), lambda i: (i, ck)),
                  pl.BlockSpec((tk, ATTN_WIDTH), lambda i: (i, cv)),
                  pl.BlockSpec((1, V7X_LANES), lambda i: (0, 0)),
                  pl.BlockSpec((1, V7X_LANES), lambda i: (0, 0))],
        out_specs=(pl.BlockSpec((tk, QK_COLS), lambda i: (i, 0)),
                   pl.BlockSpec((ATTN_HEADS, 1, tk, V_DIM), lambda i: (0, i, 0, 0)),
                   pl.BlockSpec((ATTN_HEADS, 1, V_DIM, tk), lambda i: (0, i, 0, 0)),
                   pl.BlockSpec((1, ATTN_HEADS, V7X_LANES), lambda i: (i, 0, 0))),
        compiler_params=_cp("parallel"),
        name="attn_prep",
    )(proj, proj, proj, gq, gk)
    kmax = jnp.sqrt(jnp.max(kn2[:, :, :2], axis=0))
    return qn, k4, vt4, kmax


ATTN_NEAR = 3
ATTN_BIAS_TILES = 5
ATTN_UNROLL = 6


def _bias_tile_kernel(rb_ref, o_ref, *, tk, tq):
    h = pl.program_id(0)
    d = pl.program_id(1)
    w = tk + tq
    m = lax.broadcasted_iota(I32, (F32_SUBLANES, w), 1)
    rel = (d - ATTN_BIAS_TILES // 2) * tk + (tk - 1) - m
    half = REL_BUCKETS // 2
    exact = half // 2
    side = jnp.where(rel > 0, half, 0).astype(I32)
    n = jnp.abs(rel)
    nf = jnp.maximum(n, 1).astype(F32)
    large = exact + (jnp.log(nf / exact) / math.log(REL_MAX_DIST / exact) * (half - exact)).astype(I32)
    large = jnp.minimum(large, half - 1)
    bucket = side + jnp.where(n < exact, n, large).astype(I32)
    val = jnp.zeros((F32_SUBLANES, w), F32)
    for b in range(REL_BUCKETS):
        val = jnp.where(bucket == b, rb_ref[b, h], val)
    table = jnp.broadcast_to(val[0:1, :] * LOG2E, (tk, w))
    o_ref[0, 0] = pltpu.roll(table, w - tk + 1, 1, stride=1, stride_axis=0)[:, :tq]


def bias_tiles(rel_bias, *, tk, tq):
    assert tk == tq and (tk + tq) & (tk + tq - 1) == 0
    return pl.pallas_call(
        functools.partial(_bias_tile_kernel, tk=tk, tq=tq),
        out_shape=jax.ShapeDtypeStruct((ATTN_HEADS, ATTN_BIAS_TILES, tk, tq), F32),
        grid=(ATTN_HEADS, ATTN_BIAS_TILES),
        in_specs=[pl.BlockSpec(memory_space=pltpu.SMEM)],
        out_specs=pl.BlockSpec((1, 1, tk, tq), lambda h, d: (h, d, 0, 0)),
        compiler_params=_cp("parallel", "parallel"),
        name="t5_bias_tiles",
    )(rel_bias.astype(F32))


FAST_MIN_SUM = 2.0 ** -90


def _attn_kernel(lam_ref, far_ref, kmax_ref, q_ref, k_ref, vt_ref, bias_ref, g_ref, o_ref,
                 s_ref, cm_ref, m_ref, l_ref, acc_ref, p_ref, *, nk, tk, tq, out_scale):
    h = pl.program_id(0)
    i = pl.program_id(1)
    lane = lax.broadcasted_iota(I32, (1, V7X_LANES), 1)
    q = q_ref[...]
    zero = jnp.zeros_like(q)
    qmaps = (jnp.where(lane < QK_DIM, q, zero), jnp.where(lane < QK_DIM, zero, q))

    n0 = jnp.clip(i - 1, 0, nk - ATTN_NEAR)
    nfar = nk - ATTN_NEAR

    def far_tile(t):
        return jnp.where(t < n0, t, t + ATTN_NEAR)

    def far_const(j):
        return jnp.where(j < i, far_ref[h, 0], far_ref[h, 1])

    def near_bias(j):
        return bias_ref[0, j - i + ATTN_BIAS_TILES // 2]

    qf = q.astype(F32)
    row8 = lax.broadcasted_iota(I32, (F32_SUBLANES, V7X_LANES), 0)
    lane8 = lax.broadcasted_iota(I32, (F32_SUBLANES, V7X_LANES), 1)
    pick = jnp.where(jnp.logical_or(jnp.logical_and(row8 == 0, lane8 < QK_DIM),
                                    jnp.logical_and(row8 == 1, lane8 >= QK_DIM)), 1.0, 0.0)
    qn2 = lax.dot_general(pick, qf * qf, (((1,), (1,)), ((), ())), preferred_element_type=F32,
                          precision=lax.Precision.HIGHEST)
    shift = [jnp.sqrt(qn2[c:c + 1, :]) * kmax_ref[h, c] + far_ref[h, 2] for c in range(2)]
    l_ref[...] = jnp.zeros(l_ref.shape, F32)
    acc_ref[...] = jnp.zeros(acc_ref.shape, F32)

    def fast_probs(j, slot, bias_tile, bias_const):
        kt = k_ref[0, j]
        for c in range(2):
            s = lax.dot_general(kt, qmaps[c], (((1,), (1,)), ((), ())), preferred_element_type=F32)
            if bias_tile is not None:
                s = s + bias_tile
            p = jnp.exp2(s - (shift[c] - bias_const))
            l_ref[c:c + 1, :] += jnp.sum(p, axis=0, keepdims=True)
            p_ref[slot, c] = p.astype(BF16)

    def fast_pv(j, slot):
        vt = vt_ref[0, j]
        for c in range(2):
            acc_ref[c] += jnp.dot(vt, p_ref[slot, c], preferred_element_type=F32)

    fast_probs(n0, 0, near_bias(n0), 0.0)
    for w in range(1, ATTN_NEAR):
        fast_probs(n0 + w, w % 2, near_bias(n0 + w), 0.0)
        fast_pv(n0 + w - 1, (w - 1) % 2)
    j0 = far_tile(0)
    fast_probs(j0, ATTN_NEAR % 2, None, far_const(j0))
    fast_pv(n0 + ATTN_NEAR - 1, (ATTN_NEAR - 1) % 2)

    def fast_group(u, jprev):
        for w in range(ATTN_UNROLL):
            jn = far_tile(ATTN_UNROLL * u + w + 1)
            fast_probs(jn, (ATTN_NEAR + 1 + w) % 2, None, far_const(jn))
            fast_pv(jprev, (ATTN_NEAR + w) % 2)
            jprev = jn
        return jprev

    jl = lax.fori_loop(0, (nfar - 1) // ATTN_UNROLL, fast_group, j0)
    fast_pv(jl, (ATTN_NEAR + nfar - 1) % 2)

    lmin = jnp.min(jnp.minimum(l_ref[0:1, :], l_ref[1:2, :]))

    @pl.when(jnp.logical_not(lmin >= FAST_MIN_SUM))
    def _():
        _attn_running_max(h, i, n0, nfar, far_tile, far_const, near_bias, qmaps, k_ref, vt_ref,
                          s_ref, cm_ref, m_ref, l_ref, acc_ref, tq=tq)

    o1 = acc_ref[0] / l_ref[0:1, :]
    o2 = acc_ref[1] / l_ref[1:2, :]
    o = o1 - lam_ref[0] * o2
    r = lax.rsqrt(jnp.mean(o * o, axis=0, keepdims=True) + NORM_EPS)
    o = o * r * g_ref[...] * out_scale
    o_ref[...] = o.T.astype(o_ref.dtype)


def _attn_running_max(h, i, n0, nfar, far_tile, far_const, near_bias, qmaps, k_ref, vt_ref,
                      s_ref, cm_ref, m_ref, l_ref, acc_ref, *, tq):
    m_ref[...] = jnp.full(m_ref.shape, NEG_BIG, F32)
    l_ref[...] = jnp.zeros(l_ref.shape, F32)
    acc_ref[...] = jnp.zeros(acc_ref.shape, F32)

    def scores(j, slot, bias_tile, bias_const):
        kt = k_ref[0, j]
        for c in range(2):
            s = lax.dot_general(kt, qmaps[c], (((1,), (1,)), ((), ())),
                                preferred_element_type=F32)
            if bias_tile is not None:
                s = s + bias_tile
            s_ref[slot, c] = s
            cm_ref[slot, c:c + 1, :] = jnp.max(s, axis=0, keepdims=True) + bias_const
        cm_ref[slot, 2:3, :] = jnp.zeros((1, tq), F32) + bias_const

    def absorb(j, slot):
        vt = vt_ref[0, j]
        cb = cm_ref[slot, 2:3, :]
        for c in range(2):
            m_old = m_ref[c:c + 1, :]
            m_new = jnp.maximum(m_old, cm_ref[slot, c:c + 1, :])
            alpha = jnp.exp2(m_old - m_new)
            p = jnp.exp2(s_ref[slot, c] - (m_new - cb))
            l_ref[c:c + 1, :] = alpha * l_ref[c:c + 1, :] + jnp.sum(p, axis=0, keepdims=True)
            acc_ref[c] = alpha * acc_ref[c] + jnp.dot(vt, p.astype(BF16), preferred_element_type=F32)
            m_ref[c:c + 1, :] = m_new

    scores(n0, 0, near_bias(n0), 0.0)
    scores(n0 + 1, 1, near_bias(n0 + 1), 0.0)
    absorb(n0, 0)
    scores(n0 + 2, 0, near_bias(n0 + 2), 0.0)
    absorb(n0 + 1, 1)
    j0 = far_tile(0)
    scores(j0, 1, None, far_const(j0))
    absorb(n0 + 2, 0)

    def group(u, jprev):
        for w in range(ATTN_UNROLL):
            jn = far_tile(ATTN_UNROLL * u + w + 1)
            scores(jn, w % 2, None, far_const(jn))
            absorb(jprev, (w + 1) % 2)
            jprev = jn
        return jprev

    jlast = lax.fori_loop(0, (nfar - 1) // ATTN_UNROLL, group, j0)
    absorb(jlast, 1)


def attn_bias_consts(rel_bias):
    half = REL_BUCKETS // 2
    rb = rel_bias.astype(F32) * LOG2E
    return jnp.stack([rb[half - 1], rb[REL_BUCKETS - 1], jnp.max(rb, axis=0)], axis=1)


def diff_attention(qn, k4, vt4, bias, far_bias, kmax, lam, subln_gain, *, lam_init, tq, tk):
    s = qn.shape[0]
    nk = s // tk
    assert tq == tk and nk > ATTN_NEAR and (nk - ATTN_NEAR - 1) % ATTN_UNROLL == 0
    kern = functools.partial(_attn_kernel, nk=nk, tk=tk, tq=tq, out_scale=1.0 - lam_init)
    return pl.pallas_call(
        kern,
        out_shape=jax.ShapeDtypeStruct((s, ATTN_WIDTH), BF16),
        grid=(ATTN_HEADS, s // tq),
        in_specs=[pl.BlockSpec(memory_space=pltpu.SMEM),
                  pl.BlockSpec(memory_space=pltpu.SMEM),
                  pl.BlockSpec(memory_space=pltpu.SMEM),
                  pl.BlockSpec((tq, V_DIM), lambda h, i: (i, h)),
                  pl.BlockSpec((1, nk, tk, V_DIM), lambda h, i: (h, 0, 0, 0)),
                  pl.BlockSpec((1, nk, V_DIM, tk), lambda h, i: (h, 0, 0, 0)),
                  pl.BlockSpec((1, ATTN_BIAS_TILES, tk, tq), lambda h, i: (h, 0, 0, 0)),
                  pl.BlockSpec((V_DIM, 1), lambda h, i: (0, 0))],
        out_specs=pl.BlockSpec((tq, V_DIM), lambda h, i: (i, h)),
        scratch_shapes=[pltpu.VMEM((2, 2, tk, tq), F32),
                        pltpu.VMEM((2, F32_SUBLANES, tq), F32),
                        pltpu.VMEM((F32_SUBLANES, tq), F32),
                        pltpu.VMEM((F32_SUBLANES, tq), F32),
                        pltpu.VMEM((2, V_DIM, tq), F32),
                        pltpu.VMEM((2, 2, tk, tq), BF16)],
        compiler_params=_cp("parallel", "arbitrary"),
        name="diff_attention",
    )(lam.reshape(1).astype(F32), far_bias, kmax, qn, k4, vt4, bias,
      subln_gain.reshape(V_DIM, 1).astype(F32))


def s5_operators(a_re, a_im, log_dt, b_re, b_im, c_re, c_im):
    t_len, hp = S5_CHUNK, lax.Precision.HIGHEST
    a_re, a_im = a_re.astype(F32), a_im.astype(F32)
    dt = jnp.exp(log_dt.astype(F32))[..., None]
    steps = jnp.arange(t_len + 1, dtype=F32)[:, None, None, None]
    mag = jnp.exp(a_re * dt * steps)
    ang = a_im * dt * steps
    pw_re, pw_im = mag * jnp.cos(ang), mag * jnp.sin(ang)
    den = a_re * a_re + a_im * a_im
    nr, ni = pw_re[1] - 1.0, pw_im[1]
    coef_re = ((nr * a_re + ni * a_im) / den)[..., None]
    coef_im = ((ni * a_re - nr * a_im) / den)[..., None]
    b_re, b_im = b_re.astype(F32), b_im.astype(F32)
    bb_re = coef_re * b_re - coef_im * b_im
    bb_im = coef_re * b_im + coef_im * b_re
    c_re, c_im = c_re.astype(F32), c_im.astype(F32)

    g, n_st, p_ch = a_re.shape[1], SSM_STATE, SSM_GROUP
    tp = t_len * p_ch
    pwt_re = jnp.transpose(pw_re, (1, 2, 3, 0))
    pwt_im = jnp.transpose(pw_im, (1, 2, 3, 0))
    ct_re = jnp.transpose(c_re, (0, 1, 3, 2))
    ct_im = jnp.transpose(c_im, (0, 1, 3, 2))
    bbt_re = jnp.transpose(bb_re, (0, 1, 3, 2))
    bbt_im = jnp.transpose(bb_im, (0, 1, 3, 2))
    cp_re = ct_re[:, :, :, None, :] * pwt_re[..., None] - ct_im[:, :, :, None, :] * pwt_im[..., None]
    cp_im = ct_re[:, :, :, None, :] * pwt_im[..., None] + ct_im[:, :, :, None, :] * pwt_re[..., None]

    zlag = jnp.zeros((g, n_st, t_len - 1, p_ch), F32)

    def lagged(cp):
        return (jnp.concatenate([zlag, cp[0, :, :, :t_len]], axis=2),
                jnp.concatenate([jnp.flip(cp[1, :, :, :t_len], axis=2), zlag], axis=2))

    rf_re, rb_re = lagged(cp_re)
    rf_im, rb_im = lagged(cp_im)
    r_cat = jnp.concatenate([rf_re, rf_im, rb_re, rb_im], axis=1).reshape(g, 4 * n_st, (2 * t_len - 1) * p_ch)
    a_cat = jnp.concatenate([bbt_re[0], -bbt_im[0], bbt_re[1], -bbt_im[1]], axis=-1)
    kp = jnp.einsum('gpk,gkx->gpx', a_cat, r_cat, precision=hp)
    toep = jnp.stack([kp[:, :, (t_len - 1 - j) * p_ch:(t_len - 1 - j) * p_ch + tp] for j in range(t_len)],
                     axis=1).reshape(g, tp, tp).astype(BF16)

    def seg_powers(d, reverse):
        pr = jnp.transpose(pw_re[:t_len, d], (1, 0, 2))
        pi = jnp.transpose(pw_im[:t_len, d], (1, 0, 2))
        if reverse:
            pr, pi = jnp.flip(pr, axis=1), jnp.flip(pi, axis=1)
        return pr, pi

    prf, pif = seg_powers(0, True)
    prb, pib = seg_powers(1, False)
    pa = jnp.concatenate([prf, pif, pif, prf, prb, pib, pib, prb], axis=-1)
    pb = jnp.concatenate([-pif, prf, prf, -pif, -pib, prb, prb, -pib], axis=-1)
    br = jnp.concatenate([bbt_re[0]] * 4 + [bbt_re[1]] * 4, axis=-1)
    bi = jnp.concatenate([bbt_im[0]] * 4 + [bbt_im[1]] * 4, axis=-1)
    smap = (br[:, None] * pa[:, :, None, :] + bi[:, None] * pb[:, :, None, :]).reshape(g, tp, 8 * n_st)
    smap = smap.astype(BF16)

    def out_map(d, reverse):
        wr, wi = cp_re[d][:, :, 1:t_len + 1], cp_im[d][:, :, 1:t_len + 1]
        if reverse:
            wr, wi = jnp.flip(wr, axis=2), jnp.flip(wi, axis=2)
        return [wr.reshape(g, n_st, tp), -wi.reshape(g, n_st, tp)]

    mc = jnp.concatenate(out_map(0, False) + out_map(1, True), axis=1).astype(BF16)

    def carry(d):
        ar, ai = pw_re[t_len, d], pw_im[t_len, d]
        return [jnp.concatenate([ar, ar], -1), jnp.concatenate([-ai, ai], -1),
                jnp.concatenate([ai, -ai], -1)]

    coef = jnp.stack(carry(0) + carry(1), axis=0)
    return toep, smap, mc, coef


def _gelu_tanh(x):
    return 0.5 * x * (1.0 + jnp.tanh(math.sqrt(2.0 / math.pi) * (x + 0.044715 * (x * x * x))))


def _s5_kernel(x_ref, sel_ref, toep_ref, smap_ref, mc_ref, coef_ref, d_ref, y_ref,
               u_ref, ef_ref, efs_ref, eb_ref, ebs_ref, *, cn):
    gb = toep_ref.shape[0]
    tp = toep_ref.shape[1]
    n2 = 2 * SSM_STATE
    xcat = x_ref[0]

    def sel(g):
        off = (gb - 1 - g) * SSM_GROUP
        return sel_ref[off:off + xcat.shape[1], :]

    for g in range(gb):
        u_ref[g] = jnp.dot(xcat, sel(g), preferred_element_type=F32).astype(BF16)
        e = jnp.dot(u_ref[g], smap_ref[g], preferred_element_type=F32)
        for r, ref in enumerate((ef_ref, efs_ref, eb_ref, ebs_ref)):
            ref[pl.ds(g, cn, stride=gb), :] = e[:, r * n2:(r + 1) * n2]

    cf, cfs, cfw = coef_ref[0], coef_ref[1], coef_ref[2]
    cb, cbs, cbw = coef_ref[3], coef_ref[4], coef_ref[5]

    def step(c, carry):
        s, sw, r, rw = carry
        fo = pl.multiple_of(c * gb, gb)
        bo = pl.multiple_of((cn - 1 - c) * gb, gb)
        e, es = ef_ref[pl.ds(fo, gb), :], efs_ref[pl.ds(fo, gb), :]
        ef_ref[pl.ds(fo, gb), :] = s
        s, sw = cf * s + cfs * sw + e, cf * sw + cfw * s + es
        e, es = eb_ref[pl.ds(bo, gb), :], ebs_ref[pl.ds(bo, gb), :]
        eb_ref[pl.ds(bo, gb), :] = r
        r, rw = cb * r + cbs * rw + e, cb * rw + cbw * r + es
        return s, sw, r, rw

    z = jnp.zeros((gb, n2), F32)
    lax.fori_loop(0, cn, step, (z, z, z, z))

    ycat = None
    for g in range(gb):
        u = u_ref[g]
        st = jnp.concatenate([ef_ref[pl.ds(g, cn, stride=gb), :], eb_ref[pl.ds(g, cn, stride=gb), :]],
                             axis=1).astype(BF16)
        y = (jnp.dot(u, toep_ref[g], preferred_element_type=F32)
             + jnp.dot(st, mc_ref[g], preferred_element_type=F32)
             + u.astype(F32) * d_ref[g])
        placed = lax.dot_general(_gelu_tanh(y).astype(BF16), sel(g), (((1,), (1,)), ((), ())),
                                 preferred_element_type=F32)
        ycat = placed if ycat is None else ycat + placed
    y_ref[0] = ycat.astype(y_ref.dtype)


def s5_lane_selector():
    gb, p, t_len = S5_GB, SSM_GROUP, S5_CHUNK
    r = jnp.arange(t_len * V7X_LANES + (gb - 1) * p)[:, None] - (gb - 1) * p
    c = jnp.arange(t_len * p)[None, :]
    hit = (r >= 0) & (r // V7X_LANES == c // p) & ((r % V7X_LANES) // p == 0) & (r % p == c % p)
    return hit.astype(BF16)


def s5_mixer(xcat, ops, ssm_d, layer):
    toep, smap, mc, coef = ops
    g, p, t_len = SSM_GROUPS, SSM_GROUP, S5_CHUNK
    nslab, cn, _ = xcat.shape
    tp = t_len * p
    gb = S5_GB
    assert gb * p == V7X_LANES and nslab == g // gb
    sel = s5_lane_selector()
    dsk = jnp.tile(ssm_d.astype(F32).reshape(g, 1, p), (1, 1, t_len))
    ycat = pl.pallas_call(
        functools.partial(_s5_kernel, cn=cn),
        out_shape=jax.ShapeDtypeStruct((nslab, cn, t_len * V7X_LANES), BF16),
        grid=(nslab,),
        in_specs=[pl.BlockSpec((1, cn, t_len * V7X_LANES), lambda i: (i, 0, 0)),
                  pl.BlockSpec(sel.shape, lambda i: (0, 0)),
                  pl.BlockSpec((None, gb, tp, tp), lambda i: (layer, i, 0, 0)),
                  pl.BlockSpec((None, gb, tp, smap.shape[3]), lambda i: (layer, i, 0, 0)),
                  pl.BlockSpec((None, gb, mc.shape[2], tp), lambda i: (layer, i, 0, 0)),
                  pl.BlockSpec((None, 6, gb, 2 * SSM_STATE), lambda i: (layer, 0, i, 0)),
                  pl.BlockSpec((gb, 1, tp), lambda i: (i, 0, 0))],
        out_specs=pl.BlockSpec((1, cn, t_len * V7X_LANES), lambda i: (i, 0, 0)),
        scratch_shapes=[pltpu.VMEM((gb, cn, tp), BF16)]
        + [pltpu.VMEM((cn * gb, 2 * SSM_STATE), F32) for _ in range(4)],
        compiler_params=_cp("parallel"),
        name="s5_chunked_scan",
    )(xcat, sel, toep, smap, mc, coef, dsk)
    return ycat


def _sigmoid(x):
    return 1.0 / (1.0 + jnp.exp(-x))


def _glu_kernel(ycat_ref, w_ref, o_ref, y_ref):
    cn = ycat_ref.shape[1]
    for slab in range(ycat_ref.shape[0]):
        for t in range(S5_CHUNK):
            piece = ycat_ref[slab, :, t * V7X_LANES:(t + 1) * V7X_LANES].astype(F32)
            y_ref[slab, pl.ds(t, cn, stride=S5_CHUNK), :] = piece
    y = jnp.concatenate([y_ref[slab] for slab in range(ycat_ref.shape[0])], axis=1)
    z = jnp.dot(y.astype(BF16), w_ref[0].astype(BF16), preferred_element_type=F32)
    o_ref[...] = (y * _sigmoid(z)).astype(o_ref.dtype)


def glu(ycat, w, layer, *, tm):
    nslab, cn, _ = ycat.shape
    s, d = cn * S5_CHUNK, nslab * V7X_LANES
    return pl.pallas_call(
        _glu_kernel,
        out_shape=jax.ShapeDtypeStruct((s, d), BF16),
        grid=(s // tm,),
        in_specs=[pl.BlockSpec((nslab, tm // S5_CHUNK, S5_CHUNK * V7X_LANES), lambda i: (0, i, 0)),
                  pl.BlockSpec((1, d, d), lambda i: (layer, 0, 0))],
        out_specs=pl.BlockSpec((tm, d), lambda i: (i, 0)),
        scratch_shapes=[pltpu.VMEM((nslab, tm, V7X_LANES), F32)],
        compiler_params=_cp("parallel"),
        name="half_glu",
    )(ycat, w)


def _merge_kernel(yg_ref, ao_ref, gs_ref, ga_ref, ws_ref, wa_ref, o_ref):
    a = jnp.dot(yg_ref[...], ws_ref[0].astype(BF16), preferred_element_type=F32)
    b = jnp.dot(ao_ref[...], wa_ref[0].astype(BF16), preferred_element_type=F32)
    o = _sigmoid(gs_ref[...].astype(F32)) * a + _sigmoid(ga_ref[...].astype(F32)) * b
    o_ref[...] = o.astype(o_ref.dtype)


def gated_merge(yg, ao, proj, ws, wa, layer, *, tm, tn):
    s, k = yg.shape
    n = ws.shape[2]
    cs, ca = COL_GS // tn, COL_GA // tn
    return pl.pallas_call(
        _merge_kernel,
        out_shape=jax.ShapeDtypeStruct((s, n), BF16),
        grid=(s // tm, n // tn),
        in_specs=[pl.BlockSpec((tm, k), lambda i, j: (i, 0)),
                  pl.BlockSpec((tm, k), lambda i, j: (i, 0)),
                  pl.BlockSpec((tm, tn), lambda i, j: (i, cs + j)),
                  pl.BlockSpec((tm, tn), lambda i, j: (i, ca + j)),
                  pl.BlockSpec((1, k, tn), lambda i, j: (layer, 0, j)),
                  pl.BlockSpec((1, k, tn), lambda i, j: (layer, 0, j))],
        out_specs=pl.BlockSpec((tm, tn), lambda i, j: (i, j)),
        compiler_params=_cp("parallel", "arbitrary"),
        name="gated_merge",
    )(yg, ao, proj, proj, ws, wa)


def _out_router_kernel(m_ref, x_ref, w_ref, g_ref, wrt_ref, xo_ref, h_ref, afft_ref):
    x1 = x_ref[...] + jnp.dot(m_ref[...], w_ref[...], preferred_element_type=F32)
    xo_ref[...] = x1
    r = lax.rsqrt(jnp.mean(x1 * x1, axis=-1, keepdims=True) + NORM_EPS)
    h = x1 * r * g_ref[...]
    h_ref[...] = h
    lgt = lax.dot_general(wrt_ref[...], h, (((1,), (1,)), ((), ())),
                          preferred_element_type=F32, precision=lax.Precision.HIGHEST)
    et = jnp.exp(lgt - jnp.max(lgt, axis=0, keepdims=True))
    afft_ref[...] = et / jnp.sum(et, axis=0, keepdims=True)


def out_proj_router(merged, x, w_out, gain, w_router, *, tm):
    s, d = x.shape
    e = w_router.shape[1]
    return pl.pallas_call(
        _out_router_kernel,
        out_shape=(jax.ShapeDtypeStruct((s, d), F32), jax.ShapeDtypeStruct((s, d), F32),
                   jax.ShapeDtypeStruct((e, s), F32)),
        grid=(s // tm,),
        in_specs=[pl.BlockSpec((tm, d), lambda i: (i, 0)),
                  pl.BlockSpec((tm, d), lambda i: (i, 0)),
                  pl.BlockSpec((d, d), lambda i: (0, 0)),
                  pl.BlockSpec((1, d), lambda i: (0, 0)),
                  pl.BlockSpec((e, d), lambda i: (0, 0))],
        out_specs=(pl.BlockSpec((tm, d), lambda i: (i, 0)),
                   pl.BlockSpec((tm, d), lambda i: (i, 0)),
                   pl.BlockSpec((e, tm), lambda i: (0, i))),
        compiler_params=_cp("parallel"),
        name="out_proj_router",
    )(merged, x, w_out, gain.reshape(1, d).astype(F32), w_router.astype(F32).T)


SLOT_LO_BITS = 6
SLOT_LO = 1 << SLOT_LO_BITS


def _select_kernel(afft_ref, slot_ref, cum_ref, idx_ref, acc_ref, *, s, cap, blk):
    ne = afft_ref.shape[0]

    def bit_body(b, thr):
        cand = thr | jnp.left_shift(jnp.ones((ne, 1), I32), 30 - b)
        keys = pltpu.bitcast(afft_ref[...], I32)
        cnt = jnp.sum((keys >= cand).astype(I32), axis=1, keepdims=True)
        return jnp.where(cnt >= cap, cand, thr)

    thr = lax.fori_loop(0, 31, bit_body, jnp.zeros((ne, 1), I32))
    keys = pltpu.bitcast(afft_ref[...], I32)
    need = cap - jnp.sum((keys > thr).astype(I32), axis=1, keepdims=True)

    ri = lax.broadcasted_iota(I32, (blk, blk), 0)
    ci = lax.broadcasted_iota(I32, (blk, blk), 1)
    upper = jnp.where(ri < ci, 1.0, 0.0).astype(BF16)
    na = cap // SLOT_LO
    acol = lax.broadcasted_iota(I32, (na, 1), 0)
    bcol = lax.broadcasted_iota(I32, (SLOT_LO, 1), 0)
    tlane = lax.broadcasted_iota(I32, (1, blk), 1)
    acc_ref[...] = jnp.zeros(acc_ref.shape, F32)

    def blk_body(b, carry):
        ceq, csel = carry
        off = pl.multiple_of(b * blk, blk)
        kb = pltpu.bitcast(afft_ref[:, pl.ds(off, blk)], I32)
        gt = kb > thr
        eq = kb == thr
        eqf = jnp.where(eq, 1.0, 0.0)
        rank_eq = jnp.dot(eqf.astype(BF16), upper, preferred_element_type=F32) + ceq
        sel = jnp.logical_or(gt, jnp.logical_and(eq, rank_eq < need.astype(F32)))
        self_ = jnp.where(sel, 1.0, 0.0)
        cum = jnp.dot(self_.astype(BF16), upper, preferred_element_type=F32) + csel
        cum_i = cum.astype(I32)
        cum_ref[:, pl.ds(off, blk)] = cum_i
        slot = jnp.where(sel, cum_i, -1)
        slot_ref[:, pl.ds(off, blk)] = slot
        tok = off + tlane
        hi = (tok // V7X_LANES).astype(F32)
        lo = (tok % V7X_LANES).astype(F32)
        for e in range(ne):
            srow = slot[e:e + 1, :]
            in_a = lax.shift_right_arithmetic(srow, SLOT_LO_BITS) == acol
            lhs = jnp.concatenate([jnp.where(in_a, hi, 0.0), jnp.where(in_a, lo, 0.0)],
                                  axis=0).astype(BF16)
            rhs = jnp.where((srow & (SLOT_LO - 1)) == bcol, 1.0, 0.0).astype(BF16)
            acc_ref[e] += lax.dot_general(lhs, rhs, (((1,), (1,)), ((), ())), preferred_element_type=F32)
        return (ceq + jnp.sum(eqf, axis=1, keepdims=True), csel + jnp.sum(self_, axis=1, keepdims=True))

    z = jnp.zeros((ne, 1), F32)
    lax.fori_loop(0, s // blk, blk_body, (z, z))
    a = acc_ref[...]
    idx_ref[...] = (a[:, :na, :] * float(V7X_LANES) + a[:, na:, :]).astype(I32)


def expert_select(afft, *, cap, blk):
    ne, s = afft.shape
    assert cap % SLOT_LO == 0
    return pl.pallas_call(
        functools.partial(_select_kernel, s=s, cap=cap, blk=blk),
        out_shape=(jax.ShapeDtypeStruct((ne, s), I32), jax.ShapeDtypeStruct((ne, s), I32),
                   jax.ShapeDtypeStruct((ne, cap // SLOT_LO, SLOT_LO), I32)),
        scratch_shapes=[pltpu.VMEM((ne, 2 * (cap // SLOT_LO), SLOT_LO), F32)],
        compiler_params=pltpu.CompilerParams(vmem_limit_bytes=VMEM_LIMIT),
        name="expert_select",
    )(afft)


GATHER_UNROLL = 8


def _ffn_kernel(idx_ref, h_hbm, wg_ref, wu_ref, wd_ref, y_ref, xg32_ref, xg_ref, sem, *, cap):
    e = pl.program_id(0)
    f = pl.program_id(1)

    def row_copy(r):
        tok = idx_ref[e * cap + r]
        return pltpu.make_async_copy(h_hbm.at[pl.ds(tok, 1)], xg32_ref.at[pl.ds(r, 1)], sem)

    @pl.when(f == 0)
    def _():
        def start(rb, c):
            for w in range(GATHER_UNROLL):
                row_copy(rb * GATHER_UNROLL + w).start()
            return c

        lax.fori_loop(0, cap // GATHER_UNROLL, start, 0)
        y_ref[...] = jnp.zeros(y_ref.shape, F32)
        pltpu.make_async_copy(h_hbm.at[pl.ds(0, cap)], xg32_ref, sem).wait()
        xg_ref[...] = xg32_ref[...].astype(BF16)

    xg = xg_ref[...]
    a = jnp.dot(xg, wg_ref[0, 0].astype(BF16), preferred_element_type=F32)
    b = jnp.dot(xg, wu_ref[0, 0].astype(BF16), preferred_element_type=F32)
    hid = (a * _sigmoid(a) * b).astype(BF16)
    y_ref[0, 0:cap, :] += jnp.dot(hid, wd_ref[0, 0].astype(BF16), preferred_element_type=F32)


def expert_ffn(idx, h2, wg, wu, wd, layer, *, cap, pad, fc):
    _, ne, d, ff = wg.shape
    grid_spec = pltpu.PrefetchScalarGridSpec(
        num_scalar_prefetch=1,
        grid=(ne, ff // fc),
        in_specs=[pl.BlockSpec(memory_space=pl.ANY),
                  pl.BlockSpec((1, 1, d, fc), lambda e, f, idx: (layer, e, 0, f)),
                  pl.BlockSpec((1, 1, d, fc), lambda e, f, idx: (layer, e, 0, f)),
                  pl.BlockSpec((1, 1, fc, d), lambda e, f, idx: (layer, e, f, 0))],
        out_specs=pl.BlockSpec((1, cap + pad, d), lambda e, f, idx: (e, 0, 0)),
        scratch_shapes=[pltpu.VMEM((cap, d), F32), pltpu.VMEM((cap, d), BF16),
                        pltpu.SemaphoreType.DMA(())],
    )
    return pl.pallas_call(
        functools.partial(_ffn_kernel, cap=cap),
        out_shape=jax.ShapeDtypeStruct((ne, cap + pad, d), F32),
        grid_spec=grid_spec,
        compiler_params=_cp("arbitrary", "arbitrary"),
        name="expert_ffn",
    )(idx.reshape(-1), h2, wg, wu, wd)


def _combine_kernel(st_ref, x_ref, aff_ref, slot_ref, y_hbm, o_ref, buf_ref, xbuf_ref, sem, xsem,
                    *, ne, rows, nt):
    t = pl.program_id(0)
    tm = x_ref.shape[0]

    def chunk_start(tt, e, c):
        st8 = (st_ref[tt * ne + e] // F32_SUBLANES) * F32_SUBLANES
        return pl.multiple_of(st8 + c * rows, F32_SUBLANES)

    def first_copy(tt, e, slot):
        return pltpu.make_async_copy(y_hbm.at[e, pl.ds(chunk_start(tt, e, 0), rows)],
                                     buf_ref.at[slot, e], sem.at[slot, e])

    @pl.when(t == 0)
    def _():
        for e in range(ne):
            first_copy(0, e, 0).start()

    @pl.when(t + 1 < nt)
    def _():
        for e in range(ne):
            first_copy(t + 1, e, (t + 1) % 2).start()

    slot = t % 2
    for e in range(ne):
        first_copy(t, e, slot).wait()

    pair = V7X_LANES // rows
    lane = lax.broadcasted_iota(I32, (1, V7X_LANES), 1)
    hi_parts, lo_parts = [], []
    for a in range(ne // pair):
        rel = jnp.zeros((tm, V7X_LANES), I32)
        gate = jnp.zeros((tm, V7X_LANES), F32)
        for b in range(pair):
            e = a * pair + b
            in_e = jnp.logical_and(lane >= b * rows, lane < (b + 1) * rows)
            rel = jnp.where(in_e, slot_ref[:, e:e + 1] - chunk_start(t, e, 0) + b * rows, rel)
            gate = jnp.where(in_e, aff_ref[:, e:e + 1], gate)
        gate = jnp.where(rel == lane, gate, 0.0)
        g_hi = gate.astype(BF16)
        hi_parts.append(g_hi)
        lo_parts.append((gate - g_hi.astype(F32)).astype(BF16))
    rhs = buf_ref[slot].reshape(ne * rows, x_ref.shape[1]).astype(BF16)
    o_ref[...] = (x_ref[...]
                  + jnp.dot(jnp.concatenate(hi_parts, axis=1), rhs, preferred_element_type=F32)
                  + jnp.dot(jnp.concatenate(lo_parts, axis=1), rhs, preferred_element_type=F32))

    lane_r = lax.broadcasted_iota(I32, (1, rows), 1)
    ends = [st_ref[(t + 1) * ne + e] for e in range(ne)]
    any_extra = functools.reduce(jnp.logical_or, [ends[e] > chunk_start(t, e, 1) for e in range(ne)])

    @pl.when(any_extra)
    def _():
        for e in range(ne):
            for c in range(1, tm // rows + 1):

                @pl.when(ends[e] > chunk_start(t, e, c))
                def _():
                    cp = pltpu.make_async_copy(y_hbm.at[e, pl.ds(chunk_start(t, e, c), rows)],
                                               xbuf_ref, xsem)
                    cp.start()
                    cp.wait()
                    rel = slot_ref[:, e:e + 1] - chunk_start(t, e, c)
                    oh = jnp.where(rel == lane_r, 1.0, 0.0).astype(BF16)
                    contrib = jnp.dot(oh, xbuf_ref[...].astype(BF16), preferred_element_type=F32)
                    o_ref[...] += aff_ref[:, e:e + 1] * contrib


def moe_combine(starts, x1, aff, slot, ye, *, tm, rows):
    s, d = x1.shape
    ne = aff.shape[1]
    nt = s // tm
    assert V7X_LANES % rows == 0 and ne % (V7X_LANES // rows) == 0
    grid_spec = pltpu.PrefetchScalarGridSpec(
        num_scalar_prefetch=1,
        grid=(nt,),
        in_specs=[pl.BlockSpec((tm, d), lambda t, st: (t, 0)),
                  pl.BlockSpec((tm, ne), lambda t, st: (t, 0)),
                  pl.BlockSpec((tm, ne), lambda t, st: (t, 0)),
                  pl.BlockSpec(memory_space=pl.ANY)],
        out_specs=pl.BlockSpec((tm, d), lambda t, st: (t, 0)),
        scratch_shapes=[pltpu.VMEM((2, ne, rows, d), F32), pltpu.VMEM((rows, d), F32),
                        pltpu.SemaphoreType.DMA((2, ne)), pltpu.SemaphoreType.DMA(())],
    )
    return pl.pallas_call(
        functools.partial(_combine_kernel, ne=ne, rows=rows, nt=nt),
        out_shape=jax.ShapeDtypeStruct((s, d), F32),
        grid_spec=grid_spec,
        compiler_params=_cp("arbitrary"),
        name="moe_combine",
    )(starts.reshape(-1), x1, aff, slot, ye)


TM_PROJ, TN_PROJ = 1024, 1024
TM_ROW = 256
TM_OUT = 512
ATTN_TILE = 512
SEL_BLK = 256
FFN_CHUNK = 256
COMBINE_ROWS = 64


def _layer(x, l, p, bias, far_bias):
    s = x.shape[0]
    lam_init = 0.8 - 0.6 * math.exp(-0.3 * l)
    proj, xcat = norm_matmul(x, p["norm_mix"], p["w_in"], l, tm=min(TM_PROJ, s), tn=TN_PROJ)

    ycat = s5_mixer(xcat, p["s5_ops"], p["ssm_d"], l)
    yg = glu(ycat, p["w_glu"], l, tm=min(TM_PROJ, s))

    qn, k4, vt4, kmax = attn_prep(proj, p["q_gain"], p["k_gain"], tk=ATTN_TILE)
    lam = (jnp.exp(jnp.sum(p["lambda_q1"].astype(F32) * p["lambda_k1"].astype(F32)))
           - jnp.exp(jnp.sum(p["lambda_q2"].astype(F32) * p["lambda_k2"].astype(F32))) + lam_init)
    ao = diff_attention(qn, k4, vt4, bias, far_bias, kmax, lam, p["subln_gain"], lam_init=lam_init,
                        tq=ATTN_TILE, tk=ATTN_TILE)

    merged = gated_merge(yg, ao, proj, p["w_ssm_branch"], p["w_attn_branch"], l,
                         tm=min(TM_PROJ, s), tn=TN_PROJ)
    x1, h2, afft = out_proj_router(merged, x, p["w_out"].astype(BF16), p["norm_ffn"], p["w_router"],
                                   tm=min(TM_OUT, s))

    cap = CAPACITY_FACTOR * s // N_EXPERTS
    slot_t, cum_t, idx = expert_select(afft, cap=cap, blk=SEL_BLK)
    ye = expert_ffn(idx, h2, p["w_expert_gate"], p["w_expert_up"], p["w_expert_down"], l,
                    cap=cap, pad=COMBINE_ROWS, fc=FFN_CHUNK)
    starts = jnp.concatenate([cum_t[:, ::TM_ROW].T, jnp.full((1, N_EXPERTS), cap, I32)], axis=0)
    return moe_combine(starts, x1, afft.T, slot_t.T, ye, tm=TM_ROW, rows=COMBINE_ROWS)


_LAYER_PARAMS = ("ssm_d", "q_gain", "k_gain", "lambda_q1",
                 "lambda_k1", "lambda_q2", "lambda_k2", "subln_gain", "w_out",
                 "norm_mix", "norm_ffn", "w_router")
_STACKED_PARAMS = ("w_in", "w_expert_gate", "w_expert_up", "w_expert_down", "w_glu", "w_ssm_branch",
                   "w_attn_branch")


def kernel(x, w_in, ssm_a_re, ssm_a_im, ssm_log_dt, ssm_b_re, ssm_b_im, ssm_c_re, ssm_c_im, ssm_d, w_glu, w_ssm_branch, q_gain, k_gain, lambda_q1, lambda_k1, lambda_q2, lambda_k2, subln_gain, w_attn_branch, rel_bias, w_out, norm_mix, norm_ffn, w_router, w_expert_gate, w_expert_up, w_expert_down):
    args = dict(locals())
    b = x.shape[0]
    bias = bias_tiles(rel_bias, tk=ATTN_TILE, tq=ATTN_TILE)
    far_bias = attn_bias_consts(rel_bias)
    s5_ops = jax.vmap(s5_operators)(ssm_a_re, ssm_a_im, ssm_log_dt, ssm_b_re, ssm_b_im, ssm_c_re, ssm_c_im)
    outs = []
    for bi in range(b):
        xb = x[bi].astype(F32)
        for l in range(DEPTH):
            p = {k: args[k][l] for k in _LAYER_PARAMS}
            p.update({k: args[k] for k in _STACKED_PARAMS})
            p["s5_ops"] = s5_ops
            xb = _layer(xb, l, p, bias, far_bias)
        outs.append(xb)
    return jnp.stack(outs, axis=0).astype(x.dtype)
```

```python
import functools
import math

import jax
import jax.numpy as jnp
from jax import lax
from jax.experimental import pallas as pl
from jax.experimental.pallas import tpu as pltpu

F32 = jnp.float32
BF16 = jnp.bfloat16
I32 = jnp.int32

D_MODEL = 2048
DEPTH = 2
SSM_WIDTH = D_MODEL // 2
SSM_GROUP = 16
SSM_GROUPS = SSM_WIDTH // SSM_GROUP
SSM_STATE = 64
ATTN_HEADS = 8
QK_DIM = 64
V_DIM = 2 * QK_DIM
ATTN_WIDTH = ATTN_HEADS * V_DIM
QK_COLS = ATTN_HEADS * 2 * QK_DIM
REL_BUCKETS = 32
REL_MAX_DIST = 128
N_EXPERTS = 16
CAPACITY_FACTOR = 2
EXPERT_FF = D_MODEL
NORM_EPS = 1e-6
IN_COLS = SSM_WIDTH + 2 * QK_COLS + ATTN_WIDTH + 2 * D_MODEL
COL_Q = SSM_WIDTH
COL_K = COL_Q + QK_COLS
COL_V = COL_K + QK_COLS
COL_GS = COL_V + ATTN_WIDTH
COL_GA = COL_GS + D_MODEL

V7X_LANES = 128
F32_SUBLANES = 8
V7X_VMEM_BYTES = 64 * 1024 * 1024
VMEM_LIMIT = V7X_VMEM_BYTES - 8 * 1024 * 1024
LOG2E = 1.4426950408889634

S5_CHUNK = 16
S5_GB = 8
NEG_BIG = -1e30


def _cp(*sem):
    return pltpu.CompilerParams(dimension_semantics=sem, vmem_limit_bytes=VMEM_LIMIT)


def _norm_matmul_kernel(x_ref, g_ref, w_ref, o_ref, xcat_ref, h_ref, u_ref):
    j = pl.program_id(1)

    @pl.when(j == 0)
    def _():
        x = x_ref[...]
        r = lax.rsqrt(jnp.mean(x * x, axis=-1, keepdims=True) + NORM_EPS)
        h_ref[...] = (x * r * g_ref[...]).astype(BF16)

    res = jnp.dot(h_ref[...], w_ref[...].astype(BF16), preferred_element_type=F32)
    o_ref[...] = res.astype(o_ref.dtype)

    @pl.when(j == 0)
    def _():
        cn = u_ref.shape[1] // S5_CHUNK
        for slab in range(u_ref.shape[0]):
            u_ref[slab] = res[:, slab * V7X_LANES:(slab + 1) * V7X_LANES]
            for t in range(S5_CHUNK):
                piece = u_ref[slab, pl.ds(t, cn, stride=S5_CHUNK), :]
                xcat_ref[slab, :, t * V7X_LANES:(t + 1) * V7X_LANES] = piece.astype(xcat_ref.dtype)


def norm_matmul(x, gain, w, layer, *, tm, tn):
    s, d = x.shape
    n = w.shape[2]
    assert tn == SSM_WIDTH and tm % (S5_CHUNK * F32_SUBLANES) == 0
    nslab = SSM_WIDTH // V7X_LANES
    return pl.pallas_call(
        _norm_matmul_kernel,
        out_shape=(jax.ShapeDtypeStruct((s, n), BF16),
                   jax.ShapeDtypeStruct((nslab, s // S5_CHUNK, S5_CHUNK * V7X_LANES), BF16)),
        grid=(s // tm, n // tn),
        in_specs=[pl.BlockSpec((tm, d), lambda i, j: (i, 0)),
                  pl.BlockSpec((1, d), lambda i, j: (0, 0)),
                  pl.BlockSpec((None, d, tn), lambda i, j: (layer, 0, j))],
        out_specs=(pl.BlockSpec((tm, tn), lambda i, j: (i, j)),
                   pl.BlockSpec((nslab, tm // S5_CHUNK, S5_CHUNK * V7X_LANES), lambda i, j: (0, i, 0))),
        scratch_shapes=[pltpu.VMEM((tm, d), BF16), pltpu.VMEM((nslab, tm, V7X_LANES), F32)],
        compiler_params=_cp("parallel", "arbitrary"),
        name="norm_in_proj",
    )(x, gain.reshape(1, d).astype(F32), w)


def _attn_prep_kernel(q_ref, k_ref, v_ref, gq_ref, gk_ref, qo_ref, ko_ref, vo_ref, kn_ref):
    lane = lax.broadcasted_iota(I32, (1, V7X_LANES), 1)
    lo_mask = lane < QK_DIM

    def norm(src_ref, g_ref, a):
        x = src_ref[:, a * V7X_LANES:(a + 1) * V7X_LANES].astype(F32)
        ss = x * x
        lo = jnp.sum(jnp.where(lo_mask, ss, 0.0), axis=-1, keepdims=True)
        hi = jnp.sum(jnp.where(lo_mask, 0.0, ss), axis=-1, keepdims=True)
        ms = jnp.where(lo_mask, lo, hi) * (1.0 / QK_DIM)
        return x * lax.rsqrt(ms + NORM_EPS) * g_ref[...]

    for a in range(ATTN_HEADS):
        qo_ref[:, a * V7X_LANES:(a + 1) * V7X_LANES] = norm(q_ref, gq_ref, a).astype(qo_ref.dtype)
        kb = norm(k_ref, gk_ref, a).astype(ko_ref.dtype)
        ko_ref[a, 0] = kb
        kk = kb.astype(F32) * kb.astype(F32)
        n_lo = jnp.max(jnp.sum(jnp.where(lo_mask, kk, 0.0), axis=-1, keepdims=True), axis=0, keepdims=True)
        n_hi = jnp.max(jnp.sum(jnp.where(lo_mask, 0.0, kk), axis=-1, keepdims=True), axis=0, keepdims=True)
        kn_ref[0, a:a + 1, :] = jnp.where(lane == 0, n_lo, jnp.where(lane == 1, n_hi, 0.0))
        v = v_ref[:, a * V7X_LANES:(a + 1) * V7X_LANES].astype(F32)
        vo_ref[a, 0] = v.T.astype(vo_ref.dtype)


def attn_prep(proj, q_gain, k_gain, *, tk):
    s = proj.shape[0]
    nk = s // tk
    gq = (jnp.tile(q_gain.astype(F32), 2) * (QK_DIM ** -0.5 * LOG2E)).reshape(1, V7X_LANES)
    gk = jnp.tile(k_gain.astype(F32), 2).reshape(1, V7X_LANES)
    cq, ck, cv = COL_Q // QK_COLS, COL_K // QK_COLS, COL_V // ATTN_WIDTH
    qn, k4, vt4, kn2 = pl.pallas_call(
        _attn_prep_kernel,
        out_shape=(jax.ShapeDtypeStruct((s, QK_COLS), BF16),
                   jax.ShapeDtypeStruct((ATTN_HEADS, nk, tk, V_DIM), BF16),
                   jax.ShapeDtypeStruct((ATTN_HEADS, nk, V_DIM, tk), BF16),
                   jax.ShapeDtypeStruct((nk, ATTN_HEADS, V7X_LANES), F32)),
        grid=(nk,),
        in_specs=[pl.BlockSpec((tk, QK_COLS), lambda i: (i, cq)),
                  pl.BlockSpec((tk, QK_COLS), lambda i: (i, ck)),
                  pl.BlockSpec((tk, ATTN_WIDTH), lambda i: (i, cv)),
                  pl.BlockSpec((1, V7X_LANES), lambda i: (0, 0)),
                  pl.BlockSpec((1, V7X_LANES), lambda i: (0, 0))],
        out_specs=(pl.BlockSpec((tk, QK_COLS), lambda i: (i, 0)),
                   pl.BlockSpec((ATTN_HEADS, 1, tk, V_DIM), lambda i: (0, i, 0, 0)),
                   pl.BlockSpec((ATTN_HEADS, 1, V_DIM, tk), lambda i: (0, i, 0, 0)),
                   pl.BlockSpec((1, ATTN_HEADS, V7X_LANES), lambda i: (i, 0, 0))),
        compiler_params=_cp("parallel"),
        name="attn_prep",
    )(proj, proj, proj, gq, gk)
    kmax = jnp.sqrt(jnp.max(kn2[:, :, :2], axis=0))
    return qn, k4, vt4, kmax


ATTN_NEAR = 3
ATTN_BIAS_TILES = 5
ATTN_UNROLL = 6


def _bias_tile_kernel(rb_ref, o_ref, *, tk, tq):
    h = pl.program_id(0)
    d = pl.program_id(1)
    w = tk + tq
    m = lax.broadcasted_iota(I32, (F32_SUBLANES, w), 1)
    rel = (d - ATTN_BIAS_TILES // 2) * tk + (tk - 1) - m
    half = REL_BUCKETS // 2
    exact = half // 2
    side = jnp.where(rel > 0, half, 0).astype(I32)
    n = jnp.abs(rel)
    nf = jnp.maximum(n, 1).astype(F32)
    large = exact + (jnp.log(nf / exact) / math.log(REL_MAX_DIST / exact) * (half - exact)).astype(I32)
    large = jnp.minimum(large, half - 1)
    bucket = side + jnp.where(n < exact, n, large).astype(I32)
    val = jnp.zeros((F32_SUBLANES, w), F32)
    for b in range(REL_BUCKETS):
        val = jnp.where(bucket == b, rb_ref[b, h], val)
    table = jnp.broadcast_to(val[0:1, :] * LOG2E, (tk, w))
    o_ref[0, 0] = pltpu.roll(table, w - tk + 1, 1, stride=1, stride_axis=0)[:, :tq]


def bias_tiles(rel_bias, *, tk, tq):
    assert tk == tq and (tk + tq) & (tk + tq - 1) == 0
    return pl.pallas_call(
        functools.partial(_bias_tile_kernel, tk=tk, tq=tq),
        out_shape=jax.ShapeDtypeStruct((ATTN_HEADS, ATTN_BIAS_TILES, tk, tq), F32),
        grid=(ATTN_HEADS, ATTN_BIAS_TILES),
        in_specs=[pl.BlockSpec(memory_space=pltpu.SMEM)],
        out_specs=pl.BlockSpec((1, 1, tk, tq), lambda h, d: (h, d, 0, 0)),
        compiler_params=_cp("parallel", "parallel"),
        name="t5_bias_tiles",
    )(rel_bias.astype(F32))


FAST_MIN_SUM = 2.0 ** -90


def _attn_kernel(lam_ref, far_ref, kmax_ref, q_ref, k_ref, vt_ref, bias_ref, g_ref, o_ref,
                 s_ref, cm_ref, m_ref, l_ref, acc_ref, p_ref, *, nk, tk, tq, out_scale):
    h = pl.program_id(0)
    i = pl.program_id(1)
    lane = lax.broadcasted_iota(I32, (1, V7X_LANES), 1)
    q = q_ref[...]
    zero = jnp.zeros_like(q)
    qmaps = (jnp.where(lane < QK_DIM, q, zero), jnp.where(lane < QK_DIM, zero, q))

    n0 = jnp.clip(i - 1, 0, nk - ATTN_NEAR)
    nfar = nk - ATTN_NEAR

    def far_tile(t):
        return jnp.where(t < n0, t, t + ATTN_NEAR)

    def far_const(j):
        return jnp.where(j < i, far_ref[h, 0], far_ref[h, 1])

    def near_bias(j):
        return bias_ref[0, j - i + ATTN_BIAS_TILES // 2]

    qf = q.astype(F32)
    row8 = lax.broadcasted_iota(I32, (F32_SUBLANES, V7X_LANES), 0)
    lane8 = lax.broadcasted_iota(I32, (F32_SUBLANES, V7X_LANES), 1)
    pick = jnp.where(jnp.logical_or(jnp.logical_and(row8 == 0, lane8 < QK_DIM),
                                    jnp.logical_and(row8 == 1, lane8 >= QK_DIM)), 1.0, 0.0)
    qn2 = lax.dot_general(pick, qf * qf, (((1,), (1,)), ((), ())), preferred_element_type=F32,
                          precision=lax.Precision.HIGHEST)
    shift = [jnp.sqrt(qn2[c:c + 1, :]) * kmax_ref[h, c] + far_ref[h, 2] for c in range(2)]
    l_ref[...] = jnp.zeros(l_ref.shape, F32)
    acc_ref[...] = jnp.zeros(acc_ref.shape, F32)

    def fast_probs(j, slot, bias_tile, bias_const):
        kt = k_ref[0, j]
        for c in range(2):
            s = lax.dot_general(kt, qmaps[c], (((1,), (1,)), ((), ())), preferred_element_type=F32)
            if bias_tile is not None:
                s = s + bias_tile
            p = jnp.exp2(s - (shift[c] - bias_const))
            l_ref[c:c + 1, :] += jnp.sum(p, axis=0, keepdims=True)
            p_ref[slot, c] = p.astype(BF16)

    def fast_pv(j, slot):
        vt = vt_ref[0, j]
        for c in range(2):
            acc_ref[c] += jnp.dot(vt, p_ref[slot, c], preferred_element_type=F32)

    fast_probs(n0, 0, near_bias(n0), 0.0)
    for w in range(1, ATTN_NEAR):
        fast_probs(n0 + w, w % 2, near_bias(n0 + w), 0.0)
        fast_pv(n0 + w - 1, (w - 1) % 2)
    j0 = far_tile(0)
    fast_probs(j0, ATTN_NEAR % 2, None, far_const(j0))
    fast_pv(n0 + ATTN_NEAR - 1, (ATTN_NEAR - 1) % 2)

    def fast_group(u, jprev):
        for w in range(ATTN_UNROLL):
            jn = far_tile(ATTN_UNROLL * u + w + 1)
            fast_probs(jn, (ATTN_NEAR + 1 + w) % 2, None, far_const(jn))
            fast_pv(jprev, (ATTN_NEAR + w) % 2)
            jprev = jn
        return jprev

    jl = lax.fori_loop(0, (nfar - 1) // ATTN_UNROLL, fast_group, j0)
    fast_pv(jl, (ATTN_NEAR + nfar - 1) % 2)

    lmin = jnp.min(jnp.minimum(l_ref[0:1, :], l_ref[1:2, :]))

    @pl.when(jnp.logical_not(lmin >= FAST_MIN_SUM))
    def _():
        _attn_running_max(h, i, n0, nfar, far_tile, far_const, near_bias, qmaps, k_ref, vt_ref,
                          s_ref, cm_ref, m_ref, l_ref, acc_ref, tq=tq)

    o1 = acc_ref[0] / l_ref[0:1, :]
    o2 = acc_ref[1] / l_ref[1:2, :]
    o = o1 - lam_ref[0] * o2
    r = lax.rsqrt(jnp.mean(o * o, axis=0, keepdims=True) + NORM_EPS)
    o = o * r * g_ref[...] * out_scale
    o_ref[...] = o.T.astype(o_ref.dtype)


def _attn_running_max(h, i, n0, nfar, far_tile, far_const, near_bias, qmaps, k_ref, vt_ref,
                      s_ref, cm_ref, m_ref, l_ref, acc_ref, *, tq):
    m_ref[...] = jnp.full(m_ref.shape, NEG_BIG, F32)
    l_ref[...] = jnp.zeros(l_ref.shape, F32)
    acc_ref[...] = jnp.zeros(acc_ref.shape, F32)

    def scores(j, slot, bias_tile, bias_const):
        kt = k_ref[0, j]
        for c in range(2):
            s = lax.dot_general(kt, qmaps[c], (((1,), (1,)), ((), ())),
                                preferred_element_type=F32)
            if bias_tile is not None:
                s = s + bias_tile
            s_ref[slot, c] = s
            cm_ref[slot, c:c + 1, :] = jnp.max(s, axis=0, keepdims=True) + bias_const
        cm_ref[slot, 2:3, :] = jnp.zeros((1, tq), F32) + bias_const

    def absorb(j, slot):
        vt = vt_ref[0, j]
        cb = cm_ref[slot, 2:3, :]
        for c in range(2):
            m_old = m_ref[c:c + 1, :]
            m_new = jnp.maximum(m_old, cm_ref[slot, c:c + 1, :])
            alpha = jnp.exp2(m_old - m_new)
            p = jnp.exp2(s_ref[slot, c] - (m_new - cb))
            l_ref[c:c + 1, :] = alpha * l_ref[c:c + 1, :] + jnp.sum(p, axis=0, keepdims=True)
            acc_ref[c] = alpha * acc_ref[c] + jnp.dot(vt, p.astype(BF16), preferred_element_type=F32)
            m_ref[c:c + 1, :] = m_new

    scores(n0, 0, near_bias(n0), 0.0)
    scores(n0 + 1, 1, near_bias(n0 + 1), 0.0)
    absorb(n0, 0)
    scores(n0 + 2, 0, near_bias(n0 + 2), 0.0)
    absorb(n0 + 1, 1)
    j0 = far_tile(0)
    scores(j0, 1, None, far_const(j0))
    absorb(n0 + 2, 0)

    def group(u, jprev):
        for w in range(ATTN_UNROLL):
            jn = far_tile(ATTN_UNROLL * u + w + 1)
            scores(jn, w % 2, None, far_const(jn))
            absorb(jprev, (w + 1) % 2)
            jprev = jn
        return jprev

    jlast = lax.fori_loop(0, (nfar - 1) // ATTN_UNROLL, group, j0)
    absorb(jlast, 1)


def attn_bias_consts(rel_bias):
    half = REL_BUCKETS // 2
    rb = rel_bias.astype(F32) * LOG2E
    return jnp.stack([rb[half - 1], rb[REL_BUCKETS - 1], jnp.max(rb, axis=0)], axis=1)


def diff_attention(qn, k4, vt4, bias, far_bias, kmax, lam, subln_gain, *, lam_init, tq, tk):
    s = qn.shape[0]
    nk = s // tk
    assert tq == tk and nk > ATTN_NEAR and (nk - ATTN_NEAR - 1) % ATTN_UNROLL == 0
    kern = functools.partial(_attn_kernel, nk=nk, tk=tk, tq=tq, out_scale=1.0 - lam_init)
    return pl.pallas_call(
        kern,
        out_shape=jax.ShapeDtypeStruct((s, ATTN_WIDTH), BF16),
        grid=(ATTN_HEADS, s // tq),
        in_specs=[pl.BlockSpec(memory_space=pltpu.SMEM),
                  pl.BlockSpec(memory_space=pltpu.SMEM),
                  pl.BlockSpec(memory_space=pltpu.SMEM),
                  pl.BlockSpec((tq, V_DIM), lambda h, i: (i, h)),
                  pl.BlockSpec((1, nk, tk, V_DIM), lambda h, i: (h, 0, 0, 0)),
                  pl.BlockSpec((1, nk, V_DIM, tk), lambda h, i: (h, 0, 0, 0)),
                  pl.BlockSpec((1, ATTN_BIAS_TILES, tk, tq), lambda h, i: (h, 0, 0, 0)),
                  pl.BlockSpec((V_DIM, 1), lambda h, i: (0, 0))],
        out_specs=pl.BlockSpec((tq, V_DIM), lambda h, i: (i, h)),
        scratch_shapes=[pltpu.VMEM((2, 2, tk, tq), F32),
                        pltpu.VMEM((2, F32_SUBLANES, tq), F32),
                        pltpu.VMEM((F32_SUBLANES, tq), F32),
                        pltpu.VMEM((F32_SUBLANES, tq), F32),
                        pltpu.VMEM((2, V_DIM, tq), F32),
                        pltpu.VMEM((2, 2, tk, tq), BF16)],
        compiler_params=_cp("parallel", "arbitrary"),
        name="diff_attention",
    )(lam.reshape(1).astype(F32), far_bias, kmax, qn, k4, vt4, bias,
      subln_gain.reshape(V_DIM, 1).astype(F32))


def s5_operators(a_re, a_im, log_dt, b_re, b_im, c_re, c_im):
    t_len, hp = S5_CHUNK, lax.Precision.HIGHEST
    a_re, a_im = a_re.astype(F32), a_im.astype(F32)
    dt = jnp.exp(log_dt.astype(F32))[..., None]
    steps = jnp.arange(t_len + 1, dtype=F32)[:, None, None, None]
    mag = jnp.exp(a_re * dt * steps)
    ang = a_im * dt * steps
    pw_re, pw_im = mag * jnp.cos(ang), mag * jnp.sin(ang)
    den = a_re * a_re + a_im * a_im
    nr, ni = pw_re[1] - 1.0, pw_im[1]
    coef_re = ((nr * a_re + ni * a_im) / den)[..., None]
    coef_im = ((ni * a_re - nr * a_im) / den)[..., None]
    b_re, b_im = b_re.astype(F32), b_im.astype(F32)
    bb_re = coef_re * b_re - coef_im * b_im
    bb_im = coef_re * b_im + coef_im * b_re
    c_re, c_im = c_re.astype(F32), c_im.astype(F32)

    g, n_st, p_ch = a_re.shape[1], SSM_STATE, SSM_GROUP
    tp = t_len * p_ch
    pwt_re = jnp.transpose(pw_re, (1, 2, 3, 0))
    pwt_im = jnp.transpose(pw_im, (1, 2, 3, 0))
    ct_re = jnp.transpose(c_re, (0, 1, 3, 2))
    ct_im = jnp.transpose(c_im, (0, 1, 3, 2))
    bbt_re = jnp.transpose(bb_re, (0, 1, 3, 2))
    bbt_im = jnp.transpose(bb_im, (0, 1, 3, 2))
    cp_re = ct_re[:, :, :, None, :] * pwt_re[..., None] - ct_im[:, :, :, None, :] * pwt_im[..., None]
    cp_im = ct_re[:, :, :, None, :] * pwt_im[..., None] + ct_im[:, :, :, None, :] * pwt_re[..., None]

    zlag = jnp.zeros((g, n_st, t_len - 1, p_ch), F32)

    def lagged(cp):
        return (jnp.concatenate([zlag, cp[0, :, :, :t_len]], axis=2),
                jnp.concatenate([jnp.flip(cp[1, :, :, :t_len], axis=2), zlag], axis=2))

    rf_re, rb_re = lagged(cp_re)
    rf_im, rb_im = lagged(cp_im)
    r_cat = jnp.concatenate([rf_re, rf_im, rb_re, rb_im], axis=1).reshape(g, 4 * n_st, (2 * t_len - 1) * p_ch)
    a_cat = jnp.concatenate([bbt_re[0], -bbt_im[0], bbt_re[1], -bbt_im[1]], axis=-1)
    kp = jnp.einsum('gpk,gkx->gpx', a_cat, r_cat, precision=hp)
    toep = jnp.stack([kp[:, :, (t_len - 1 - j) * p_ch:(t_len - 1 - j) * p_ch + tp] for j in range(t_len)],
                     axis=1).reshape(g, tp, tp).astype(BF16)

    def seg_powers(d, reverse):
        pr = jnp.transpose(pw_re[:t_len, d], (1, 0, 2))
        pi = jnp.transpose(pw_im[:t_len, d], (1, 0, 2))
        if reverse:
            pr, pi = jnp.flip(pr, axis=1), jnp.flip(pi, axis=1)
        return pr, pi

    prf, pif = seg_powers(0, True)
    prb, pib = seg_powers(1, False)
    pa = jnp.concatenate([prf, pif, pif, prf, prb, pib, pib, prb], axis=-1)
    pb = jnp.concatenate([-pif, prf, prf, -pif, -pib, prb, prb, -pib], axis=-1)
    br = jnp.concatenate([bbt_re[0]] * 4 + [bbt_re[1]] * 4, axis=-1)
    bi = jnp.concatenate([bbt_im[0]] * 4 + [bbt_im[1]] * 4, axis=-1)
    smap = (br[:, None] * pa[:, :, None, :] + bi[:, None] * pb[:, :, None, :]).reshape(g, tp, 8 * n_st)
    smap = smap.astype(BF16)

    def out_map(d, reverse):
        wr, wi = cp_re[d][:, :, 1:t_len + 1], cp_im[d][:, :, 1:t_len + 1]
        if reverse:
            wr, wi = jnp.flip(wr, axis=2), jnp.flip(wi, axis=2)
        return [wr.reshape(g, n_st, tp), -wi.reshape(g, n_st, tp)]

    mc = jnp.concatenate(out_map(0, False) + out_map(1, True), axis=1).astype(BF16)

    def carry(d):
        ar, ai = pw_re[t_len, d], pw_im[t_len, d]
        return [jnp.concatenate([ar, ar], -1), jnp.concatenate([-ai, ai], -1),
                jnp.concatenate([ai, -ai], -1)]

    coef = jnp.stack(carry(0) + carry(1), axis=0)
    return toep, smap, mc, coef


def _gelu_tanh(x):
    return 0.5 * x * (1.0 + jnp.tanh(math.sqrt(2.0 / math.pi) * (x + 0.044715 * (x * x * x))))


def _s5_kernel(x_ref, sel_ref, toep_ref, smap_ref, mc_ref, coef_ref, d_ref, y_ref,
               u_ref, ef_ref, efs_ref, eb_ref, ebs_ref, *, cn):
    gb = toep_ref.shape[0]
    tp = toep_ref.shape[1]
    n2 = 2 * SSM_STATE
    xcat = x_ref[0]

    def sel(g):
        off = (gb - 1 - g) * SSM_GROUP
        return sel_ref[off:off + xcat.shape[1], :]

    for g in range(gb):
        u_ref[g] = jnp.dot(xcat, sel(g), preferred_element_type=F32).astype(BF16)
        e = jnp.dot(u_ref[g], smap_ref[g], preferred_element_type=F32)
        for r, ref in enumerate((ef_ref, efs_ref, eb_ref, ebs_ref)):
            ref[pl.ds(g, cn, stride=gb), :] = e[:, r * n2:(r + 1) * n2]

    cf, cfs, cfw = coef_ref[0], coef_ref[1], coef_ref[2]
    cb, cbs, cbw = coef_ref[3], coef_ref[4], coef_ref[5]

    def step(c, carry):
        s, sw, r, rw = carry
        fo = pl.multiple_of(c * gb, gb)
        bo = pl.multiple_of((cn - 1 - c) * gb, gb)
        e, es = ef_ref[pl.ds(fo, gb), :], efs_ref[pl.ds(fo, gb), :]
        ef_ref[pl.ds(fo, gb), :] = s
        s, sw = cf * s + cfs * sw + e, cf * sw + cfw * s + es
        e, es = eb_ref[pl.ds(bo, gb), :], ebs_ref[pl.ds(bo, gb), :]
        eb_ref[pl.ds(bo, gb), :] = r
        r, rw = cb * r + cbs * rw + e, cb * rw + cbw * r + es
        return s, sw, r, rw

    z = jnp.zeros((gb, n2), F32)
    lax.fori_loop(0, cn, step, (z, z, z, z))

    ycat = None
    for g in range(gb):
        u = u_ref[g]
        st = jnp.concatenate([ef_ref[pl.ds(g, cn, stride=gb), :], eb_ref[pl.ds(g, cn, stride=gb), :]],
                             axis=1).astype(BF16)
        y = (jnp.dot(u, toep_ref[g], preferred_element_type=F32)
             + jnp.dot(st, mc_ref[g], preferred_element_type=F32)
             + u.astype(F32) * d_ref[g])
        placed = lax.dot_general(_gelu_tanh(y).astype(BF16), sel(g), (((1,), (1,)), ((), ())),
                                 preferred_element_type=F32)
        ycat = placed if ycat is None else ycat + placed
    y_ref[0] = ycat.astype(y_ref.dtype)


def s5_lane_selector():
    gb, p, t_len = S5_GB, SSM_GROUP, S5_CHUNK
    r = jnp.arange(t_len * V7X_LANES + (gb - 1) * p)[:, None] - (gb - 1) * p
    c = jnp.arange(t_len * p)[None, :]
    hit = (r >= 0) & (r // V7X_LANES == c // p) & ((r % V7X_LANES) // p == 0) & (r % p == c % p)
    return hit.astype(BF16)


def s5_mixer(xcat, ops, ssm_d, layer):
    toep, smap, mc, coef = ops
    g, p, t_len = SSM_GROUPS, SSM_GROUP, S5_CHUNK
    nslab, cn, _ = xcat.shape
    tp = t_len * p
    gb = S5_GB
    assert gb * p == V7X_LANES and nslab == g // gb
    sel = s5_lane_selector()
    dsk = jnp.tile(ssm_d.astype(F32).reshape(g, 1, p), (1, 1, t_len))
    ycat = pl.pallas_call(
        functools.partial(_s5_kernel, cn=cn),
        out_shape=jax.ShapeDtypeStruct((nslab, cn, t_len * V7X_LANES), BF16),
        grid=(nslab,),
        in_specs=[pl.BlockSpec((1, cn, t_len * V7X_LANES), lambda i: (i, 0, 0)),
                  pl.BlockSpec(sel.shape, lambda i: (0, 0)),
                  pl.BlockSpec((None, gb, tp, tp), lambda i: (layer, i, 0, 0)),
                  pl.BlockSpec((None, gb, tp, smap.shape[3]), lambda i: (layer, i, 0, 0)),
                  pl.BlockSpec((None, gb, mc.shape[2], tp), lambda i: (layer, i, 0, 0)),
                  pl.BlockSpec((None, 6, gb, 2 * SSM_STATE), lambda i: (layer, 0, i, 0)),
                  pl.BlockSpec((gb, 1, tp), lambda i: (i, 0, 0))],
        out_specs=pl.BlockSpec((1, cn, t_len * V7X_LANES), lambda i: (i, 0, 0)),
        scratch_shapes=[pltpu.VMEM((gb, cn, tp), BF16)]
        + [pltpu.VMEM((cn * gb, 2 * SSM_STATE), F32) for _ in range(4)],
        compiler_params=_cp("parallel"),
        name="s5_chunked_scan",
    )(xcat, sel, toep, smap, mc, coef, dsk)
    return ycat


def _sigmoid(x):
    return 1.0 / (1.0 + jnp.exp(-x))


def _glu_kernel(ycat_ref, w_ref, o_ref, y_ref):
    cn = ycat_ref.shape[1]
    for slab in range(ycat_ref.shape[0]):
        for t in range(S5_CHUNK):
            piece = ycat_ref[slab, :, t * V7X_LANES:(t + 1) * V7X_LANES].astype(F32)
            y_ref[slab, pl.ds(t, cn, stride=S5_CHUNK), :] = piece
    y = jnp.concatenate([y_ref[slab] for slab in range(ycat_ref.shape[0])], axis=1)
    z = jnp.dot(y.astype(BF16), w_ref[0].astype(BF16), preferred_element_type=F32)
    o_ref[...] = (y * _sigmoid(z)).astype(o_ref.dtype)


def glu(ycat, w, layer, *, tm):
    nslab, cn, _ = ycat.shape
    s, d = cn * S5_CHUNK, nslab * V7X_LANES
    return pl.pallas_call(
        _glu_kernel,
        out_shape=jax.ShapeDtypeStruct((s, d), BF16),
        grid=(s // tm,),
        in_specs=[pl.BlockSpec((nslab, tm // S5_CHUNK, S5_CHUNK * V7X_LANES), lambda i: (0, i, 0)),
                  pl.BlockSpec((1, d, d), lambda i: (layer, 0, 0))],
        out_specs=pl.BlockSpec((tm, d), lambda i: (i, 0)),
        scratch_shapes=[pltpu.VMEM((nslab, tm, V7X_LANES), F32)],
        compiler_params=_cp("parallel"),
        name="half_glu",
    )(ycat, w)


def _merge_kernel(yg_ref, ao_ref, gs_ref, ga_ref, ws_ref, wa_ref, o_ref):
    a = jnp.dot(yg_ref[...], ws_ref[0].astype(BF16), preferred_element_type=F32)
    b = jnp.dot(ao_ref[...], wa_ref[0].astype(BF16), preferred_element_type=F32)
    o = _sigmoid(gs_ref[...].astype(F32)) * a + _sigmoid(ga_ref[...].astype(F32)) * b
    o_ref[...] = o.astype(o_ref.dtype)


def gated_merge(yg, ao, proj, ws, wa, layer, *, tm, tn):
    s, k = yg.shape
    n = ws.shape[2]
    cs, ca = COL_GS // tn, COL_GA // tn
    return pl.pallas_call(
        _merge_kernel,
        out_shape=jax.ShapeDtypeStruct((s, n), BF16),
        grid=(s // tm, n // tn),
        in_specs=[pl.BlockSpec((tm, k), lambda i, j: (i, 0)),
                  pl.BlockSpec((tm, k), lambda i, j: (i, 0)),
                  pl.BlockSpec((tm, tn), lambda i, j: (i, cs + j)),
                  pl.BlockSpec((tm, tn), lambda i, j: (i, ca + j)),
                  pl.BlockSpec((1, k, tn), lambda i, j: (layer, 0, j)),
                  pl.BlockSpec((1, k, tn), lambda i, j: (layer, 0, j))],
        out_specs=pl.BlockSpec((tm, tn), lambda i, j: (i, j)),
        compiler_params=_cp("parallel", "arbitrary"),
        name="gated_merge",
    )(yg, ao, proj, proj, ws, wa)


def _out_router_kernel(m_ref, x_ref, w_ref, g_ref, wrt_ref, xo_ref, h_ref, afft_ref):
    x1 = x_ref[...] + jnp.dot(m_ref[...], w_ref[...], preferred_element_type=F32)
    xo_ref[...] = x1
    r = lax.rsqrt(jnp.mean(x1 * x1, axis=-1, keepdims=True) + NORM_EPS)
    h = x1 * r * g_ref[...]
    h_ref[...] = h
    lgt = lax.dot_general(wrt_ref[...], h, (((1,), (1,)), ((), ())),
                          preferred_element_type=F32, precision=lax.Precision.HIGHEST)
    et = jnp.exp(lgt - jnp.max(lgt, axis=0, keepdims=True))
    afft_ref[...] = et / jnp.sum(et, axis=0, keepdims=True)


def out_proj_router(merged, x, w_out, gain, w_router, *, tm):
    s, d = x.shape
    e = w_router.shape[1]
    return pl.pallas_call(
        _out_router_kernel,
        out_shape=(jax.ShapeDtypeStruct((s, d), F32), jax.ShapeDtypeStruct((s, d), F32),
                   jax.ShapeDtypeStruct((e, s), F32)),
        grid=(s // tm,),
        in_specs=[pl.BlockSpec((tm, d), lambda i: (i, 0)),
                  pl.BlockSpec((tm, d), lambda i: (i, 0)),
                  pl.BlockSpec((d, d), lambda i: (0, 0)),
                  pl.BlockSpec((1, d), lambda i: (0, 0)),
                  pl.BlockSpec((e, d), lambda i: (0, 0))],
        out_specs=(pl.BlockSpec((tm, d), lambda i: (i, 0)),
                   pl.BlockSpec((tm, d), lambda i: (i, 0)),
                   pl.BlockSpec((e, tm), lambda i: (0, i))),
        compiler_params=_cp("parallel"),
        name="out_proj_router",
    )(merged, x, w_out, gain.reshape(1, d).astype(F32), w_router.astype(F32).T)


SLOT_LO_BITS = 6
SLOT_LO = 1 << SLOT_LO_BITS


def _select_kernel(afft_ref, slot_ref, cum_ref, idx_ref, acc_ref, *, s, cap, blk):
    ne = afft_ref.shape[0]

    def bit_body(b, thr):
        cand = thr | jnp.left_shift(jnp.ones((ne, 1), I32), 30 - b)
        keys = pltpu.bitcast(afft_ref[...], I32)
        cnt = jnp.sum((keys >= cand).astype(I32), axis=1, keepdims=True)
        return jnp.where(cnt >= cap, cand, thr)

    thr = lax.fori_loop(0, 31, bit_body, jnp.zeros((ne, 1), I32))
    keys = pltpu.bitcast(afft_ref[...], I32)
    need = cap - jnp.sum((keys > thr).astype(I32), axis=1, keepdims=True)

    ri = lax.broadcasted_iota(I32, (blk, blk), 0)
    ci = lax.broadcasted_iota(I32, (blk, blk), 1)
    upper = jnp.where(ri < ci, 1.0, 0.0).astype(BF16)
    na = cap // SLOT_LO
    acol = lax.broadcasted_iota(I32, (na, 1), 0)
    bcol = lax.broadcasted_iota(I32, (SLOT_LO, 1), 0)
    tlane = lax.broadcasted_iota(I32, (1, blk), 1)
    acc_ref[...] = jnp.zeros(acc_ref.shape, F32)

    def blk_body(b, carry):
        ceq, csel = carry
        off = pl.multiple_of(b * blk, blk)
        kb = pltpu.bitcast(afft_ref[:, pl.ds(off, blk)], I32)
        gt = kb > thr
        eq = kb == thr
        eqf = jnp.where(eq, 1.0, 0.0)
        rank_eq = jnp.dot(eqf.astype(BF16), upper, preferred_element_type=F32) + ceq
        sel = jnp.logical_or(gt, jnp.logical_and(eq, rank_eq < need.astype(F32)))
        self_ = jnp.where(sel, 1.0, 0.0)
        cum = jnp.dot(self_.astype(BF16), upper, preferred_element_type=F32) + csel
        cum_i = cum.astype(I32)
        cum_ref[:, pl.ds(off, blk)] = cum_i
        slot = jnp.where(sel, cum_i, -1)
        slot_ref[:, pl.ds(off, blk)] = slot
        tok = off + tlane
        hi = (tok // V7X_LANES).astype(F32)
        lo = (tok % V7X_LANES).astype(F32)
        for e in range(ne):
            srow = slot[e:e + 1, :]
            in_a = lax.shift_right_arithmetic(srow, SLOT_LO_BITS) == acol
            lhs = jnp.concatenate([jnp.where(in_a, hi, 0.0), jnp.where(in_a, lo, 0.0)],
                                  axis=0).astype(BF16)
            rhs = jnp.where((srow & (SLOT_LO - 1)) == bcol, 1.0, 0.0).astype(BF16)
            acc_ref[e] += lax.dot_general(lhs, rhs, (((1,), (1,)), ((), ())), preferred_element_type=F32)
        return (ceq + jnp.sum(eqf, axis=1, keepdims=True), csel + jnp.sum(self_, axis=1, keepdims=True))

    z = jnp.zeros((ne, 1), F32)
    lax.fori_loop(0, s // blk, blk_body, (z, z))
    a = acc_ref[...]
    idx_ref[...] = (a[:, :na, :] * float(V7X_LANES) + a[:, na:, :]).astype(I32)


def expert_select(afft, *, cap, blk):
    ne, s = afft.shape
    assert cap % SLOT_LO == 0
    return pl.pallas_call(
        functools.partial(_select_kernel, s=s, cap=cap, blk=blk),
        out_shape=(jax.ShapeDtypeStruct((ne, s), I32), jax.ShapeDtypeStruct((ne, s), I32),
                   jax.ShapeDtypeStruct((ne, cap // SLOT_LO, SLOT_LO), I32)),
        scratch_shapes=[pltpu.VMEM((ne, 2 * (cap // SLOT_LO), SLOT_LO), F32)],
        compiler_params=pltpu.CompilerParams(vmem_limit_bytes=VMEM_LIMIT),
        name="expert_select",
    )(afft)


GATHER_UNROLL = 8


def _ffn_kernel(idx_ref, h_hbm, wg_ref, wu_ref, wd_ref, y_ref, xg32_ref, xg_ref, sem, *, cap):
    e = pl.program_id(0)
    f = pl.program_id(1)

    def row_copy(r):
        tok = idx_ref[e * cap + r]
        return pltpu.make_async_copy(h_hbm.at[pl.ds(tok, 1)], xg32_ref.at[pl.ds(r, 1)], sem)

    @pl.when(f == 0)
    def _():
        def start(rb, c):
            for w in range(GATHER_UNROLL):
                row_copy(rb * GATHER_UNROLL + w).start()
            return c

        lax.fori_loop(0, cap // GATHER_UNROLL, start, 0)
        y_ref[...] = jnp.zeros(y_ref.shape, F32)
        pltpu.make_async_copy(h_hbm.at[pl.ds(0, cap)], xg32_ref, sem).wait()
        xg_ref[...] = xg32_ref[...].astype(BF16)

    xg = xg_ref[...]
    a = jnp.dot(xg, wg_ref[0, 0].astype(BF16), preferred_element_type=F32)
    b = jnp.dot(xg, wu_ref[0, 0].astype(BF16), preferred_element_type=F32)
    hid = (a * _sigmoid(a) * b).astype(BF16)
    y_ref[0, 0:cap, :] += jnp.dot(hid, wd_ref[0, 0].astype(BF16), preferred_element_type=F32)


def expert_ffn(idx, h2, wg, wu, wd, layer, *, cap, pad, fc):
    _, ne, d, ff = wg.shape
    grid_spec = pltpu.PrefetchScalarGridSpec(
        num_scalar_prefetch=1,
        grid=(ne, ff // fc),
        in_specs=[pl.BlockSpec(memory_space=pl.ANY),
                  pl.BlockSpec((1, 1, d, fc), lambda e, f, idx: (layer, e, 0, f)),
                  pl.BlockSpec((1, 1, d, fc), lambda e, f, idx: (layer, e, 0, f)),
                  pl.BlockSpec((1, 1, fc, d), lambda e, f, idx: (layer, e, f, 0))],
        out_specs=pl.BlockSpec((1, cap + pad, d), lambda e, f, idx: (e, 0, 0)),
        scratch_shapes=[pltpu.VMEM((cap, d), F32), pltpu.VMEM((cap, d), BF16),
                        pltpu.SemaphoreType.DMA(())],
    )
    return pl.pallas_call(
        functools.partial(_ffn_kernel, cap=cap),
        out_shape=jax.ShapeDtypeStruct((ne, cap + pad, d), F32),
        grid_spec=grid_spec,
        compiler_params=_cp("arbitrary", "arbitrary"),
        name="expert_ffn",
    )(idx.reshape(-1), h2, wg, wu, wd)


def _combine_kernel(st_ref, x_ref, aff_ref, slot_ref, y_hbm, o_ref, buf_ref, xbuf_ref, sem, xsem,
                    *, ne, rows, nt):
    t = pl.program_id(0)
    tm = x_ref.shape[0]

    def chunk_start(tt, e, c):
        st8 = (st_ref[tt * ne + e] // F32_SUBLANES) * F32_SUBLANES
        return pl.multiple_of(st8 + c * rows, F32_SUBLANES)

    def first_copy(tt, e, slot):
        return pltpu.make_async_copy(y_hbm.at[e, pl.ds(chunk_start(tt, e, 0), rows)],
                                     buf_ref.at[slot, e], sem.at[slot, e])

    @pl.when(t == 0)
    def _():
        for e in range(ne):
            first_copy(0, e, 0).start()

    @pl.when(t + 1 < nt)
    def _():
        for e in range(ne):
            first_copy(t + 1, e, (t + 1) % 2).start()

    slot = t % 2
    for e in range(ne):
        first_copy(t, e, slot).wait()

    pair = V7X_LANES // rows
    lane = lax.broadcasted_iota(I32, (1, V7X_LANES), 1)
    hi_parts, lo_parts = [], []
    for a in range(ne // pair):
        rel = jnp.zeros((tm, V7X_LANES), I32)
        gate = jnp.zeros((tm, V7X_LANES), F32)
        for b in range(pair):
            e = a * pair + b
            in_e = jnp.logical_and(lane >= b * rows, lane < (b + 1) * rows)
            rel = jnp.where(in_e, slot_ref[:, e:e + 1] - chunk_start(t, e, 0) + b * rows, rel)
            gate = jnp.where(in_e, aff_ref[:, e:e + 1], gate)
        gate = jnp.where(rel == lane, gate, 0.0)
        g_hi = gate.astype(BF16)
        hi_parts.append(g_hi)
        lo_parts.append((gate - g_hi.astype(F32)).astype(BF16))
    rhs = buf_ref[slot].reshape(ne * rows, x_ref.shape[1]).astype(BF16)
    o_ref[...] = (x_ref[...]
                  + jnp.dot(jnp.concatenate(hi_parts, axis=1), rhs, preferred_element_type=F32)
                  + jnp.dot(jnp.concatenate(lo_parts, axis=1), rhs, preferred_element_type=F32))

    lane_r = lax.broadcasted_iota(I32, (1, rows), 1)
    ends = [st_ref[(t + 1) * ne + e] for e in range(ne)]
    any_extra = functools.reduce(jnp.logical_or, [ends[e] > chunk_start(t, e, 1) for e in range(ne)])

    @pl.when(any_extra)
    def _():
        for e in range(ne):
            for c in range(1, tm // rows + 1):

                @pl.when(ends[e] > chunk_start(t, e, c))
                def _():
                    cp = pltpu.make_async_copy(y_hbm.at[e, pl.ds(chunk_start(t, e, c), rows)],
                                               xbuf_ref, xsem)
                    cp.start()
                    cp.wait()
                    rel = slot_ref[:, e:e + 1] - chunk_start(t, e, c)
                    oh = jnp.where(rel == lane_r, 1.0, 0.0).astype(BF16)
                    contrib = jnp.dot(oh, xbuf_ref[...].astype(BF16), preferred_element_type=F32)
                    o_ref[...] += aff_ref[:, e:e + 1] * contrib


def moe_combine(starts, x1, aff, slot, ye, *, tm, rows):
    s, d = x1.shape
    ne = aff.shape[1]
    nt = s // tm
    assert V7X_LANES % rows == 0 and ne % (V7X_LANES // rows) == 0
    grid_spec = pltpu.PrefetchScalarGridSpec(
        num_scalar_prefetch=1,
        grid=(nt,),
        in_specs=[pl.BlockSpec((tm, d), lambda t, st: (t, 0)),
                  pl.BlockSpec((tm, ne), lambda t, st: (t, 0)),
                  pl.BlockSpec((tm, ne), lambda t, st: (t, 0)),
                  pl.BlockSpec(memory_space=pl.ANY)],
        out_specs=pl.BlockSpec((tm, d), lambda t, st: (t, 0)),
        scratch_shapes=[pltpu.VMEM((2, ne, rows, d), F32), pltpu.VMEM((rows, d), F32),
                        pltpu.SemaphoreType.DMA((2, ne)), pltpu.SemaphoreType.DMA(())],
    )
    return pl.pallas_call(
        functools.partial(_combine_kernel, ne=ne, rows=rows, nt=nt),
        out_shape=jax.ShapeDtypeStruct((s, d), F32),
        grid_spec=grid_spec,
        compiler_params=_cp("arbitrary"),
        name="moe_combine",
    )(starts.reshape(-1), x1, aff, slot, ye)


TM_PROJ, TN_PROJ = 1024, 1024
TM_ROW = 256
TM_OUT = 512
ATTN_TILE = 512
SEL_BLK = 256
FFN_CHUNK = 256
COMBINE_ROWS = 64


def _layer(x, l, p, bias, far_bias):
    s = x.shape[0]
    lam_init = 0.8 - 0.6 * math.exp(-0.3 * l)
    proj, xcat = norm_matmul(x, p["norm_mix"], p["w_in"], l, tm=min(TM_PROJ, s), tn=TN_PROJ)

    ycat = s5_mixer(xcat, p["s5_ops"], p["ssm_d"], l)
    yg = glu(ycat, p["w_glu"], l, tm=min(TM_PROJ, s))

    qn, k4, vt4, kmax = attn_prep(proj, p["q_gain"], p["k_gain"], tk=ATTN_TILE)
    lam = (jnp.exp(jnp.sum(p["lambda_q1"].astype(F32) * p["lambda_k1"].astype(F32)))
           - jnp.exp(jnp.sum(p["lambda_q2"].astype(F32) * p["lambda_k2"].astype(F32))) + lam_init)
    ao = diff_attention(qn, k4, vt4, bias, far_bias, kmax, lam, p["subln_gain"], lam_init=lam_init,
                        tq=ATTN_TILE, tk=ATTN_TILE)

    merged = gated_merge(yg, ao, proj, p["w_ssm_branch"], p["w_attn_branch"], l,
                         tm=min(TM_PROJ, s), tn=TN_PROJ)
    x1, h2, afft = out_proj_router(merged, x, p["w_out"].astype(BF16), p["norm_ffn"], p["w_router"],
                                   tm=min(TM_OUT, s))

    cap = CAPACITY_FACTOR * s // N_EXPERTS
    slot_t, cum_t, idx = expert_select(afft, cap=cap, blk=SEL_BLK)
    ye = expert_ffn(idx, h2, p["w_expert_gate"], p["w_expert_up"], p["w_expert_down"], l,
                    cap=cap, pad=COMBINE_ROWS, fc=FFN_CHUNK)
    starts = jnp.concatenate([cum_t[:, ::TM_ROW].T, jnp.full((1, N_EXPERTS), cap, I32)], axis=0)
    return moe_combine(starts, x1, afft.T, slot_t.T, ye, tm=TM_ROW, rows=COMBINE_ROWS)


_LAYER_PARAMS = ("ssm_d", "q_gain", "k_gain", "lambda_q1",
                 "lambda_k1", "lambda_q2", "lambda_k2", "subln_gain", "w_out",
                 "norm_mix", "norm_ffn", "w_router")
_STACKED_PARAMS = ("w_in", "w_expert_gate", "w_expert_up", "w_expert_down", "w_glu", "w_ssm_branch",
                   "w_attn_branch")


def kernel(x, w_in, ssm_a_re, ssm_a_im, ssm_log_dt, ssm_b_re, ssm_b_im, ssm_c_re, ssm_c_im, ssm_d, w_glu, w_ssm_branch, q_gain, k_gain, lambda_q1, lambda_k1, lambda_q2, lambda_k2, subln_gain, w_attn_branch, rel_bias, w_out, norm_mix, norm_ffn, w_router, w_expert_gate, w_expert_up, w_expert_down):
    args = dict(locals())
    b = x.shape[0]
    bias = bias_tiles(rel_bias, tk=ATTN_TILE, tq=ATTN_TILE)
    far_bias = attn_bias_consts(rel_bias)
    s5_ops = jax.vmap(s5_operators)(ssm_a_re, ssm_a_im, ssm_log_dt, ssm_b_re, ssm_b_im, ssm_c_re, ssm_c_im)
    outs = []
    for bi in range(b):
        xb = x[bi].astype(F32)
        for l in range(DEPTH):
            p = {k: args[k][l] for k in _LAYER_PARAMS}
            p.update({k: args[k] for k in _STACKED_PARAMS})
            p["s5_ops"] = s5_ops
            xb = _layer(xb, l, p, bias, far_bias)
        outs.append(xb)
    return jnp.stack(outs, axis=0).astype(x.dtype)
```

```python
import functools
import math

import jax
import jax.numpy as jnp
from jax import lax
from jax.experimental import pallas as pl
from jax.experimental.pallas import tpu as pltpu

F32 = jnp.float32
BF16 = jnp.bfloat16
I32 = jnp.int32

D_MODEL = 2048
DEPTH = 2
SSM_WIDTH = D_MODEL // 2
SSM_GROUP = 16
SSM_GROUPS = SSM_WIDTH // SSM_GROUP
SSM_STATE = 64
ATTN_HEADS = 8
QK_DIM = 64
V_DIM = 2 * QK_DIM
ATTN_WIDTH = ATTN_HEADS * V_DIM
QK_COLS = ATTN_HEADS * 2 * QK_DIM
REL_BUCKETS = 32
REL_MAX_DIST = 128
N_EXPERTS = 16
CAPACITY_FACTOR = 2
EXPERT_FF = D_MODEL
NORM_EPS = 1e-6
IN_COLS = SSM_WIDTH + 2 * QK_COLS + ATTN_WIDTH + 2 * D_MODEL
COL_Q = SSM_WIDTH
COL_K = COL_Q + QK_COLS
COL_V = COL_K + QK_COLS
COL_GS = COL_V + ATTN_WIDTH
COL_GA = COL_GS + D_MODEL

V7X_LANES = 128
F32_SUBLANES = 8
V7X_VMEM_BYTES = 64 * 1024 * 1024
VMEM_LIMIT = V7X_VMEM_BYTES - 8 * 1024 * 1024
LOG2E = 1.4426950408889634

S5_CHUNK = 16
S5_GB = 8
NEG_BIG = -1e30


def _cp(*sem):
    return pltpu.CompilerParams(dimension_semantics=sem, vmem_limit_bytes=VMEM_LIMIT)


def _norm_matmul_kernel(x_ref, g_ref, w_ref, o_ref, xcat_ref, h_ref, u_ref):
    j = pl.program_id(1)

    @pl.when(j == 0)
    def _():
        x = x_ref[...]
        r = lax.rsqrt(jnp.mean(x * x, axis=-1, keepdims=True) + NORM_EPS)
        h_ref[...] = (x * r * g_ref[...]).astype(BF16)

    res = jnp.dot(h_ref[...], w_ref[...].astype(BF16), preferred_element_type=F32)
    o_ref[...] = res.astype(o_ref.dtype)

    @pl.when(j == 0)
    def _():
        cn = u_ref.shape[1] // S5_CHUNK
        for slab in range(u_ref.shape[0]):
            u_ref[slab] = res[:, slab * V7X_LANES:(slab + 1) * V7X_LANES]
            for t in range(S5_CHUNK):
                piece = u_ref[slab, pl.ds(t, cn, stride=S5_CHUNK), :]
                xcat_ref[slab, :, t * V7X_LANES:(t + 1) * V7X_LANES] = piece.astype(xcat_ref.dtype)


def norm_matmul(x, gain, w, layer, *, tm, tn):
    s, d = x.shape
    n = w.shape[2]
    assert tn == SSM_WIDTH and tm % (S5_CHUNK * F32_SUBLANES) == 0
    nslab = SSM_WIDTH // V7X_LANES
    return pl.pallas_call(
        _norm_matmul_kernel,
        out_shape=(jax.ShapeDtypeStruct((s, n), BF16),
                   jax.ShapeDtypeStruct((nslab, s // S5_CHUNK, S5_CHUNK * V7X_LANES), BF16)),
        grid=(s // tm, n // tn),
        in_specs=[pl.BlockSpec((tm, d), lambda i, j: (i, 0)),
                  pl.BlockSpec((1, d), lambda i, j: (0, 0)),
                  pl.BlockSpec((None, d, tn), lambda i, j: (layer, 0, j))],
        out_specs=(pl.BlockSpec((tm, tn), lambda i, j: (i, j)),
                   pl.BlockSpec((nslab, tm // S5_CHUNK, S5_CHUNK * V7X_LANES), lambda i, j: (0, i, 0))),
        scratch_shapes=[pltpu.VMEM((tm, d), BF16), pltpu.VMEM((nslab, tm, V7X_LANES), F32)],
        compiler_params=_cp("parallel", "arbitrary"),
        name="norm_in_proj",
    )(x, gain.reshape(1, d).astype(F32), w)


def _attn_prep_kernel(q_ref, k_ref, v_ref, gq_ref, gk_ref, qo_ref, ko_ref, vo_ref):
    lane = lax.broadcasted_iota(I32, (1, V7X_LANES), 1)
    lo_mask = lane < QK_DIM

    def norm(src_ref, g_ref, a):
        x = src_ref[:, a * V7X_LANES:(a + 1) * V7X_LANES].astype(F32)
        ss = x * x
        lo = jnp.sum(jnp.where(lo_mask, ss, 0.0), axis=-1, keepdims=True)
        hi = jnp.sum(jnp.where(lo_mask, 0.0, ss), axis=-1, keepdims=True)
        ms = jnp.where(lo_mask, lo, hi) * (1.0 / QK_DIM)
        return x * lax.rsqrt(ms + NORM_EPS) * g_ref[...]

    for a in range(ATTN_HEADS):
        qo_ref[:, a * V7X_LANES:(a + 1) * V7X_LANES] = norm(q_ref, gq_ref, a).astype(qo_ref.dtype)
        ko_ref[a, 0] = norm(k_ref, gk_ref, a).astype(ko_ref.dtype)
        v = v_ref[:, a * V7X_LANES:(a + 1) * V7X_LANES].astype(F32)
        vo_ref[a, 0] = v.T.astype(vo_ref.dtype)


def attn_prep(proj, q_gain, k_gain, *, tk):
    s = proj.shape[0]
    nk = s // tk
    gq = (jnp.tile(q_gain.astype(F32), 2) * (QK_DIM ** -0.5 * LOG2E)).reshape(1, V7X_LANES)
    gk = jnp.tile(k_gain.astype(F32), 2).reshape(1, V7X_LANES)
    cq, ck, cv = COL_Q // QK_COLS, COL_K // QK_COLS, COL_V // ATTN_WIDTH
    qn, k4, vt4 = pl.pallas_call(
        _attn_prep_kernel,
        out_shape=(jax.ShapeDtypeStruct((s, QK_COLS), BF16),
                   jax.ShapeDtypeStruct((ATTN_HEADS, nk, tk, V_DIM), BF16),
                   jax.ShapeDtypeStruct((ATTN_HEADS, nk, V_DIM, tk), BF16)),
        grid=(nk,),
        in_specs=[pl.BlockSpec((tk, QK_COLS), lambda i: (i, cq)),
                  pl.BlockSpec((tk, QK_COLS), lambda i: (i, ck)),
                  pl.BlockSpec((tk, ATTN_WIDTH), lambda i: (i, cv)),
                  pl.BlockSpec((1, V7X_LANES), lambda i: (0, 0)),
                  pl.BlockSpec((1, V7X_LANES), lambda i: (0, 0))],
        out_specs=(pl.BlockSpec((tk, QK_COLS), lambda i: (i, 0)),
                   pl.BlockSpec((ATTN_HEADS, 1, tk, V_DIM), lambda i: (0, i, 0, 0)),
                   pl.BlockSpec((ATTN_HEADS, 1, V_DIM, tk), lambda i: (0, i, 0, 0))),
        compiler_params=_cp("parallel"),
        name="attn_prep",
    )(proj, proj, proj, gq, gk)
    kmax = math.sqrt(QK_DIM) * (1.0 + 2.0 ** -7) * jnp.max(jnp.abs(k_gain.astype(F32)))
    return qn, k4, vt4, jnp.full((ATTN_HEADS, 2), kmax, F32)


ATTN_NEAR = 3
ATTN_BIAS_TILES = 5
ATTN_UNROLL = 6


def _bias_tile_kernel(rb_ref, o_ref, *, tk, tq):
    h = pl.program_id(0)
    d = pl.program_id(1)
    w = tk + tq
    m = lax.broadcasted_iota(I32, (F32_SUBLANES, w), 1)
    rel = (d - ATTN_BIAS_TILES // 2) * tk + (tk - 1) - m
    half = REL_BUCKETS // 2
    exact = half // 2
    side = jnp.where(rel > 0, half, 0).astype(I32)
    n = jnp.abs(rel)
    nf = jnp.maximum(n, 1).astype(F32)
    large = exact + (jnp.log(nf / exact) / math.log(REL_MAX_DIST / exact) * (half - exact)).astype(I32)
    large = jnp.minimum(large, half - 1)
    bucket = side + jnp.where(n < exact, n, large).astype(I32)
    val = jnp.zeros((F32_SUBLANES, w), F32)
    for b in range(REL_BUCKETS):
        val = jnp.where(bucket == b, rb_ref[b, h], val)
    table = jnp.broadcast_to(val[0:1, :] * LOG2E, (tk, w))
    o_ref[0, 0] = pltpu.roll(table, w - tk + 1, 1, stride=1, stride_axis=0)[:, :tq]


def bias_tiles(rel_bias, *, tk, tq):
    assert tk == tq and (tk + tq) & (tk + tq - 1) == 0
    return pl.pallas_call(
        functools.partial(_bias_tile_kernel, tk=tk, tq=tq),
        out_shape=jax.ShapeDtypeStruct((ATTN_HEADS, ATTN_BIAS_TILES, tk, tq), F32),
        grid=(ATTN_HEADS, ATTN_BIAS_TILES),
        in_specs=[pl.BlockSpec(memory_space=pltpu.SMEM)],
        out_specs=pl.BlockSpec((1, 1, tk, tq), lambda h, d: (h, d, 0, 0)),
        compiler_params=_cp("parallel", "parallel"),
        name="t5_bias_tiles",
    )(rel_bias.astype(F32))


FAST_MIN_SUM = 2.0 ** -90


def _attn_kernel(lam_ref, far_ref, kmax_ref, q_ref, k_ref, vt_ref, bias_ref, g_ref, o_ref,
                 s_ref, cm_ref, m_ref, l_ref, acc_ref, p_ref, *, nk, tk, tq, out_scale):
    h = pl.program_id(0)
    i = pl.program_id(1)
    lane = lax.broadcasted_iota(I32, (1, V7X_LANES), 1)
    q = q_ref[...]
    zero = jnp.zeros_like(q)
    qmaps = (jnp.where(lane < QK_DIM, q, zero), jnp.where(lane < QK_DIM, zero, q))

    n0 = jnp.clip(i - 1, 0, nk - ATTN_NEAR)
    nfar = nk - ATTN_NEAR

    def far_tile(t):
        return jnp.where(t < n0, t, t + ATTN_NEAR)

    def far_const(j):
        return jnp.where(j < i, far_ref[h, 0], far_ref[h, 1])

    def near_bias(j):
        return bias_ref[0, j - i + ATTN_BIAS_TILES // 2]

    qf = q.astype(F32)
    row8 = lax.broadcasted_iota(I32, (F32_SUBLANES, V7X_LANES), 0)
    lane8 = lax.broadcasted_iota(I32, (F32_SUBLANES, V7X_LANES), 1)
    pick = jnp.where(jnp.logical_or(jnp.logical_and(row8 == 0, lane8 < QK_DIM),
                                    jnp.logical_and(row8 == 1, lane8 >= QK_DIM)), 1.0, 0.0)
    qn2 = lax.dot_general(pick, qf * qf, (((1,), (1,)), ((), ())), preferred_element_type=F32,
                          precision=lax.Precision.HIGHEST)
    shift = [jnp.sqrt(qn2[c:c + 1, :]) * kmax_ref[h, c] + far_ref[h, 2] for c in range(2)]
    l_ref[...] = jnp.zeros(l_ref.shape, F32)
    acc_ref[...] = jnp.zeros(acc_ref.shape, F32)

    def fast_probs(j, slot, bias_tile, bias_const):
        kt = k_ref[0, j]
        for c in range(2):
            s = lax.dot_general(kt, qmaps[c], (((1,), (1,)), ((), ())), preferred_element_type=F32)
            if bias_tile is not None:
                s = s + bias_tile
            p = jnp.exp2(s - (shift[c] - bias_const))
            l_ref[c:c + 1, :] += jnp.sum(p, axis=0, keepdims=True)
            p_ref[slot, c] = p.astype(BF16)

    def fast_pv(j, slot):
        vt = vt_ref[0, j]
        for c in range(2):
            acc_ref[c] += jnp.dot(vt, p_ref[slot, c], preferred_element_type=F32)

    fast_probs(n0, 0, near_bias(n0), 0.0)
    for w in range(1, ATTN_NEAR):
        fast_probs(n0 + w, w % 2, near_bias(n0 + w), 0.0)
        fast_pv(n0 + w - 1, (w - 1) % 2)
    j0 = far_tile(0)
    fast_probs(j0, ATTN_NEAR % 2, None, far_const(j0))
    fast_pv(n0 + ATTN_NEAR - 1, (ATTN_NEAR - 1) % 2)

    def fast_group(u, jprev):
        for w in range(ATTN_UNROLL):
            jn = far_tile(ATTN_UNROLL * u + w + 1)
            fast_probs(jn, (ATTN_NEAR + 1 + w) % 2, None, far_const(jn))
            fast_pv(jprev, (ATTN_NEAR + w) % 2)
            jprev = jn
        return jprev

    jl = lax.fori_loop(0, (nfar - 1) // ATTN_UNROLL, fast_group, j0)
    fast_pv(jl, (ATTN_NEAR + nfar - 1) % 2)

    lmin = jnp.min(jnp.minimum(l_ref[0:1, :], l_ref[1:2, :]))

    @pl.when(jnp.logical_not(lmin >= FAST_MIN_SUM))
    def _():
        _attn_running_max(h, i, n0, nfar, far_tile, far_const, near_bias, qmaps, k_ref, vt_ref,
                          s_ref, cm_ref, m_ref, l_ref, acc_ref, tq=tq)

    o1 = acc_ref[0] / l_ref[0:1, :]
    o2 = acc_ref[1] / l_ref[1:2, :]
    o = o1 - lam_ref[0] * o2
    r = lax.rsqrt(jnp.mean(o * o, axis=0, keepdims=True) + NORM_EPS)
    o = o * r * g_ref[...] * out_scale
    o_ref[...] = o.T.astype(o_ref.dtype)


def _attn_running_max(h, i, n0, nfar, far_tile, far_const, near_bias, qmaps, k_ref, vt_ref,
                      s_ref, cm_ref, m_ref, l_ref, acc_ref, *, tq):
    m_ref[...] = jnp.full(m_ref.shape, NEG_BIG, F32)
    l_ref[...] = jnp.zeros(l_ref.shape, F32)
    acc_ref[...] = jnp.zeros(acc_ref.shape, F32)

    def scores(j, slot, bias_tile, bias_const):
        kt = k_ref[0, j]
        for c in range(2):
            s = lax.dot_general(kt, qmaps[c], (((1,), (1,)), ((), ())),
                                preferred_element_type=F32)
            if bias_tile is not None:
                s = s + bias_tile
            s_ref[slot, c] = s
            cm_ref[slot, c:c + 1, :] = jnp.max(s, axis=0, keepdims=True) + bias_const
        cm_ref[slot, 2:3, :] = jnp.zeros((1, tq), F32) + bias_const

    def absorb(j, slot):
        vt = vt_ref[0, j]
        cb = cm_ref[slot, 2:3, :]
        for c in range(2):
            m_old = m_ref[c:c + 1, :]
            m_new = jnp.maximum(m_old, cm_ref[slot, c:c + 1, :])
            alpha = jnp.exp2(m_old - m_new)
            p = jnp.exp2(s_ref[slot, c] - (m_new - cb))
            l_ref[c:c + 1, :] = alpha * l_ref[c:c + 1, :] + jnp.sum(p, axis=0, keepdims=True)
            acc_ref[c] = alpha * acc_ref[c] + jnp.dot(vt, p.astype(BF16), preferred_element_type=F32)
            m_ref[c:c + 1, :] = m_new

    scores(n0, 0, near_bias(n0), 0.0)
    scores(n0 + 1, 1, near_bias(n0 + 1), 0.0)
    absorb(n0, 0)
    scores(n0 + 2, 0, near_bias(n0 + 2), 0.0)
    absorb(n0 + 1, 1)
    j0 = far_tile(0)
    scores(j0, 1, None, far_const(j0))
    absorb(n0 + 2, 0)

    def group(u, jprev):
        for w in range(ATTN_UNROLL):
            jn = far_tile(ATTN_UNROLL * u + w + 1)
            scores(jn, w % 2, None, far_const(jn))
            absorb(jprev, (w + 1) % 2)
            jprev = jn
        return jprev

    jlast = lax.fori_loop(0, (nfar - 1) // ATTN_UNROLL, group, j0)
    absorb(jlast, 1)


def attn_bias_consts(rel_bias):
    half = REL_BUCKETS // 2
    rb = rel_bias.astype(F32) * LOG2E
    return jnp.stack([rb[half - 1], rb[REL_BUCKETS - 1], jnp.max(rb, axis=0)], axis=1)


def diff_attention(qn, k4, vt4, bias, far_bias, kmax, lam, subln_gain, *, lam_init, tq, tk):
    s = qn.shape[0]
    nk = s // tk
    assert tq == tk and nk > ATTN_NEAR and (nk - ATTN_NEAR - 1) % ATTN_UNROLL == 0
    kern = functools.partial(_attn_kernel, nk=nk, tk=tk, tq=tq, out_scale=1.0 - lam_init)
    return pl.pallas_call(
        kern,
        out_shape=jax.ShapeDtypeStruct((s, ATTN_WIDTH), BF16),
        grid=(ATTN_HEADS, s // tq),
        in_specs=[pl.BlockSpec(memory_space=pltpu.SMEM),
                  pl.BlockSpec(memory_space=pltpu.SMEM),
                  pl.BlockSpec(memory_space=pltpu.SMEM),
                  pl.BlockSpec((tq, V_DIM), lambda h, i: (i, h)),
                  pl.BlockSpec((1, nk, tk, V_DIM), lambda h, i: (h, 0, 0, 0)),
                  pl.BlockSpec((1, nk, V_DIM, tk), lambda h, i: (h, 0, 0, 0)),
                  pl.BlockSpec((1, ATTN_BIAS_TILES, tk, tq), lambda h, i: (h, 0, 0, 0)),
                  pl.BlockSpec((V_DIM, 1), lambda h, i: (0, 0))],
        out_specs=pl.BlockSpec((tq, V_DIM), lambda h, i: (i, h)),
        scratch_shapes=[pltpu.VMEM((2, 2, tk, tq), F32),
                        pltpu.VMEM((2, F32_SUBLANES, tq), F32),
                        pltpu.VMEM((F32_SUBLANES, tq), F32),
                        pltpu.VMEM((F32_SUBLANES, tq), F32),
                        pltpu.VMEM((2, V_DIM, tq), F32),
                        pltpu.VMEM((2, 2, tk, tq), BF16)],
        compiler_params=_cp("parallel", "arbitrary"),
        name="diff_attention",
    )(lam.reshape(1).astype(F32), far_bias, kmax, qn, k4, vt4, bias,
      subln_gain.reshape(V_DIM, 1).astype(F32))


def s5_operators(a_re, a_im, log_dt, b_re, b_im, c_re, c_im):
    t_len, hp = S5_CHUNK, lax.Precision.HIGHEST
    a_re, a_im = a_re.astype(F32), a_im.astype(F32)
    dt = jnp.exp(log_dt.astype(F32))[..., None]
    steps = jnp.arange(t_len + 1, dtype=F32)[:, None, None, None]
    mag = jnp.exp(a_re * dt * steps)
    ang = a_im * dt * steps
    pw_re, pw_im = mag * jnp.cos(ang), mag * jnp.sin(ang)
    den = a_re * a_re + a_im * a_im
    nr, ni = pw_re[1] - 1.0, pw_im[1]
    coef_re = ((nr * a_re + ni * a_im) / den)[..., None]
    coef_im = ((ni * a_re - nr * a_im) / den)[..., None]
    b_re, b_im = b_re.astype(F32), b_im.astype(F32)
    bb_re = coef_re * b_re - coef_im * b_im
    bb_im = coef_re * b_im + coef_im * b_re
    c_re, c_im = c_re.astype(F32), c_im.astype(F32)

    g, n_st, p_ch = a_re.shape[1], SSM_STATE, SSM_GROUP
    tp = t_len * p_ch
    pwt_re = jnp.transpose(pw_re, (1, 2, 3, 0))
    pwt_im = jnp.transpose(pw_im, (1, 2, 3, 0))
    ct_re = jnp.transpose(c_re, (0, 1, 3, 2))
    ct_im = jnp.transpose(c_im, (0, 1, 3, 2))
    bbt_re = jnp.transpose(bb_re, (0, 1, 3, 2))
    bbt_im = jnp.transpose(bb_im, (0, 1, 3, 2))
    cp_re = ct_re[:, :, :, None, :] * pwt_re[..., None] - ct_im[:, :, :, None, :] * pwt_im[..., None]
    cp_im = ct_re[:, :, :, None, :] * pwt_im[..., None] + ct_im[:, :, :, None, :] * pwt_re[..., None]

    zlag = jnp.zeros((g, n_st, t_len - 1, p_ch), F32)

    def lagged(cp):
        return (jnp.concatenate([zlag, cp[0, :, :, :t_len]], axis=2),
                jnp.concatenate([jnp.flip(cp[1, :, :, :t_len], axis=2), zlag], axis=2))

    rf_re, rb_re = lagged(cp_re)
    rf_im, rb_im = lagged(cp_im)
    r_cat = jnp.concatenate([rf_re, rf_im, rb_re, rb_im], axis=1).reshape(g, 4 * n_st, (2 * t_len - 1) * p_ch)
    a_cat = jnp.concatenate([bbt_re[0], -bbt_im[0], bbt_re[1], -bbt_im[1]], axis=-1)
    kp = jnp.einsum('gpk,gkx->gpx', a_cat, r_cat, precision=hp)
    toep = jnp.stack([kp[:, :, (t_len - 1 - j) * p_ch:(t_len - 1 - j) * p_ch + tp] for j in range(t_len)],
                     axis=1).reshape(g, tp, tp).astype(BF16)

    def seg_powers(d, reverse):
        pr = jnp.transpose(pw_re[:t_len, d], (1, 0, 2))
        pi = jnp.transpose(pw_im[:t_len, d], (1, 0, 2))
        if reverse:
            pr, pi = jnp.flip(pr, axis=1), jnp.flip(pi, axis=1)
        return pr, pi

    prf, pif = seg_powers(0, True)
    prb, pib = seg_powers(1, False)
    pa = jnp.concatenate([prf, pif, pif, prf, prb, pib, pib, prb], axis=-1)
    pb = jnp.concatenate([-pif, prf, prf, -pif, -pib, prb, prb, -pib], axis=-1)
    br = jnp.concatenate([bbt_re[0]] * 4 + [bbt_re[1]] * 4, axis=-1)
    bi = jnp.concatenate([bbt_im[0]] * 4 + [bbt_im[1]] * 4, axis=-1)
    smap = (br[:, None] * pa[:, :, None, :] + bi[:, None] * pb[:, :, None, :]).reshape(g, tp, 8 * n_st)
    smap = smap.astype(BF16)

    def out_map(d, reverse):
        wr, wi = cp_re[d][:, :, 1:t_len + 1], cp_im[d][:, :, 1:t_len + 1]
        if reverse:
            wr, wi = jnp.flip(wr, axis=2), jnp.flip(wi, axis=2)
        return [wr.reshape(g, n_st, tp), -wi.reshape(g, n_st, tp)]

    mc = jnp.concatenate(out_map(0, False) + out_map(1, True), axis=1).astype(BF16)

    def carry(d):
        ar, ai = pw_re[t_len, d], pw_im[t_len, d]
        return [jnp.concatenate([ar, ar], -1), jnp.concatenate([-ai, ai], -1),
                jnp.concatenate([ai, -ai], -1)]

    coef = jnp.stack(carry(0) + carry(1), axis=0)
    return toep, smap, mc, coef


def _gelu_tanh(x):
    return 0.5 * x * (1.0 + jnp.tanh(math.sqrt(2.0 / math.pi) * (x + 0.044715 * (x * x * x))))


def _s5_kernel(x_ref, sel_ref, toep_ref, smap_ref, mc_ref, coef_ref, d_ref, y_ref,
               u_ref, ef_ref, efs_ref, eb_ref, ebs_ref, *, cn):
    gb = toep_ref.shape[0]
    tp = toep_ref.shape[1]
    n2 = 2 * SSM_STATE
    xcat = x_ref[0]

    def sel(g):
        off = (gb - 1 - g) * SSM_GROUP
        return sel_ref[off:off + xcat.shape[1], :]

    for g in range(gb):
        u_ref[g] = jnp.dot(xcat, sel(g), preferred_element_type=F32).astype(BF16)
        e = jnp.dot(u_ref[g], smap_ref[g], preferred_element_type=F32)
        for r, ref in enumerate((ef_ref, efs_ref, eb_ref, ebs_ref)):
            ref[pl.ds(g, cn, stride=gb), :] = e[:, r * n2:(r + 1) * n2]

    cf, cfs, cfw = coef_ref[0], coef_ref[1], coef_ref[2]
    cb, cbs, cbw = coef_ref[3], coef_ref[4], coef_ref[5]

    def step(c, carry):
        s, sw, r, rw = carry
        fo = pl.multiple_of(c * gb, gb)
        bo = pl.multiple_of((cn - 1 - c) * gb, gb)
        e, es = ef_ref[pl.ds(fo, gb), :], efs_ref[pl.ds(fo, gb), :]
        ef_ref[pl.ds(fo, gb), :] = s
        s, sw = cf * s + cfs * sw + e, cf * sw + cfw * s + es
        e, es = eb_ref[pl.ds(bo, gb), :], ebs_ref[pl.ds(bo, gb), :]
        eb_ref[pl.ds(bo, gb), :] = r
        r, rw = cb * r + cbs * rw + e, cb * rw + cbw * r + es
        return s, sw, r, rw

    z = jnp.zeros((gb, n2), F32)
    lax.fori_loop(0, cn, step, (z, z, z, z))

    ycat = None
    for g in range(gb):
        u = u_ref[g]
        st = jnp.concatenate([ef_ref[pl.ds(g, cn, stride=gb), :], eb_ref[pl.ds(g, cn, stride=gb), :]],
                             axis=1).astype(BF16)
        y = (jnp.dot(u, toep_ref[g], preferred_element_type=F32)
             + jnp.dot(st, mc_ref[g], preferred_element_type=F32)
             + u.astype(F32) * d_ref[g])
        placed = lax.dot_general(_gelu_tanh(y).astype(BF16), sel(g), (((1,), (1,)), ((), ())),
                                 preferred_element_type=F32)
        ycat = placed if ycat is None else ycat + placed
    y_ref[0] = ycat.astype(y_ref.dtype)


def s5_lane_selector():
    gb, p, t_len = S5_GB, SSM_GROUP, S5_CHUNK
    r = jnp.arange(t_len * V7X_LANES + (gb - 1) * p)[:, None] - (gb - 1) * p
    c = jnp.arange(t_len * p)[None, :]
    hit = (r >= 0) & (r // V7X_LANES == c // p) & ((r % V7X_LANES) // p == 0) & (r % p == c % p)
    return hit.astype(BF16)


def s5_mixer(xcat, ops, ssm_d, layer):
    toep, smap, mc, coef = ops
    g, p, t_len = SSM_GROUPS, SSM_GROUP, S5_CHUNK
    nslab, cn, _ = xcat.shape
    tp = t_len * p
    gb = S5_GB
    assert gb * p == V7X_LANES and nslab == g // gb
    sel = s5_lane_selector()
    dsk = jnp.tile(ssm_d.astype(F32).reshape(g, 1, p), (1, 1, t_len))
    ycat = pl.pallas_call(
        functools.partial(_s5_kernel, cn=cn),
        out_shape=jax.ShapeDtypeStruct((nslab, cn, t_len * V7X_LANES), BF16),
        grid=(nslab,),
        in_specs=[pl.BlockSpec((1, cn, t_len * V7X_LANES), lambda i: (i, 0, 0)),
                  pl.BlockSpec(sel.shape, lambda i: (0, 0)),
                  pl.BlockSpec((None, gb, tp, tp), lambda i: (layer, i, 0, 0)),
                  pl.BlockSpec((None, gb, tp, smap.shape[3]), lambda i: (layer, i, 0, 0)),
                  pl.BlockSpec((None, gb, mc.shape[2], tp), lambda i: (layer, i, 0, 0)),
                  pl.BlockSpec((None, 6, gb, 2 * SSM_STATE), lambda i: (layer, 0, i, 0)),
                  pl.BlockSpec((gb, 1, tp), lambda i: (i, 0, 0))],
        out_specs=pl.BlockSpec((1, cn, t_len * V7X_LANES), lambda i: (i, 0, 0)),
        scratch_shapes=[pltpu.VMEM((gb, cn, tp), BF16)]
        + [pltpu.VMEM((cn * gb, 2 * SSM_STATE), F32) for _ in range(4)],
        compiler_params=_cp("parallel"),
        name="s5_chunked_scan",
    )(xcat, sel, toep, smap, mc, coef, dsk)
    return ycat


def _sigmoid(x):
    return 1.0 / (1.0 + jnp.exp(-x))


def _glu_kernel(ycat_ref, w_ref, o_ref, y_ref):
    cn = ycat_ref.shape[1]
    for slab in range(ycat_ref.shape[0]):
        for t in range(S5_CHUNK):
            piece = ycat_ref[slab, :, t * V7X_LANES:(t + 1) * V7X_LANES].astype(F32)
            y_ref[slab, pl.ds(t, cn, stride=S5_CHUNK), :] = piece
    y = jnp.concatenate([y_ref[slab] for slab in range(ycat_ref.shape[0])], axis=1)
    z = jnp.dot(y.astype(BF16), w_ref[0].astype(BF16), preferred_element_type=F32)
    o_ref[...] = (y * _sigmoid(z)).astype(o_ref.dtype)


def glu(ycat, w, layer, *, tm):
    nslab, cn, _ = ycat.shape
    s, d = cn * S5_CHUNK, nslab * V7X_LANES
    return pl.pallas_call(
        _glu_kernel,
        out_shape=jax.ShapeDtypeStruct((s, d), BF16),
        grid=(s // tm,),
        in_specs=[pl.BlockSpec((nslab, tm // S5_CHUNK, S5_CHUNK * V7X_LANES), lambda i: (0, i, 0)),
                  pl.BlockSpec((1, d, d), lambda i: (layer, 0, 0))],
        out_specs=pl.BlockSpec((tm, d), lambda i: (i, 0)),
        scratch_shapes=[pltpu.VMEM((nslab, tm, V7X_LANES), F32)],
        compiler_params=_cp("parallel"),
        name="half_glu",
    )(ycat, w)


def _merge_kernel(yg_ref, ao_ref, gs_ref, ga_ref, ws_ref, wa_ref, o_ref):
    a = jnp.dot(yg_ref[...], ws_ref[0].astype(BF16), preferred_element_type=F32)
    b = jnp.dot(ao_ref[...], wa_ref[0].astype(BF16), preferred_element_type=F32)
    o = _sigmoid(gs_ref[...].astype(F32)) * a + _sigmoid(ga_ref[...].astype(F32)) * b
    o_ref[...] = o.astype(o_ref.dtype)


def gated_merge(yg, ao, proj, ws, wa, layer, *, tm, tn):
    s, k = yg.shape
    n = ws.shape[2]
    cs, ca = COL_GS // tn, COL_GA // tn
    return pl.pallas_call(
        _merge_kernel,
        out_shape=jax.ShapeDtypeStruct((s, n), BF16),
        grid=(s // tm, n // tn),
        in_specs=[pl.BlockSpec((tm, k), lambda i, j: (i, 0)),
                  pl.BlockSpec((tm, k), lambda i, j: (i, 0)),
                  pl.BlockSpec((tm, tn), lambda i, j: (i, cs + j)),
                  pl.BlockSpec((tm, tn), lambda i, j: (i, ca + j)),
                  pl.BlockSpec((1, k, tn), lambda i, j: (layer, 0, j)),
                  pl.BlockSpec((1, k, tn), lambda i, j: (layer, 0, j))],
        out_specs=pl.BlockSpec((tm, tn), lambda i, j: (i, j)),
        compiler_params=_cp("parallel", "arbitrary"),
        name="gated_merge",
    )(yg, ao, proj, proj, ws, wa)


def _out_router_kernel(m_ref, x_ref, w_ref, g_ref, wrt_ref, xo_ref, h_ref, afft_ref):
    x1 = x_ref[...] + jnp.dot(m_ref[...], w_ref[...], preferred_element_type=F32)
    xo_ref[...] = x1
    r = lax.rsqrt(jnp.mean(x1 * x1, axis=-1, keepdims=True) + NORM_EPS)
    h = x1 * r * g_ref[...]
    h_ref[...] = h
    lgt = lax.dot_general(wrt_ref[...], h, (((1,), (1,)), ((), ())),
                          preferred_element_type=F32, precision=lax.Precision.HIGHEST)
    et = jnp.exp(lgt - jnp.max(lgt, axis=0, keepdims=True))
    afft_ref[...] = et / jnp.sum(et, axis=0, keepdims=True)


def out_proj_router(merged, x, w_out, gain, w_router, *, tm):
    s, d = x.shape
    e = w_router.shape[1]
    return pl.pallas_call(
        _out_router_kernel,
        out_shape=(jax.ShapeDtypeStruct((s, d), F32), jax.ShapeDtypeStruct((s, d), F32),
                   jax.ShapeDtypeStruct((e, s), F32)),
        grid=(s // tm,),
        in_specs=[pl.BlockSpec((tm, d), lambda i: (i, 0)),
                  pl.BlockSpec((tm, d), lambda i: (i, 0)),
                  pl.BlockSpec((d, d), lambda i: (0, 0)),
                  pl.BlockSpec((1, d), lambda i: (0, 0)),
                  pl.BlockSpec((e, d), lambda i: (0, 0))],
        out_specs=(pl.BlockSpec((tm, d), lambda i: (i, 0)),
                   pl.BlockSpec((tm, d), lambda i: (i, 0)),
                   pl.BlockSpec((e, tm), lambda i: (0, i))),
        compiler_params=_cp("parallel"),
        name="out_proj_router",
    )(merged, x, w_out, gain.reshape(1, d).astype(F32), w_router.astype(F32).T)


SLOT_LO_BITS = 6
SLOT_LO = 1 << SLOT_LO_BITS


def _select_kernel(afft_ref, slot_ref, cum_ref, idx_ref, acc_ref, *, s, cap, blk):
    ne = afft_ref.shape[0]

    def bit_body(b, thr):
        cand = thr | jnp.left_shift(jnp.ones((ne, 1), I32), 30 - b)
        keys = pltpu.bitcast(afft_ref[...], I32)
        cnt = jnp.sum((keys >= cand).astype(I32), axis=1, keepdims=True)
        return jnp.where(cnt >= cap, cand, thr)

    thr = lax.fori_loop(0, 31, bit_body, jnp.zeros((ne, 1), I32))
    keys = pltpu.bitcast(afft_ref[...], I32)
    need = cap - jnp.sum((keys > thr).astype(I32), axis=1, keepdims=True)

    ri = lax.broadcasted_iota(I32, (blk, blk), 0)
    ci = lax.broadcasted_iota(I32, (blk, blk), 1)
    upper = jnp.where(ri < ci, 1.0, 0.0).astype(BF16)
    na = cap // SLOT_LO
    acol = lax.broadcasted_iota(I32, (na, 1), 0)
    bcol = lax.broadcasted_iota(I32, (SLOT_LO, 1), 0)
    tlane = lax.broadcasted_iota(I32, (1, blk), 1)
    acc_ref[...] = jnp.zeros(acc_ref.shape, F32)

    def blk_body(b, carry):
        ceq, csel = carry
        off = pl.multiple_of(b * blk, blk)
        kb = pltpu.bitcast(afft_ref[:, pl.ds(off, blk)], I32)
        gt = kb > thr
        eq = kb == thr
        eqf = jnp.where(eq, 1.0, 0.0)
        rank_eq = jnp.dot(eqf.astype(BF16), upper, preferred_element_type=F32) + ceq
        sel = jnp.logical_or(gt, jnp.logical_and(eq, rank_eq < need.astype(F32)))
        self_ = jnp.where(sel, 1.0, 0.0)
        cum = jnp.dot(self_.astype(BF16), upper, preferred_element_type=F32) + csel
        cum_i = cum.astype(I32)
        cum_ref[:, pl.ds(off, blk)] = cum_i
        slot = jnp.where(sel, cum_i, -1)
        slot_ref[:, pl.ds(off, blk)] = slot
        tok = off + tlane
        hi = (tok // V7X_LANES).astype(F32)
        lo = (tok % V7X_LANES).astype(F32)
        for e in range(ne):
            srow = slot[e:e + 1, :]
            in_a = lax.shift_right_arithmetic(srow, SLOT_LO_BITS) == acol
            lhs = jnp.concatenate([jnp.where(in_a, hi, 0.0), jnp.where(in_a, lo, 0.0)],
                                  axis=0).astype(BF16)
            rhs = jnp.where((srow & (SLOT_LO - 1)) == bcol, 1.0, 0.0).astype(BF16)
            acc_ref[e] += lax.dot_general(lhs, rhs, (((1,), (1,)), ((), ())), preferred_element_type=F32)
        return (ceq + jnp.sum(eqf, axis=1, keepdims=True), csel + jnp.sum(self_, axis=1, keepdims=True))

    z = jnp.zeros((ne, 1), F32)
    lax.fori_loop(0, s // blk, blk_body, (z, z))
    a = acc_ref[...]
    idx_ref[...] = (a[:, :na, :] * float(V7X_LANES) + a[:, na:, :]).astype(I32)


def expert_select(afft, *, cap, blk):
    ne, s = afft.shape
    assert cap % SLOT_LO == 0
    return pl.pallas_call(
        functools.partial(_select_kernel, s=s, cap=cap, blk=blk),
        out_shape=(jax.ShapeDtypeStruct((ne, s), I32), jax.ShapeDtypeStruct((ne, s), I32),
                   jax.ShapeDtypeStruct((ne, cap // SLOT_LO, SLOT_LO), I32)),
        scratch_shapes=[pltpu.VMEM((ne, 2 * (cap // SLOT_LO), SLOT_LO), F32)],
        compiler_params=pltpu.CompilerParams(vmem_limit_bytes=VMEM_LIMIT),
        name="expert_select",
    )(afft)


GATHER_UNROLL = 8


def _ffn_kernel(idx_ref, h_hbm, wg_ref, wu_ref, wd_ref, y_ref, xg32_ref, xg_ref, sem, *, cap):
    e = pl.program_id(0)
    f = pl.program_id(1)

    def row_copy(r):
        tok = idx_ref[e * cap + r]
        return pltpu.make_async_copy(h_hbm.at[pl.ds(tok, 1)], xg32_ref.at[pl.ds(r, 1)], sem)

    @pl.when(f == 0)
    def _():
        def start(rb, c):
            for w in range(GATHER_UNROLL):
                row_copy(rb * GATHER_UNROLL + w).start()
            return c

        lax.fori_loop(0, cap // GATHER_UNROLL, start, 0)
        y_ref[...] = jnp.zeros(y_ref.shape, F32)
        pltpu.make_async_copy(h_hbm.at[pl.ds(0, cap)], xg32_ref, sem).wait()
        xg_ref[...] = xg32_ref[...].astype(BF16)

    xg = xg_ref[...]
    a = jnp.dot(xg, wg_ref[0, 0].astype(BF16), preferred_element_type=F32)
    b = jnp.dot(xg, wu_ref[0, 0].astype(BF16), preferred_element_type=F32)
    hid = (a * _sigmoid(a) * b).astype(BF16)
    y_ref[0, 0:cap, :] += jnp.dot(hid, wd_ref[0, 0].astype(BF16), preferred_element_type=F32)


def expert_ffn(idx, h2, wg, wu, wd, layer, *, cap, pad, fc):
    _, ne, d, ff = wg.shape
    grid_spec = pltpu.PrefetchScalarGridSpec(
        num_scalar_prefetch=1,
        grid=(ne, ff // fc),
        in_specs=[pl.BlockSpec(memory_space=pl.ANY),
                  pl.BlockSpec((1, 1, d, fc), lambda e, f, idx: (layer, e, 0, f)),
                  pl.BlockSpec((1, 1, d, fc), lambda e, f, idx: (layer, e, 0, f)),
                  pl.BlockSpec((1, 1, fc, d), lambda e, f, idx: (layer, e, f, 0))],
        out_specs=pl.BlockSpec((1, cap + pad, d), lambda e, f, idx: (e, 0, 0)),
        scratch_shapes=[pltpu.VMEM((cap, d), F32), pltpu.VMEM((cap, d), BF16),
                        pltpu.SemaphoreType.DMA(())],
    )
    return pl.pallas_call(
        functools.partial(_ffn_kernel, cap=cap),
        out_shape=jax.ShapeDtypeStruct((ne, cap + pad, d), F32),
        grid_spec=grid_spec,
        compiler_params=_cp("arbitrary", "arbitrary"),
        name="expert_ffn",
    )(idx.reshape(-1), h2, wg, wu, wd)


def _combine_kernel(st_ref, x_ref, aff_ref, slot_ref, y_hbm, o_ref, buf_ref, xbuf_ref, sem, xsem,
                    *, ne, rows, nt):
    t = pl.program_id(0)
    tm = x_ref.shape[0]

    def chunk_start(tt, e, c):
        st8 = (st_ref[tt * ne + e] // F32_SUBLANES) * F32_SUBLANES
        return pl.multiple_of(st8 + c * rows, F32_SUBLANES)

    def first_copy(tt, e, slot):
        return pltpu.make_async_copy(y_hbm.at[e, pl.ds(chunk_start(tt, e, 0), rows)],
                                     buf_ref.at[slot, e], sem.at[slot, e])

    @pl.when(t == 0)
    def _():
        for e in range(ne):
            first_copy(0, e, 0).start()

    @pl.when(t + 1 < nt)
    def _():
        for e in range(ne):
            first_copy(t + 1, e, (t + 1) % 2).start()

    slot = t % 2
    for e in range(ne):
        first_copy(t, e, slot).wait()

    pair = V7X_LANES // rows
    lane = lax.broadcasted_iota(I32, (1, V7X_LANES), 1)
    hi_parts, lo_parts = [], []
    for a in range(ne // pair):
        rel = jnp.zeros((tm, V7X_LANES), I32)
        gate = jnp.zeros((tm, V7X_LANES), F32)
        for b in range(pair):
            e = a * pair + b
            in_e = jnp.logical_and(lane >= b * rows, lane < (b + 1) * rows)
            rel = jnp.where(in_e, slot_ref[:, e:e + 1] - chunk_start(t, e, 0) + b * rows, rel)
            gate = jnp.where(in_e, aff_ref[:, e:e + 1], gate)
        gate = jnp.where(rel == lane, gate, 0.0)
        g_hi = gate.astype(BF16)
        hi_parts.append(g_hi)
        lo_parts.append((gate - g_hi.astype(F32)).astype(BF16))
    rhs = buf_ref[slot].reshape(ne * rows, x_ref.shape[1]).astype(BF16)
    o_ref[...] = (x_ref[...]
                  + jnp.dot(jnp.concatenate(hi_parts, axis=1), rhs, preferred_element_type=F32)
                  + jnp.dot(jnp.concatenate(lo_parts, axis=1), rhs, preferred_element_type=F32))

    lane_r = lax.broadcasted_iota(I32, (1, rows), 1)
    ends = [st_ref[(t + 1) * ne + e] for e in range(ne)]
    any_extra = functools.reduce(jnp.logical_or, [ends[e] > chunk_start(t, e, 1) for e in range(ne)])

    @pl.when(any_extra)
    def _():
        for e in range(ne):
            for c in range(1, tm // rows + 1):

                @pl.when(ends[e] > chunk_start(t, e, c))
                def _():
                    cp = pltpu.make_async_copy(y_hbm.at[e, pl.ds(chunk_start(t, e, c), rows)],
                                               xbuf_ref, xsem)
                    cp.start()
                    cp.wait()
                    rel = slot_ref[:, e:e + 1] - chunk_start(t, e, c)
                    oh = jnp.where(rel == lane_r, 1.0, 0.0).astype(BF16)
                    contrib = jnp.dot(oh, xbuf_ref[...].astype(BF16), preferred_element_type=F32)
                    o_ref[...] += aff_ref[:, e:e + 1] * contrib


def moe_combine(starts, x1, aff, slot, ye, *, tm, rows):
    s, d = x1.shape
    ne = aff.shape[1]
    nt = s // tm
    assert V7X_LANES % rows == 0 and ne % (V7X_LANES // rows) == 0
    grid_spec = pltpu.PrefetchScalarGridSpec(
        num_scalar_prefetch=1,
        grid=(nt,),
        in_specs=[pl.BlockSpec((tm, d), lambda t, st: (t, 0)),
                  pl.BlockSpec((tm, ne), lambda t, st: (t, 0)),
                  pl.BlockSpec((tm, ne), lambda t, st: (t, 0)),
                  pl.BlockSpec(memory_space=pl.ANY)],
        out_specs=pl.BlockSpec((tm, d), lambda t, st: (t, 0)),
        scratch_shapes=[pltpu.VMEM((2, ne, rows, d), F32), pltpu.VMEM((rows, d), F32),
                        pltpu.SemaphoreType.DMA((2, ne)), pltpu.SemaphoreType.DMA(())],
    )
    return pl.pallas_call(
        functools.partial(_combine_kernel, ne=ne, rows=rows, nt=nt),
        out_shape=jax.ShapeDtypeStruct((s, d), F32),
        grid_spec=grid_spec,
        compiler_params=_cp("arbitrary"),
        name="moe_combine",
    )(starts.reshape(-1), x1, aff, slot, ye)


TM_PROJ, TN_PROJ = 1024, 1024
TM_ROW = 256
TM_OUT = 512
ATTN_TILE = 512
SEL_BLK = 256
FFN_CHUNK = 256
COMBINE_ROWS = 64


def _layer(x, l, p, bias, far_bias):
    s = x.shape[0]
    lam_init = 0.8 - 0.6 * math.exp(-0.3 * l)
    proj, xcat = norm_matmul(x, p["norm_mix"], p["w_in"], l, tm=min(TM_PROJ, s), tn=TN_PROJ)

    ycat = s5_mixer(xcat, p["s5_ops"], p["ssm_d"], l)
    yg = glu(ycat, p["w_glu"], l, tm=min(TM_PROJ, s))

    qn, k4, vt4, kmax = attn_prep(proj, p["q_gain"], p["k_gain"], tk=ATTN_TILE)
    lam = (jnp.exp(jnp.sum(p["lambda_q1"].astype(F32) * p["lambda_k1"].astype(F32)))
           - jnp.exp(jnp.sum(p["lambda_q2"].astype(F32) * p["lambda_k2"].astype(F32))) + lam_init)
    ao = diff_attention(qn, k4, vt4, bias, far_bias, kmax, lam, p["subln_gain"], lam_init=lam_init,
                        tq=ATTN_TILE, tk=ATTN_TILE)

    merged = gated_merge(yg, ao, proj, p["w_ssm_branch"], p["w_attn_branch"], l,
                         tm=min(TM_PROJ, s), tn=TN_PROJ)
    x1, h2, afft = out_proj_router(merged, x, p["w_out"].astype(BF16), p["norm_ffn"], p["w_router"],
                                   tm=min(TM_OUT, s))

    cap = CAPACITY_FACTOR * s // N_EXPERTS
    slot_t, cum_t, idx = expert_select(afft, cap=cap, blk=SEL_BLK)
    ye = expert_ffn(idx, h2, p["w_expert_gate"], p["w_expert_up"], p["w_expert_down"], l,
                    cap=cap, pad=COMBINE_ROWS, fc=FFN_CHUNK)
    starts = jnp.concatenate([cum_t[:, ::TM_ROW].T, jnp.full((1, N_EXPERTS), cap, I32)], axis=0)
    return moe_combine(starts, x1, afft.T, slot_t.T, ye, tm=TM_ROW, rows=COMBINE_ROWS)


_LAYER_PARAMS = ("ssm_d", "q_gain", "k_gain", "lambda_q1",
                 "lambda_k1", "lambda_q2", "lambda_k2", "subln_gain", "w_out",
                 "norm_mix", "norm_ffn", "w_router")
_STACKED_PARAMS = ("w_in", "w_expert_gate", "w_expert_up", "w_expert_down", "w_glu", "w_ssm_branch",
                   "w_attn_branch")


def kernel(x, w_in, ssm_a_re, ssm_a_im, ssm_log_dt, ssm_b_re, ssm_b_im, ssm_c_re, ssm_c_im, ssm_d, w_glu, w_ssm_branch, q_gain, k_gain, lambda_q1, lambda_k1, lambda_q2, lambda_k2, subln_gain, w_attn_branch, rel_bias, w_out, norm_mix, norm_ffn, w_router, w_expert_gate, w_expert_up, w_expert_down):
    args = dict(locals())
    b = x.shape[0]
    bias = bias_tiles(rel_bias, tk=ATTN_TILE, tq=ATTN_TILE)
    far_bias = attn_bias_consts(rel_bias)
    s5_ops = jax.vmap(s5_operators)(ssm_a_re, ssm_a_im, ssm_log_dt, ssm_b_re, ssm_b_im, ssm_c_re, ssm_c_im)
    outs = []
    for bi in range(b):
        xb = x[bi].astype(F32)
        for l in range(DEPTH):
            p = {k: args[k][l] for k in _LAYER_PARAMS}
            p.update({k: args[k] for k in _STACKED_PARAMS})
            p["s5_ops"] = s5_ops
            xb = _layer(xb, l, p, bias, far_bias)
        outs.append(xb)
    return jnp.stack(outs, axis=0).astype(x.dtype)
```

```python
import functools
import math

import jax
import jax.numpy as jnp
from jax import lax
from jax.experimental import pallas as pl
from jax.experimental.pallas import tpu as pltpu

F32 = jnp.float32
BF16 = jnp.bfloat16
I32 = jnp.int32

D_MODEL = 2048
DEPTH = 2
SSM_WIDTH = D_MODEL // 2
SSM_GROUP = 16
SSM_GROUPS = SSM_WIDTH // SSM_GROUP
SSM_STATE = 64
ATTN_HEADS = 8
QK_DIM = 64
V_DIM = 2 * QK_DIM
ATTN_WIDTH = ATTN_HEADS * V_DIM
QK_COLS = ATTN_HEADS * 2 * QK_DIM
REL_BUCKETS = 32
REL_MAX_DIST = 128
N_EXPERTS = 16
CAPACITY_FACTOR = 2
EXPERT_FF = D_MODEL
NORM_EPS = 1e-6
IN_COLS = SSM_WIDTH + 2 * QK_COLS + ATTN_WIDTH + 2 * D_MODEL
COL_Q = SSM_WIDTH
COL_K = COL_Q + QK_COLS
COL_V = COL_K + QK_COLS
COL_GS = COL_V + ATTN_WIDTH
COL_GA = COL_GS + D_MODEL

V7X_LANES = 128
F32_SUBLANES = 8
V7X_VMEM_BYTES = 64 * 1024 * 1024
VMEM_LIMIT = V7X_VMEM_BYTES - 8 * 1024 * 1024
LOG2E = 1.4426950408889634

S5_CHUNK = 16
S5_GB = 8
NEG_BIG = -1e30


def _cp(*sem):
    return pltpu.CompilerParams(dimension_semantics=sem, vmem_limit_bytes=VMEM_LIMIT)


def _norm_matmul_kernel(x_ref, g_ref, w_ref, o_ref, xcat_ref, h_ref, u_ref):
    j = pl.program_id(1)

    @pl.when(j == 0)
    def _():
        x = x_ref[...]
        r = lax.rsqrt(jnp.mean(x * x, axis=-1, keepdims=True) + NORM_EPS)
        h_ref[...] = (x * r * g_ref[...]).astype(BF16)

    res = jnp.dot(h_ref[...], w_ref[...].astype(BF16), preferred_element_type=F32)
    o_ref[...] = res.astype(o_ref.dtype)

    @pl.when(j == 0)
    def _():
        cn = u_ref.shape[1] // S5_CHUNK
        for slab in range(u_ref.shape[0]):
            u_ref[slab] = res[:, slab * V7X_LANES:(slab + 1) * V7X_LANES]
            for t in range(S5_CHUNK):
                piece = u_ref[slab, pl.ds(t, cn, stride=S5_CHUNK), :]
                xcat_ref[slab, :, t * V7X_LANES:(t + 1) * V7X_LANES] = piece.astype(xcat_ref.dtype)


def norm_matmul(x, gain, w, layer, *, tm, tn):
    s, d = x.shape
    n = w.shape[2]
    assert tn == SSM_WIDTH and tm % (S5_CHUNK * F32_SUBLANES) == 0
    nslab = SSM_WIDTH // V7X_LANES
    return pl.pallas_call(
        _norm_matmul_kernel,
        out_shape=(jax.ShapeDtypeStruct((s, n), BF16),
                   jax.ShapeDtypeStruct((nslab, s // S5_CHUNK, S5_CHUNK * V7X_LANES), BF16)),
        grid=(s // tm, n // tn),
        in_specs=[pl.BlockSpec((tm, d), lambda i, j: (i, 0)),
                  pl.BlockSpec((1, d), lambda i, j: (0, 0)),
                  pl.BlockSpec((None, d, tn), lambda i, j: (layer, 0, j))],
        out_specs=(pl.BlockSpec((tm, tn), lambda i, j: (i, j)),
                   pl.BlockSpec((nslab, tm // S5_CHUNK, S5_CHUNK * V7X_LANES), lambda i, j: (0, i, 0))),
        scratch_shapes=[pltpu.VMEM((tm, d), BF16), pltpu.VMEM((nslab, tm, V7X_LANES), F32)],
        compiler_params=_cp("parallel", "arbitrary"),
        name="norm_in_proj",
    )(x, gain.reshape(1, d).astype(F32), w)


def _attn_prep_kernel(q_ref, k_ref, v_ref, gq_ref, gk_ref, qo_ref, ko_ref, vo_ref):
    lane = lax.broadcasted_iota(I32, (1, V7X_LANES), 1)
    lo_mask = lane < QK_DIM

    def norm(src_ref, g_ref, a):
        x = src_ref[:, a * V7X_LANES:(a + 1) * V7X_LANES].astype(F32)
        ss = x * x
        lo = jnp.sum(jnp.where(lo_mask, ss, 0.0), axis=-1, keepdims=True)
        hi = jnp.sum(jnp.where(lo_mask, 0.0, ss), axis=-1, keepdims=True)
        ms = jnp.where(lo_mask, lo, hi) * (1.0 / QK_DIM)
        return x * lax.rsqrt(ms + NORM_EPS) * g_ref[...]

    for a in range(ATTN_HEADS):
        qo_ref[:, a * V7X_LANES:(a + 1) * V7X_LANES] = norm(q_ref, gq_ref, a).astype(qo_ref.dtype)
        ko_ref[a, 0] = norm(k_ref, gk_ref, a).astype(ko_ref.dtype)
        v = v_ref[:, a * V7X_LANES:(a + 1) * V7X_LANES].astype(F32)
        vo_ref[a, 0] = v.T.astype(vo_ref.dtype)


def attn_prep(proj, q_gain, k_gain, *, tk):
    s = proj.shape[0]
    nk = s // tk
    gq = (jnp.tile(q_gain.astype(F32), 2) * (QK_DIM ** -0.5 * LOG2E)).reshape(1, V7X_LANES)
    gk = jnp.tile(k_gain.astype(F32), 2).reshape(1, V7X_LANES)
    cq, ck, cv = COL_Q // QK_COLS, COL_K // QK_COLS, COL_V // ATTN_WIDTH
    qn, k4, vt4 = pl.pallas_call(
        _attn_prep_kernel,
        out_shape=(jax.ShapeDtypeStruct((s, QK_COLS), BF16),
                   jax.ShapeDtypeStruct((ATTN_HEADS, nk, tk, V_DIM), BF16),
                   jax.ShapeDtypeStruct((ATTN_HEADS, nk, V_DIM, tk), BF16)),
        grid=(nk,),
        in_specs=[pl.BlockSpec((tk, QK_COLS), lambda i: (i, cq)),
                  pl.BlockSpec((tk, QK_COLS), lambda i: (i, ck)),
                  pl.BlockSpec((tk, ATTN_WIDTH), lambda i: (i, cv)),
                  pl.BlockSpec((1, V7X_LANES), lambda i: (0, 0)),
                  pl.BlockSpec((1, V7X_LANES), lambda i: (0, 0))],
        out_specs=(pl.BlockSpec((tk, QK_COLS), lambda i: (i, 0)),
                   pl.BlockSpec((ATTN_HEADS, 1, tk, V_DIM), lambda i: (0, i, 0, 0)),
                   pl.BlockSpec((ATTN_HEADS, 1, V_DIM, tk), lambda i: (0, i, 0, 0))),
        compiler_params=_cp("parallel"),
        name="attn_prep",
    )(proj, proj, proj, gq, gk)
    kmax = math.sqrt(QK_DIM) * (1.0 + 2.0 ** -7) * jnp.max(jnp.abs(k_gain.astype(F32)))
    return qn, k4, vt4, jnp.full((ATTN_HEADS, 2), kmax, F32)


ATTN_NEAR = 3
ATTN_BIAS_TILES = 5
ATTN_UNROLL = 6


def _bias_tile_kernel(rb_ref, o_ref, *, tk, tq):
    h = pl.program_id(0)
    d = pl.program_id(1)
    w = tk + tq
    m = lax.broadcasted_iota(I32, (F32_SUBLANES, w), 1)
    rel = (d - ATTN_BIAS_TILES // 2) * tk + (tk - 1) - m
    half = REL_BUCKETS // 2
    exact = half // 2
    side = jnp.where(rel > 0, half, 0).astype(I32)
    n = jnp.abs(rel)
    nf = jnp.maximum(n, 1).astype(F32)
    large = exact + (jnp.log(nf / exact) / math.log(REL_MAX_DIST / exact) * (half - exact)).astype(I32)
    large = jnp.minimum(large, half - 1)
    bucket = side + jnp.where(n < exact, n, large).astype(I32)
    val = jnp.zeros((F32_SUBLANES, w), F32)
    for b in range(REL_BUCKETS):
        val = jnp.where(bucket == b, rb_ref[b, h], val)
    table = jnp.broadcast_to(val[0:1, :] * LOG2E, (tk, w))
    o_ref[0, 0] = pltpu.roll(table, w - tk + 1, 1, stride=1, stride_axis=0)[:, :tq]


def bias_tiles(rel_bias, *, tk, tq):
    assert tk == tq and (tk + tq) & (tk + tq - 1) == 0
    return pl.pallas_call(
        functools.partial(_bias_tile_kernel, tk=tk, tq=tq),
        out_shape=jax.ShapeDtypeStruct((ATTN_HEADS, ATTN_BIAS_TILES, tk, tq), F32),
        grid=(ATTN_HEADS, ATTN_BIAS_TILES),
        in_specs=[pl.BlockSpec(memory_space=pltpu.SMEM)],
        out_specs=pl.BlockSpec((1, 1, tk, tq), lambda h, d: (h, d, 0, 0)),
        compiler_params=_cp("parallel", "parallel"),
        name="t5_bias_tiles",
    )(rel_bias.astype(F32))


FAST_MIN_SUM = 2.0 ** -90


def _attn_kernel(lam_ref, far_ref, kmax_ref, q_ref, k_ref, vt_ref, bias_ref, g_ref, o_ref,
                 s_ref, cm_ref, m_ref, l_ref, acc_ref, p_ref, *, nk, tk, tq, out_scale):
    h = pl.program_id(0)
    i = pl.program_id(1)
    lane = lax.broadcasted_iota(I32, (1, V7X_LANES), 1)
    q = q_ref[...]
    zero = jnp.zeros_like(q)
    qmaps = (jnp.where(lane < QK_DIM, q, zero), jnp.where(lane < QK_DIM, zero, q))

    n0 = jnp.clip(i - 1, 0, nk - ATTN_NEAR)
    nfar = nk - ATTN_NEAR

    def far_tile(t):
        return jnp.where(t < n0, t, t + ATTN_NEAR)

    def far_const(j):
        return jnp.where(j < i, far_ref[h, 0], far_ref[h, 1])

    def near_bias(j):
        return bias_ref[0, j - i + ATTN_BIAS_TILES // 2]

    qf = q.astype(F32)
    row8 = lax.broadcasted_iota(I32, (F32_SUBLANES, V7X_LANES), 0)
    lane8 = lax.broadcasted_iota(I32, (F32_SUBLANES, V7X_LANES), 1)
    pick = jnp.where(jnp.logical_or(jnp.logical_and(row8 == 0, lane8 < QK_DIM),
                                    jnp.logical_and(row8 == 1, lane8 >= QK_DIM)), 1.0, 0.0)
    qn2 = lax.dot_general(pick, qf * qf, (((1,), (1,)), ((), ())), preferred_element_type=F32,
                          precision=lax.Precision.HIGHEST)
    shift = [jnp.sqrt(qn2[c:c + 1, :]) * kmax_ref[h, c] + far_ref[h, 2] for c in range(2)]
    l_ref[...] = jnp.zeros(l_ref.shape, F32)
    acc_ref[...] = jnp.zeros(acc_ref.shape, F32)

    def fast_probs(j, slot, bias_tile, bias_const):
        kt = k_ref[0, j]
        for c in range(2):
            s = lax.dot_general(kt, qmaps[c], (((1,), (1,)), ((), ())), preferred_element_type=F32)
            if bias_tile is not None:
                s = s + bias_tile
            p = jnp.exp2(s - (shift[c] - bias_const))
            l_ref[c:c + 1, :] += jnp.sum(p, axis=0, keepdims=True)
            p_ref[slot, c] = p.astype(BF16)

    def fast_pv(j, slot):
        vt = vt_ref[0, j]
        for c in range(2):
            acc_ref[c] += jnp.dot(vt, p_ref[slot, c], preferred_element_type=F32)

    fast_probs(n0, 0, near_bias(n0), 0.0)
    for w in range(1, ATTN_NEAR):
        fast_probs(n0 + w, w % 2, near_bias(n0 + w), 0.0)
        fast_pv(n0 + w - 1, (w - 1) % 2)
    j0 = far_tile(0)
    fast_probs(j0, ATTN_NEAR % 2, None, far_const(j0))
    fast_pv(n0 + ATTN_NEAR - 1, (ATTN_NEAR - 1) % 2)

    def fast_group(u, jprev):
        for w in range(ATTN_UNROLL):
            jn = far_tile(ATTN_UNROLL * u + w + 1)
            fast_probs(jn, (ATTN_NEAR + 1 + w) % 2, None, far_const(jn))
            fast_pv(jprev, (ATTN_NEAR + w) % 2)
            jprev = jn
        return jprev

    jl = lax.fori_loop(0, (nfar - 1) // ATTN_UNROLL, fast_group, j0)
    fast_pv(jl, (ATTN_NEAR + nfar - 1) % 2)

    lmin = jnp.min(jnp.minimum(l_ref[0:1, :], l_ref[1:2, :]))

    @pl.when(jnp.logical_not(lmin >= FAST_MIN_SUM))
    def _():
        _attn_running_max(h, i, n0, nfar, far_tile, far_const, near_bias, qmaps, k_ref, vt_ref,
                          s_ref, cm_ref, m_ref, l_ref, acc_ref, tq=tq)

    o1 = acc_ref[0] / l_ref[0:1, :]
    o2 = acc_ref[1] / l_ref[1:2, :]
    o = o1 - lam_ref[0] * o2
    r = lax.rsqrt(jnp.mean(o * o, axis=0, keepdims=True) + NORM_EPS)
    o = o * r * g_ref[...] * out_scale
    o_ref[...] = o.T.astype(o_ref.dtype)


def _attn_running_max(h, i, n0, nfar, far_tile, far_const, near_bias, qmaps, k_ref, vt_ref,
                      s_ref, cm_ref, m_ref, l_ref, acc_ref, *, tq):
    m_ref[...] = jnp.full(m_ref.shape, NEG_BIG, F32)
    l_ref[...] = jnp.zeros(l_ref.shape, F32)
    acc_ref[...] = jnp.zeros(acc_ref.shape, F32)

    def scores(j, slot, bias_tile, bias_const):
        kt = k_ref[0, j]
        for c in range(2):
            s = lax.dot_general(kt, qmaps[c], (((1,), (1,)), ((), ())),
                                preferred_element_type=F32)
            if bias_tile is not None:
                s = s + bias_tile
            s_ref[slot, c] = s
            cm_ref[slot, c:c + 1, :] = jnp.max(s, axis=0, keepdims=True) + bias_const
        cm_ref[slot, 2:3, :] = jnp.zeros((1, tq), F32) + bias_const

    def absorb(j, slot):
        vt = vt_ref[0, j]
        cb = cm_ref[slot, 2:3, :]
        for c in range(2):
            m_old = m_ref[c:c + 1, :]
            m_new = jnp.maximum(m_old, cm_ref[slot, c:c + 1, :])
            alpha = jnp.exp2(m_old - m_new)
            p = jnp.exp2(s_ref[slot, c] - (m_new - cb))
            l_ref[c:c + 1, :] = alpha * l_ref[c:c + 1, :] + jnp.sum(p, axis=0, keepdims=True)
            acc_ref[c] = alpha * acc_ref[c] + jnp.dot(vt, p.astype(BF16), preferred_element_type=F32)
            m_ref[c:c + 1, :] = m_new

    scores(n0, 0, near_bias(n0), 0.0)
    scores(n0 + 1, 1, near_bias(n0 + 1), 0.0)
    absorb(n0, 0)
    scores(n0 + 2, 0, near_bias(n0 + 2), 0.0)
    absorb(n0 + 1, 1)
    j0 = far_tile(0)
    scores(j0, 1, None, far_const(j0))
    absorb(n0 + 2, 0)

    def group(u, jprev):
        for w in range(ATTN_UNROLL):
            jn = far_tile(ATTN_UNROLL * u + w + 1)
            scores(jn, w % 2, None, far_const(jn))
            absorb(jprev, (w + 1) % 2)
            jprev = jn
        return jprev

    jlast = lax.fori_loop(0, (nfar - 1) // ATTN_UNROLL, group, j0)
    absorb(jlast, 1)


def attn_bias_consts(rel_bias):
    half = REL_BUCKETS // 2
    rb = rel_bias.astype(F32) * LOG2E
    return jnp.stack([rb[half - 1], rb[REL_BUCKETS - 1], jnp.max(rb, axis=0)], axis=1)


def diff_attention(qn, k4, vt4, bias, far_bias, kmax, lam, subln_gain, *, lam_init, tq, tk):
    s = qn.shape[0]
    nk = s // tk
    assert tq == tk and nk > ATTN_NEAR and (nk - ATTN_NEAR - 1) % ATTN_UNROLL == 0
    kern = functools.partial(_attn_kernel, nk=nk, tk=tk, tq=tq, out_scale=1.0 - lam_init)
    return pl.pallas_call(
        kern,
        out_shape=jax.ShapeDtypeStruct((s, ATTN_WIDTH), BF16),
        grid=(ATTN_HEADS, s // tq),
        in_specs=[pl.BlockSpec(memory_space=pltpu.SMEM),
                  pl.BlockSpec(memory_space=pltpu.SMEM),
                  pl.BlockSpec(memory_space=pltpu.SMEM),
                  pl.BlockSpec((tq, V_DIM), lambda h, i: (i, h)),
                  pl.BlockSpec((1, nk, tk, V_DIM), lambda h, i: (h, 0, 0, 0)),
                  pl.BlockSpec((1, nk, V_DIM, tk), lambda h, i: (h, 0, 0, 0)),
                  pl.BlockSpec((1, ATTN_BIAS_TILES, tk, tq), lambda h, i: (h, 0, 0, 0)),
                  pl.BlockSpec((V_DIM, 1), lambda h, i: (0, 0))],
        out_specs=pl.BlockSpec((tq, V_DIM), lambda h, i: (i, h)),
        scratch_shapes=[pltpu.VMEM((2, 2, tk, tq), F32),
                        pltpu.VMEM((2, F32_SUBLANES, tq), F32),
                        pltpu.VMEM((F32_SUBLANES, tq), F32),
                        pltpu.VMEM((F32_SUBLANES, tq), F32),
                        pltpu.VMEM((2, V_DIM, tq), F32),
                        pltpu.VMEM((2, 2, tk, tq), BF16)],
        compiler_params=_cp("parallel", "arbitrary"),
        name="diff_attention",
    )(lam.reshape(1).astype(F32), far_bias, kmax, qn, k4, vt4, bias,
      subln_gain.reshape(V_DIM, 1).astype(F32))


def s5_operators(a_re, a_im, log_dt, b_re, b_im, c_re, c_im):
    t_len, hp = S5_CHUNK, lax.Precision.HIGHEST
    a_re, a_im = a_re.astype(F32), a_im.astype(F32)
    dt = jnp.exp(log_dt.astype(F32))[..., None]
    steps = jnp.arange(t_len + 1, dtype=F32)[:, None, None, None]
    mag = jnp.exp(a_re * dt * steps)
    ang = a_im * dt * steps
    pw_re, pw_im = mag * jnp.cos(ang), mag * jnp.sin(ang)
    den = a_re * a_re + a_im * a_im
    nr, ni = pw_re[1] - 1.0, pw_im[1]
    coef_re = ((nr * a_re + ni * a_im) / den)[..., None]
    coef_im = ((ni * a_re - nr * a_im) / den)[..., None]
    b_re, b_im = b_re.astype(F32), b_im.astype(F32)
    bb_re = coef_re * b_re - coef_im * b_im
    bb_im = coef_re * b_im + coef_im * b_re
    c_re, c_im = c_re.astype(F32), c_im.astype(F32)

    g, n_st, p_ch = a_re.shape[1], SSM_STATE, SSM_GROUP
    tp = t_len * p_ch
    pwt_re = jnp.transpose(pw_re, (1, 2, 3, 0))
    pwt_im = jnp.transpose(pw_im, (1, 2, 3, 0))
    ct_re = jnp.transpose(c_re, (0, 1, 3, 2))
    ct_im = jnp.transpose(c_im, (0, 1, 3, 2))
    bbt_re = jnp.transpose(bb_re, (0, 1, 3, 2))
    bbt_im = jnp.transpose(bb_im, (0, 1, 3, 2))
    cp_re = ct_re[:, :, :, None, :] * pwt_re[..., None] - ct_im[:, :, :, None, :] * pwt_im[..., None]
    cp_im = ct_re[:, :, :, None, :] * pwt_im[..., None] + ct_im[:, :, :, None, :] * pwt_re[..., None]

    zlag = jnp.zeros((g, n_st, t_len - 1, p_ch), F32)

    def lagged(cp):
        return (jnp.concatenate([zlag, cp[0, :, :, :t_len]], axis=2),
                jnp.concatenate([jnp.flip(cp[1, :, :, :t_len], axis=2), zlag], axis=2))

    rf_re, rb_re = lagged(cp_re)
    rf_im, rb_im = lagged(cp_im)
    r_cat = jnp.concatenate([rf_re, rf_im, rb_re, rb_im], axis=1).reshape(g, 4 * n_st, (2 * t_len - 1) * p_ch)
    a_cat = jnp.concatenate([bbt_re[0], -bbt_im[0], bbt_re[1], -bbt_im[1]], axis=-1)
    kp = jnp.einsum('gpk,gkx->gpx', a_cat, r_cat, precision=hp)
    toep = jnp.stack([kp[:, :, (t_len - 1 - j) * p_ch:(t_len - 1 - j) * p_ch + tp] for j in range(t_len)],
                     axis=1).reshape(g, tp, tp).astype(BF16)

    def seg_powers(d, reverse):
        pr = jnp.transpose(pw_re[:t_len, d], (1, 0, 2))
        pi = jnp.transpose(pw_im[:t_len, d], (1, 0, 2))
        if reverse:
            pr, pi = jnp.flip(pr, axis=1), jnp.flip(pi, axis=1)
        return pr, pi

    prf, pif = seg_powers(0, True)
    prb, pib = seg_powers(1, False)
    pa = jnp.concatenate([prf, pif, pif, prf, prb, pib, pib, prb], axis=-1)
    pb = jnp.concatenate([-pif, prf, prf, -pif, -pib, prb, prb, -pib], axis=-1)
    br = jnp.concatenate([bbt_re[0]] * 4 + [bbt_re[1]] * 4, axis=-1)
    bi = jnp.concatenate([bbt_im[0]] * 4 + [bbt_im[1]] * 4, axis=-1)
    smap = (br[:, None] * pa[:, :, None, :] + bi[:, None] * pb[:, :, None, :]).reshape(g, tp, 8 * n_st)
    smap = smap.astype(BF16)

    def out_map(d, reverse):
        wr, wi = cp_re[d][:, :, 1:t_len + 1], cp_im[d][:, :, 1:t_len + 1]
        if reverse:
            wr, wi = jnp.flip(wr, axis=2), jnp.flip(wi, axis=2)
        return [wr.reshape(g, n_st, tp), -wi.reshape(g, n_st, tp)]

    mc = jnp.concatenate(out_map(0, False) + out_map(1, True), axis=1).astype(BF16)

    def carry(d):
        ar, ai = pw_re[t_len, d], pw_im[t_len, d]
        return [jnp.concatenate([ar, ar], -1), jnp.concatenate([-ai, ai], -1),
                jnp.concatenate([ai, -ai], -1)]

    coef = jnp.stack(carry(0) + carry(1), axis=0)
    return toep, smap, mc, coef


def _gelu_tanh(x):
    return 0.5 * x * (1.0 + jnp.tanh(math.sqrt(2.0 / math.pi) * (x + 0.044715 * (x * x * x))))


def _s5_kernel(x_ref, sel_ref, toep_ref, smap_ref, mc_ref, coef_ref, d_ref, y_ref,
               u_ref, ef_ref, efs_ref, eb_ref, ebs_ref, *, cn):
    gb = toep_ref.shape[0]
    tp = toep_ref.shape[1]
    n2 = 2 * SSM_STATE
    xcat = x_ref[0]

    def sel(g):
        off = (gb - 1 - g) * SSM_GROUP
        return sel_ref[off:off + xcat.shape[1], :]

    for g in range(gb):
        u_ref[g] = jnp.dot(xcat, sel(g), preferred_element_type=F32).astype(BF16)
        e = jnp.dot(u_ref[g], smap_ref[g], preferred_element_type=F32)
        for r, ref in enumerate((ef_ref, efs_ref, eb_ref, ebs_ref)):
            ref[pl.ds(g, cn, stride=gb), :] = e[:, r * n2:(r + 1) * n2]

    cf, cfs, cfw = coef_ref[0], coef_ref[1], coef_ref[2]
    cb, cbs, cbw = coef_ref[3], coef_ref[4], coef_ref[5]

    def step(c, carry):
        s, sw, r, rw = carry
        fo = pl.multiple_of(c * gb, gb)
        bo = pl.multiple_of((cn - 1 - c) * gb, gb)
        e, es = ef_ref[pl.ds(fo, gb), :], efs_ref[pl.ds(fo, gb), :]
        ef_ref[pl.ds(fo, gb), :] = s
        s, sw = cf * s + cfs * sw + e, cf * sw + cfw * s + es
        e, es = eb_ref[pl.ds(bo, gb), :], ebs_ref[pl.ds(bo, gb), :]
        eb_ref[pl.ds(bo, gb), :] = r
        r, rw = cb * r + cbs * rw + e, cb * rw + cbw * r + es
        return s, sw, r, rw

    z = jnp.zeros((gb, n2), F32)
    lax.fori_loop(0, cn, step, (z, z, z, z))

    ycat = None
    for g in range(gb):
        u = u_ref[g]
        st = jnp.concatenate([ef_ref[pl.ds(g, cn, stride=gb), :], eb_ref[pl.ds(g, cn, stride=gb), :]],
                             axis=1).astype(BF16)
        y = (jnp.dot(u, toep_ref[g], preferred_element_type=F32)
             + jnp.dot(st, mc_ref[g], preferred_element_type=F32)
             + u.astype(F32) * d_ref[g])
        placed = lax.dot_general(_gelu_tanh(y).astype(BF16), sel(g), (((1,), (1,)), ((), ())),
                                 preferred_element_type=F32)
        ycat = placed if ycat is None else ycat + placed
    y_ref[0] = ycat.astype(y_ref.dtype)


def s5_lane_selector():
    gb, p, t_len = S5_GB, SSM_GROUP, S5_CHUNK
    r = jnp.arange(t_len * V7X_LANES + (gb - 1) * p)[:, None] - (gb - 1) * p
    c = jnp.arange(t_len * p)[None, :]
    hit = (r >= 0) & (r // V7X_LANES == c // p) & ((r % V7X_LANES) // p == 0) & (r % p == c % p)
    return hit.astype(BF16)


def s5_mixer(xcat, ops, ssm_d, layer):
    toep, smap, mc, coef = ops
    g, p, t_len = SSM_GROUPS, SSM_GROUP, S5_CHUNK
    nslab, cn, _ = xcat.shape
    tp = t_len * p
    gb = S5_GB
    assert gb * p == V7X_LANES and nslab == g // gb
    sel = s5_lane_selector()
    dsk = jnp.tile(ssm_d.astype(F32).reshape(g, 1, p), (1, 1, t_len))
    ycat = pl.pallas_call(
        functools.partial(_s5_kernel, cn=cn),
        out_shape=jax.ShapeDtypeStruct((nslab, cn, t_len * V7X_LANES), BF16),
        grid=(nslab,),
        in_specs=[pl.BlockSpec((1, cn, t_len * V7X_LANES), lambda i: (i, 0, 0)),
                  pl.BlockSpec(sel.shape, lambda i: (0, 0)),
                  pl.BlockSpec((None, gb, tp, tp), lambda i: (layer, i, 0, 0)),
                  pl.BlockSpec((None, gb, tp, smap.shape[3]), lambda i: (layer, i, 0, 0)),
                  pl.BlockSpec((None, gb, mc.shape[2], tp), lambda i: (layer, i, 0, 0)),
                  pl.BlockSpec((None, 6, gb, 2 * SSM_STATE), lambda i: (layer, 0, i, 0)),
                  pl.BlockSpec((gb, 1, tp), lambda i: (i, 0, 0))],
        out_specs=pl.BlockSpec((1, cn, t_len * V7X_LANES), lambda i: (i, 0, 0)),
        scratch_shapes=[pltpu.VMEM((gb, cn, tp), BF16)]
        + [pltpu.VMEM((cn * gb, 2 * SSM_STATE), F32) for _ in range(4)],
        compiler_params=_cp("parallel"),
        name="s5_chunked_scan",
    )(xcat, sel, toep, smap, mc, coef, dsk)
    return ycat


def _sigmoid(x):
    return 1.0 / (1.0 + jnp.exp(-x))


def _glu_kernel(ycat_ref, w_ref, o_ref, y_ref):
    cn = ycat_ref.shape[1]
    for slab in range(ycat_ref.shape[0]):
        for t in range(S5_CHUNK):
            piece = ycat_ref[slab, :, t * V7X_LANES:(t + 1) * V7X_LANES].astype(F32)
            y_ref[slab, pl.ds(t, cn, stride=S5_CHUNK), :] = piece
    y = jnp.concatenate([y_ref[slab] for slab in range(ycat_ref.shape[0])], axis=1)
    z = jnp.dot(y.astype(BF16), w_ref[0].astype(BF16), preferred_element_type=F32)
    o_ref[...] = (y * _sigmoid(z)).astype(o_ref.dtype)


def glu(ycat, w, layer, *, tm):
    nslab, cn, _ = ycat.shape
    s, d = cn * S5_CHUNK, nslab * V7X_LANES
    return pl.pallas_call(
        _glu_kernel,
        out_shape=jax.ShapeDtypeStruct((s, d), BF16),
        grid=(s // tm,),
        in_specs=[pl.BlockSpec((nslab, tm // S5_CHUNK, S5_CHUNK * V7X_LANES), lambda i: (0, i, 0)),
                  pl.BlockSpec((1, d, d), lambda i: (layer, 0, 0))],
        out_specs=pl.BlockSpec((tm, d), lambda i: (i, 0)),
        scratch_shapes=[pltpu.VMEM((nslab, tm, V7X_LANES), F32)],
        compiler_params=_cp("parallel"),
        name="half_glu",
    )(ycat, w)


def _merge_kernel(yg_ref, ao_ref, gs_ref, ga_ref, ws_ref, wa_ref, o_ref):
    a = jnp.dot(yg_ref[...], ws_ref[0].astype(BF16), preferred_element_type=F32)
    b = jnp.dot(ao_ref[...], wa_ref[0].astype(BF16), preferred_element_type=F32)
    o = _sigmoid(gs_ref[...].astype(F32)) * a + _sigmoid(ga_ref[...].astype(F32)) * b
    o_ref[...] = o.astype(o_ref.dtype)


def gated_merge(yg, ao, proj, ws, wa, layer, *, tm, tn):
    s, k = yg.shape
    n = ws.shape[2]
    cs, ca = COL_GS // tn, COL_GA // tn
    return pl.pallas_call(
        _merge_kernel,
        out_shape=jax.ShapeDtypeStruct((s, n), BF16),
        grid=(s // tm, n // tn),
        in_specs=[pl.BlockSpec((tm, k), lambda i, j: (i, 0)),
                  pl.BlockSpec((tm, k), lambda i, j: (i, 0)),
                  pl.BlockSpec((tm, tn), lambda i, j: (i, cs + j)),
                  pl.BlockSpec((tm, tn), lambda i, j: (i, ca + j)),
                  pl.BlockSpec((1, k, tn), lambda i, j: (layer, 0, j)),
                  pl.BlockSpec((1, k, tn), lambda i, j: (layer, 0, j))],
        out_specs=pl.BlockSpec((tm, tn), lambda i, j: (i, j)),
        compiler_params=_cp("parallel", "arbitrary"),
        name="gated_merge",
    )(yg, ao, proj, proj, ws, wa)


def _out_router_kernel(m_ref, x_ref, w_ref, g_ref, wrt_ref, xo_ref, h_ref, afft_ref):
    x1 = x_ref[...] + jnp.dot(m_ref[...], w_ref[...], preferred_element_type=F32)
    xo_ref[...] = x1
    r = lax.rsqrt(jnp.mean(x1 * x1, axis=-1, keepdims=True) + NORM_EPS)
    h = x1 * r * g_ref[...]
    h_ref[...] = h
    lgt = lax.dot_general(wrt_ref[...], h, (((1,), (1,)), ((), ())),
                          preferred_element_type=F32, precision=lax.Precision.HIGHEST)
    et = jnp.exp(lgt - jnp.max(lgt, axis=0, keepdims=True))
    afft_ref[...] = et / jnp.sum(et, axis=0, keepdims=True)


def out_proj_router(merged, x, w_out, gain, w_router, *, tm):
    s, d = x.shape
    e = w_router.shape[1]
    return pl.pallas_call(
        _out_router_kernel,
        out_shape=(jax.ShapeDtypeStruct((s, d), F32), jax.ShapeDtypeStruct((s, d), F32),
                   jax.ShapeDtypeStruct((e, s), F32)),
        grid=(s // tm,),
        in_specs=[pl.BlockSpec((tm, d), lambda i: (i, 0)),
                  pl.BlockSpec((tm, d), lambda i: (i, 0)),
                  pl.BlockSpec((d, d), lambda i: (0, 0)),
                  pl.BlockSpec((1, d), lambda i: (0, 0)),
                  pl.BlockSpec((e, d), lambda i: (0, 0))],
        out_specs=(pl.BlockSpec((tm, d), lambda i: (i, 0)),
                   pl.BlockSpec((tm, d), lambda i: (i, 0)),
                   pl.BlockSpec((e, tm), lambda i: (0, i))),
        compiler_params=_cp("parallel"),
        name="out_proj_router",
    )(merged, x, w_out, gain.reshape(1, d).astype(F32), w_router.astype(F32).T)


SLOT_LO_BITS = 6
SLOT_LO = 1 << SLOT_LO_BITS


def _select_kernel(afft_ref, slot_ref, cum_ref, idx_ref, acc_ref, *, s, cap, blk):
    ne = afft_ref.shape[0]

    def bit_body(b, thr):
        cand = thr | jnp.left_shift(jnp.ones((ne, 1), I32), 30 - b)
        keys = pltpu.bitcast(afft_ref[...], I32)
        cnt = jnp.sum((keys >= cand).astype(I32), axis=1, keepdims=True)
        return jnp.where(cnt >= cap, cand, thr)

    thr = lax.fori_loop(0, 31, bit_body, jnp.zeros((ne, 1), I32))
    keys = pltpu.bitcast(afft_ref[...], I32)
    need = cap - jnp.sum((keys > thr).astype(I32), axis=1, keepdims=True)

    ri = lax.broadcasted_iota(I32, (blk, blk), 0)
    ci = lax.broadcasted_iota(I32, (blk, blk), 1)
    upper = jnp.where(ri < ci, 1.0, 0.0).astype(BF16)
    na = cap // SLOT_LO
    acol = lax.broadcasted_iota(I32, (na, 1), 0)
    bcol = lax.broadcasted_iota(I32, (SLOT_LO, 1), 0)
    tlane = lax.broadcasted_iota(I32, (1, blk), 1)
    acc_ref[...] = jnp.zeros(acc_ref.shape, F32)

    def blk_body(b, carry):
        ceq, csel = carry
        off = pl.multiple_of(b * blk, blk)
        kb = pltpu.bitcast(afft_ref[:, pl.ds(off, blk)], I32)
        gt = kb > thr
        eq = kb == thr
        eqf = jnp.where(eq, 1.0, 0.0)
        rank_eq = jnp.dot(eqf.astype(BF16), upper, preferred_element_type=F32) + ceq
        sel = jnp.logical_or(gt, jnp.logical_and(eq, rank_eq < need.astype(F32)))
        self_ = jnp.where(sel, 1.0, 0.0)
        cum = jnp.dot(self_.astype(BF16), upper, preferred_element_type=F32) + csel
        cum_i = cum.astype(I32)
        cum_ref[:, pl.ds(off, blk)] = cum_i
        slot = jnp.where(sel, cum_i, -1)
        slot_ref[:, pl.ds(off, blk)] = slot
        tok = off + tlane
        hi = (tok // V7X_LANES).astype(F32)
        lo = (tok % V7X_LANES).astype(F32)
        for e in range(ne):
            srow = slot[e:e + 1, :]
            in_a = lax.shift_right_arithmetic(srow, SLOT_LO_BITS) == acol
            lhs = jnp.concatenate([jnp.where(in_a, hi, 0.0), jnp.where(in_a, lo, 0.0)],
                                  axis=0).astype(BF16)
            rhs = jnp.where((srow & (SLOT_LO - 1)) == bcol, 1.0, 0.0).astype(BF16)
            acc_ref[e] += lax.dot_general(lhs, rhs, (((1,), (1,)), ((), ())), preferred_element_type=F32)
        return (ceq + jnp.sum(eqf, axis=1, keepdims=True), csel + jnp.sum(self_, axis=1, keepdims=True))

    z = jnp.zeros((ne, 1), F32)
    lax.fori_loop(0, s // blk, blk_body, (z, z))
    a = acc_ref[...]
    idx_ref[...] = (a[:, :na, :] * float(V7X_LANES) + a[:, na:, :]).astype(I32)


def expert_select(afft, *, cap, blk):
    ne, s = afft.shape
    assert cap % SLOT_LO == 0
    return pl.pallas_call(
        functools.partial(_select_kernel, s=s, cap=cap, blk=blk),
        out_shape=(jax.ShapeDtypeStruct((ne, s), I32), jax.ShapeDtypeStruct((ne, s), I32),
                   jax.ShapeDtypeStruct((ne, cap // SLOT_LO, SLOT_LO), I32)),
        scratch_shapes=[pltpu.VMEM((ne, 2 * (cap // SLOT_LO), SLOT_LO), F32)],
        compiler_params=pltpu.CompilerParams(vmem_limit_bytes=VMEM_LIMIT),
        name="expert_select",
    )(afft)


GATHER_UNROLL = 8


def _ffn_kernel(idx_ref, h_hbm, wg_ref, wu_ref, wd_ref, y_ref, xg32_ref, xg_ref, sem, *, cap):
    e = pl.program_id(0)
    f = pl.program_id(1)

    def row_copy(r):
        tok = idx_ref[e * cap + r]
        return pltpu.make_async_copy(h_hbm.at[pl.ds(tok, 1)], xg32_ref.at[pl.ds(r, 1)], sem)

    @pl.when(f == 0)
    def _():
        def start(rb, c):
            for w in range(GATHER_UNROLL):
                row_copy(rb * GATHER_UNROLL + w).start(priority=w % 2)
            return c

        lax.fori_loop(0, cap // GATHER_UNROLL, start, 0)
        y_ref[...] = jnp.zeros(y_ref.shape, F32)
        pltpu.make_async_copy(h_hbm.at[pl.ds(0, cap)], xg32_ref, sem).wait()
        xg_ref[...] = xg32_ref[...].astype(BF16)

    xg = xg_ref[...]
    a = jnp.dot(xg, wg_ref[0, 0].astype(BF16), preferred_element_type=F32)
    b = jnp.dot(xg, wu_ref[0, 0].astype(BF16), preferred_element_type=F32)
    hid = (a * _sigmoid(a) * b).astype(BF16)
    y_ref[0, 0:cap, :] += jnp.dot(hid, wd_ref[0, 0].astype(BF16), preferred_element_type=F32)


def expert_ffn(idx, h2, wg, wu, wd, layer, *, cap, pad, fc):
    _, ne, d, ff = wg.shape
    grid_spec = pltpu.PrefetchScalarGridSpec(
        num_scalar_prefetch=1,
        grid=(ne, ff // fc),
        in_specs=[pl.BlockSpec(memory_space=pl.ANY),
                  pl.BlockSpec((1, 1, d, fc), lambda e, f, idx: (layer, e, 0, f)),
                  pl.BlockSpec((1, 1, d, fc), lambda e, f, idx: (layer, e, 0, f)),
                  pl.BlockSpec((1, 1, fc, d), lambda e, f, idx: (layer, e, f, 0))],
        out_specs=pl.BlockSpec((1, cap + pad, d), lambda e, f, idx: (e, 0, 0)),
        scratch_shapes=[pltpu.VMEM((cap, d), F32), pltpu.VMEM((cap, d), BF16),
                        pltpu.SemaphoreType.DMA(())],
    )
    return pl.pallas_call(
        functools.partial(_ffn_kernel, cap=cap),
        out_shape=jax.ShapeDtypeStruct((ne, cap + pad, d), F32),
        grid_spec=grid_spec,
        compiler_params=_cp("arbitrary", "arbitrary"),
        name="expert_ffn",
    )(idx.reshape(-1), h2, wg, wu, wd)


def _combine_kernel(st_ref, x_ref, aff_ref, slot_ref, y_hbm, o_ref, buf_ref, xbuf_ref, sem, xsem,
                    *, ne, rows, nt):
    t = pl.program_id(0)
    tm = x_ref.shape[0]

    def chunk_start(tt, e, c):
        st8 = (st_ref[tt * ne + e] // F32_SUBLANES) * F32_SUBLANES
        return pl.multiple_of(st8 + c * rows, F32_SUBLANES)

    def first_copy(tt, e, slot):
        return pltpu.make_async_copy(y_hbm.at[e, pl.ds(chunk_start(tt, e, 0), rows)],
                                     buf_ref.at[slot, e], sem.at[slot, e])

    @pl.when(t == 0)
    def _():
        for e in range(ne):
            first_copy(0, e, 0).start()

    @pl.when(t + 1 < nt)
    def _():
        for e in range(ne):
            first_copy(t + 1, e, (t + 1) % 2).start()

    slot = t % 2
    for e in range(ne):
        first_copy(t, e, slot).wait()

    pair = V7X_LANES // rows
    lane = lax.broadcasted_iota(I32, (1, V7X_LANES), 1)
    hi_parts, lo_parts = [], []
    for a in range(ne // pair):
        rel = jnp.zeros((tm, V7X_LANES), I32)
        gate = jnp.zeros((tm, V7X_LANES), F32)
        for b in range(pair):
            e = a * pair + b
            in_e = jnp.logical_and(lane >= b * rows, lane < (b + 1) * rows)
            rel = jnp.where(in_e, slot_ref[:, e:e + 1] - chunk_start(t, e, 0) + b * rows, rel)
            gate = jnp.where(in_e, aff_ref[:, e:e + 1], gate)
        gate = jnp.where(rel == lane, gate, 0.0)
        g_hi = gate.astype(BF16)
        hi_parts.append(g_hi)
        lo_parts.append((gate - g_hi.astype(F32)).astype(BF16))
    rhs = buf_ref[slot].reshape(ne * rows, x_ref.shape[1]).astype(BF16)
    o_ref[...] = (x_ref[...]
                  + jnp.dot(jnp.concatenate(hi_parts, axis=1), rhs, preferred_element_type=F32)
                  + jnp.dot(jnp.concatenate(lo_parts, axis=1), rhs, preferred_element_type=F32))

    lane_r = lax.broadcasted_iota(I32, (1, rows), 1)
    ends = [st_ref[(t + 1) * ne + e] for e in range(ne)]
    any_extra = functools.reduce(jnp.logical_or, [ends[e] > chunk_start(t, e, 1) for e in range(ne)])

    @pl.when(any_extra)
    def _():
        for e in range(ne):
            for c in range(1, tm // rows + 1):

                @pl.when(ends[e] > chunk_start(t, e, c))
                def _():
                    cp = pltpu.make_async_copy(y_hbm.at[e, pl.ds(chunk_start(t, e, c), rows)],
                                               xbuf_ref, xsem)
                    cp.start()
                    cp.wait()
                    rel = slot_ref[:, e:e + 1] - chunk_start(t, e, c)
                    oh = jnp.where(rel == lane_r, 1.0, 0.0).astype(BF16)
                    contrib = jnp.dot(oh, xbuf_ref[...].astype(BF16), preferred_element_type=F32)
                    o_ref[...] += aff_ref[:, e:e + 1] * contrib


def moe_combine(starts, x1, aff, slot, ye, *, tm, rows):
    s, d = x1.shape
    ne = aff.shape[1]
    nt = s // tm
    assert V7X_LANES % rows == 0 and ne % (V7X_LANES // rows) == 0
    grid_spec = pltpu.PrefetchScalarGridSpec(
        num_scalar_prefetch=1,
        grid=(nt,),
        in_specs=[pl.BlockSpec((tm, d), lambda t, st: (t, 0)),
                  pl.BlockSpec((tm, ne), lambda t, st: (t, 0)),
                  pl.BlockSpec((tm, ne), lambda t, st: (t, 0)),
                  pl.BlockSpec(memory_space=pl.ANY)],
        out_specs=pl.BlockSpec((tm, d), lambda t, st: (t, 0)),
        scratch_shapes=[pltpu.VMEM((2, ne, rows, d), F32), pltpu.VMEM((rows, d), F32),
                        pltpu.SemaphoreType.DMA((2, ne)), pltpu.SemaphoreType.DMA(())],
    )
    return pl.pallas_call(
        functools.partial(_combine_kernel, ne=ne, rows=rows, nt=nt),
        out_shape=jax.ShapeDtypeStruct((s, d), F32),
        grid_spec=grid_spec,
        compiler_params=_cp("arbitrary"),
        name="moe_combine",
    )(starts.reshape(-1), x1, aff, slot, ye)


TM_PROJ, TN_PROJ = 1024, 1024
TM_ROW = 256
TM_OUT = 512
ATTN_TILE = 512
SEL_BLK = 256
FFN_CHUNK = 256
COMBINE_ROWS = 64


def _layer(x, l, p, bias, far_bias):
    s = x.shape[0]
    lam_init = 0.8 - 0.6 * math.exp(-0.3 * l)
    proj, xcat = norm_matmul(x, p["norm_mix"], p["w_in"], l, tm=min(TM_PROJ, s), tn=TN_PROJ)

    ycat = s5_mixer(xcat, p["s5_ops"], p["ssm_d"], l)
    yg = glu(ycat, p["w_glu"], l, tm=min(TM_PROJ, s))

    qn, k4, vt4, kmax = attn_prep(proj, p["q_gain"], p["k_gain"], tk=ATTN_TILE)
    lam = (jnp.exp(jnp.sum(p["lambda_q1"].astype(F32) * p["lambda_k1"].astype(F32)))
           - jnp.exp(jnp.sum(p["lambda_q2"].astype(F32) * p["lambda_k2"].astype(F32))) + lam_init)
    ao = diff_attention(qn, k4, vt4, bias, far_bias, kmax, lam, p["subln_gain"], lam_init=lam_init,
                        tq=ATTN_TILE, tk=ATTN_TILE)

    merged = gated_merge(yg, ao, proj, p["w_ssm_branch"], p["w_attn_branch"], l,
                         tm=min(TM_PROJ, s), tn=TN_PROJ)
    x1, h2, afft = out_proj_router(merged, x, p["w_out"].astype(BF16), p["norm_ffn"], p["w_router"],
                                   tm=min(TM_OUT, s))

    cap = CAPACITY_FACTOR * s // N_EXPERTS
    slot_t, cum_t, idx = expert_select(afft, cap=cap, blk=SEL_BLK)
    ye = expert_ffn(idx, h2, p["w_expert_gate"], p["w_expert_up"], p["w_expert_down"], l,
                    cap=cap, pad=COMBINE_ROWS, fc=FFN_CHUNK)
    starts = jnp.concatenate([cum_t[:, ::TM_ROW].T, jnp.full((1, N_EXPERTS), cap, I32)], axis=0)
    return moe_combine(starts, x1, afft.T, slot_t.T, ye, tm=TM_ROW, rows=COMBINE_ROWS)


_LAYER_PARAMS = ("ssm_d", "q_gain", "k_gain", "lambda_q1",
                 "lambda_k1", "lambda_q2", "lambda_k2", "subln_gain", "w_out",
                 "norm_mix", "norm_ffn", "w_router")
_STACKED_PARAMS = ("w_in", "w_expert_gate", "w_expert_up", "w_expert_down", "w_glu", "w_ssm_branch",
                   "w_attn_branch")


def kernel(x, w_in, ssm_a_re, ssm_a_im, ssm_log_dt, ssm_b_re, ssm_b_im, ssm_c_re, ssm_c_im, ssm_d, w_glu, w_ssm_branch, q_gain, k_gain, lambda_q1, lambda_k1, lambda_q2, lambda_k2, subln_gain, w_attn_branch, rel_bias, w_out, norm_mix, norm_ffn, w_router, w_expert_gate, w_expert_up, w_expert_down):
    args = dict(locals())
    b = x.shape[0]
    bias = bias_tiles(rel_bias, tk=ATTN_TILE, tq=ATTN_TILE)
    far_bias = attn_bias_consts(rel_bias)
    s5_ops = jax.vmap(s5_operators)(ssm_a_re, ssm_a_im, ssm_log_dt, ssm_b_re, ssm_b_im, ssm_c_re, ssm_c_im)
    outs = []
    for bi in range(b):
        xb = x[bi].astype(F32)
        for l in range(DEPTH):
            p = {k: args[k][l] for k in _LAYER_PARAMS}
            p.update({k: args[k] for k in _STACKED_PARAMS})
            p["s5_ops"] = s5_ops
            xb = _layer(xb, l, p, bias, far_bias)
        outs.append(xb)
    return jnp.stack(outs, axis=0).astype(x.dtype)
```
